```python
import jax, jax.numpy as jnp
from jax import lax
import numpy as np

D_MODEL = 1024
BATCH = 4
SEQ = 4096
DEPTH = 2
DEC_BATCH = 128
DEC_SEQ = 4
PAST_LEN = 16384
PAGE_SIZE = 128

ATT_HEADS = 8
ATT_KV_HEADS = 2
HEAD_DIM = 64
GROUP = ATT_HEADS // ATT_KV_HEADS
ATT_WIDTH = ATT_HEADS * HEAD_DIM
KV_WIDTH = ATT_KV_HEADS * HEAD_DIM
WINDOW = 128
Q_BLOCK = 128
ROT_DIM = HEAD_DIM // 4
ROPE_THETA = 500000.0
HG_HEADS = 4
HG_DK = 128
HG_DV = 128
HG_KEY_WIDTH = HG_HEADS * HG_DK
HG_VAL_WIDTH = HG_HEADS * HG_DV
HG_CHUNK = 32
MIX_WIDTH = ATT_WIDTH + HG_VAL_WIDTH
IN_WIDTH = ATT_WIDTH + 2 * KV_WIDTH + 2 * HG_KEY_WIDTH + 2 * HG_VAL_WIDTH
D_FF = 4 * D_MODEL
EPS = 1e-6

kernel_name = 'hymba_swa_sink_hgrn2_decoder_step'


def rms_norm(x, g):
    xf = x.astype(jnp.float32)
    y = xf * lax.rsqrt(jnp.mean(xf * xf, axis=-1, keepdims=True) + EPS)
    return (y * g.astype(jnp.float32)).astype(x.dtype)


def partial_rotary(x, pos):
    half = ROT_DIM // 2
    inv_freq = jnp.power(ROPE_THETA, -jnp.arange(half, dtype=jnp.float32) * (2.0 / ROT_DIM))
    ang = pos.astype(jnp.float32)[:, None] * inv_freq[None, :]
    cos = jnp.cos(ang)[None, :, None, :]
    sin = jnp.sin(ang)[None, :, None, :]
    xf = x.astype(jnp.float32)
    x1 = xf[..., :half]
    x2 = xf[..., half:ROT_DIM]
    out = jnp.concatenate([x1 * cos - x2 * sin, x2 * cos + x1 * sin, xf[..., ROT_DIM:]], axis=-1)
    return out.astype(x.dtype)


def window_attention(q, k_full, v_full, n_prefix_valid, sinks):
    B, L = q.shape[0], q.shape[1]
    qb = min(Q_BLOCK, L)
    n_blk = L // qb
    span = qb + WINDOW
    key_idx = jnp.arange(n_blk)[:, None] * qb + jnp.arange(span)[None, :]
    kb = k_full[:, key_idx]
    vb = v_full[:, key_idx]
    qg = q.reshape(B, n_blk, qb, ATT_KV_HEADS, GROUP, HEAD_DIM)
    scores = jnp.einsum('bnqkgd,bnskd->bnkgqs', qg, kb,
                        preferred_element_type=jnp.float32) * (HEAD_DIM ** -0.5)
    q_idx = WINDOW + jnp.arange(n_blk)[:, None] * qb + jnp.arange(qb)[None, :]
    dist = q_idx[:, :, None] - key_idx[:, None, :]
    allowed = (dist >= 0) & (dist <= WINDOW) & (key_idx[:, None, :] >= WINDOW - n_prefix_valid)
    scores = jnp.where(allowed[None, :, None, None, :, :], scores, -jnp.inf)
    sink = sinks.astype(jnp.float32).reshape(ATT_KV_HEADS, GROUP)[None, None, :, :, None, None]
    m = jnp.maximum(jnp.max(scores, axis=-1, keepdims=True), sink)
    p = jnp.exp(scores - m)
    denom = jnp.sum(p, axis=-1, keepdims=True) + jnp.exp(sink - m)
    out = jnp.einsum('bnkgqs,bnskd->bnqkgd', (p / denom).astype(v_full.dtype), vb)
    return out.reshape(B, L, ATT_WIDTH)


def hgrn2_recurrence(q, k, v, log_f, s0):
    B, L = q.shape[0], q.shape[1]
    c = min(HG_CHUNK, L)
    n = -(-L // c)
    pad = n * c - L

    def prep(a):
        a = jnp.pad(a.astype(jnp.float32), ((0, 0), (0, pad), (0, 0), (0, 0)))
        return a.reshape(B, n, c, HG_HEADS, a.shape[-1]).transpose(1, 0, 3, 2, 4)

    qc, kc, vc, gc = prep(q), prep(k), prep(v), prep(log_f)
    causal = jnp.tril(jnp.ones((c, c), dtype=bool))

    def step(S, inp):
        qi, ki, vi, gi = inp
        b = jnp.cumsum(gi, axis=2)
        o_inter = jnp.einsum('bhtd,bhde->bhte', qi * jnp.exp(b), S)
        diff = b[:, :, :, None, :] - b[:, :, None, :, :]
        decay = jnp.exp(jnp.where(causal[None, None, :, :, None], diff, -jnp.inf))
        a = jnp.einsum('bhtd,bhsd,bhtsd->bhts', qi, ki, decay)
        o = o_inter + jnp.einsum('bhts,bhse->bhte', a, vi)
        b_last = b[:, :, -1:, :]
        S_new = jnp.exp(b_last[:, :, 0, :, None]) * S + \
            jnp.einsum('bhsd,bhse->bhde', ki * jnp.exp(b_last - b), vi)
        return S_new, o

    S_fin, o = lax.scan(step, s0.astype(jnp.float32), (qc, kc, vc, gc))
    o = o.transpose(1, 0, 3, 2, 4).reshape(B, n * c, HG_HEADS, HG_DV)[:, :L]
    return o, S_fin


def mixer(h, k_prefix, v_prefix, s0, pos0, n_prefix_valid, w_in_l, sinks_l, lb_l, out_norm_l):
    B, L, _ = h.shape
    proj = jnp.einsum('bld,de->ble', h, w_in_l)
    widths = [ATT_WIDTH, KV_WIDTH, KV_WIDTH, HG_KEY_WIDTH, HG_KEY_WIDTH, HG_VAL_WIDTH]
    splits = []
    acc = 0
    for wd in widths:
        acc += wd
        splits.append(acc)
    q_a, k_a, v_a, q_h, f_h, i_h, g_h = jnp.split(proj, splits, axis=-1)

    pos = pos0 + jnp.arange(L, dtype=jnp.int32)
    q_a = partial_rotary(q_a.reshape(B, L, ATT_HEADS, HEAD_DIM), pos)
    k_a = partial_rotary(k_a.reshape(B, L, ATT_KV_HEADS, HEAD_DIM), pos)
    v_a = v_a.reshape(B, L, ATT_KV_HEADS, HEAD_DIM)
    k_full = jnp.concatenate([k_prefix.astype(k_a.dtype), k_a], axis=1)
    v_full = jnp.concatenate([v_prefix.astype(v_a.dtype), v_a], axis=1)
    att_out = window_attention(q_a, k_full, v_full, n_prefix_valid, sinks_l)
    new_k = k_full[:, -WINDOW:]
    new_v = v_full[:, -WINDOW:]

    z = f_h.astype(jnp.float32).reshape(B, L, HG_HEADS, HG_DK)
    lb = lb_l.reshape(HG_HEADS, HG_DK)
    log_f = jnp.logaddexp(jnp.log(lb), jnp.log1p(-lb) + jax.nn.log_sigmoid(z))
    k_in = (1.0 - lb) * jax.nn.sigmoid(-z)
    q_in = jax.nn.silu(q_h.astype(jnp.float32)).reshape(B, L, HG_HEADS, HG_DK)
    v_in = i_h.reshape(B, L, HG_HEADS, HG_DV)
    o, S_fin = hgrn2_recurrence(q_in, k_in, v_in, log_f, s0)
    gate = jax.nn.silu(g_h.astype(jnp.float32)).reshape(B, L, HG_HEADS, HG_DV)
    hg_out = (rms_norm(o, out_norm_l) * gate).reshape(B, L, HG_VAL_WIDTH)

    mix = jnp.concatenate([att_out.astype(h.dtype), hg_out.astype(h.dtype)], axis=-1)
    return mix, new_k, new_v, S_fin.astype(s0.dtype)


def trunk(x, k_bufs, v_bufs, states, pos0, n_prefix_valid, attn_norm, w_in, att_sinks,
          lower_bounds, hgrn_out_norm, w_o, mlp_norm, w_up, w_down, final_norm):
    new_ks, new_vs, new_ss = [], [], []
    for l in range(DEPTH):
        h = rms_norm(x, attn_norm[l])
        mix, nk, nv, ns = mixer(h, k_bufs[l], v_bufs[l], states[l], pos0, n_prefix_valid,
                                w_in[l], att_sinks[l], lower_bounds[l], hgrn_out_norm[l])
        x = x + jnp.einsum('ble,ed->bld', mix, w_o[l])
        h = rms_norm(x, mlp_norm[l])
        u = jnp.square(jax.nn.relu(jnp.einsum('bld,df->blf', h, w_up[l])))
        x = x + jnp.einsum('blf,fd->bld', u, w_down[l])
        new_ks.append(nk)
        new_vs.append(nv)
        new_ss.append(ns)
    return rms_norm(x, final_norm), jnp.stack(new_ks), jnp.stack(new_vs), jnp.stack(new_ss)


def setup_inputs(seed: int = 0) -> dict:
    key = jax.random.key(seed)
    ks = jax.random.split(key, 16)
    f32 = jnp.float32
    nrm = lambda k, shape: jax.random.normal(k, shape, dtype=f32)
    return {
        'x_prompt': nrm(ks[0], (BATCH, SEQ, D_MODEL)),
        'x_sample': nrm(ks[1], (DEC_BATCH, DEC_SEQ, D_MODEL)),
        'cache_k': nrm(ks[2], (DEPTH, DEC_BATCH, WINDOW, ATT_KV_HEADS, HEAD_DIM)),
        'cache_v': nrm(ks[3], (DEPTH, DEC_BATCH, WINDOW, ATT_KV_HEADS, HEAD_DIM)),
        'state_hgrn': 0.3 * nrm(ks[4], (DEPTH, DEC_BATCH, HG_HEADS, HG_DK, HG_DV)),
        'attn_norm': 1.0 + 0.02 * nrm(ks[5], (DEPTH, D_MODEL)),
        'w_in': nrm(ks[6], (DEPTH, D_MODEL, IN_WIDTH)) * D_MODEL ** -0.5,
        'att_sinks': 0.5 * nrm(ks[7], (DEPTH, ATT_HEADS)),
        'hgrn_lower_bounds': 0.1 * nrm(ks[8], (DEPTH, HG_KEY_WIDTH)),
        'hgrn_out_norm': 1.0 + 0.02 * nrm(ks[9], (DEPTH, HG_DV)),
        'w_o': nrm(ks[10], (DEPTH, MIX_WIDTH, D_MODEL)) * MIX_WIDTH ** -0.5,
        'mlp_norm': 1.0 + 0.02 * nrm(ks[11], (DEPTH, D_MODEL)),
        'w_up': nrm(ks[12], (DEPTH, D_MODEL, D_FF)) * D_MODEL ** -0.5,
        'w_down': nrm(ks[13], (DEPTH, D_FF, D_MODEL)) * D_FF ** -0.5,
        'final_norm': 1.0 + 0.02 * nrm(ks[14], (D_MODEL,)),
    }


def reference(x_prompt, x_sample, cache_k, cache_v, state_hgrn, attn_norm, w_in, att_sinks,
              hgrn_lower_bounds, hgrn_out_norm, w_o, mlp_norm, w_up, w_down, final_norm):
    p = jax.nn.softmax(hgrn_lower_bounds.astype(jnp.float32), axis=0)
    lower_bounds = jnp.maximum(jnp.cumsum(p, axis=0) - p[0:1], 0.0)
    weights = (attn_norm, w_in, att_sinks, lower_bounds, hgrn_out_norm, w_o, mlp_norm, w_up, w_down, final_norm)

    zero_kv = jnp.zeros((DEPTH, BATCH, WINDOW, ATT_KV_HEADS, HEAD_DIM), dtype=x_prompt.dtype)
    zero_s = jnp.zeros((DEPTH, BATCH, HG_HEADS, HG_DK, HG_DV), dtype=state_hgrn.dtype)
    y_prompt, nk_p, nv_p, ns_p = trunk(x_prompt, zero_kv, zero_kv, zero_s, 0, 0, *weights)

    y_sample, nk_s, nv_s, ns_s = trunk(x_sample, cache_k, cache_v, state_hgrn, PAST_LEN,
                                       min(WINDOW, PAST_LEN), *weights)
    return (y_prompt, y_sample, nk_p, nv_p, ns_p, nk_s, nv_s, ns_s)
```

```python
import functools

import jax
import jax.numpy as jnp
from jax import lax
from jax.experimental import pallas as pl
from jax.experimental.pallas import tpu as pltpu

F32 = jnp.float32
BF16 = jnp.bfloat16

D_MODEL = 1024
DEPTH = 2
PAST_LEN = 16384
ATT_HEADS = 8
ATT_KV_HEADS = 2
HEAD_DIM = 64
GROUP = ATT_HEADS // ATT_KV_HEADS
ATT_WIDTH = ATT_HEADS * HEAD_DIM
KV_WIDTH = ATT_KV_HEADS * HEAD_DIM
WINDOW = 128
ROT_DIM = HEAD_DIM // 4
ROT_HALF = ROT_DIM // 2
ROPE_THETA = 500000.0
HG_HEADS = 4
HG_DK = 128
HG_DV = 128
HG_WIDTH = HG_HEADS * HG_DK
MIX_WIDTH = ATT_WIDTH + HG_WIDTH
IN_WIDTH = ATT_WIDTH + 2 * KV_WIDTH + 4 * HG_WIDTH
D_FF = 4 * D_MODEL
EPS = 1e-6

OFF_Q = 0
OFF_K = ATT_WIDTH
OFF_V = OFF_K + KV_WIDTH
OFF_HQ = OFF_V + KV_WIDTH
OFF_HF = OFF_HQ + HG_WIDTH
OFF_HI = OFF_HF + HG_WIDTH
OFF_HG = OFF_HI + HG_WIDTH

LANES = 128
PROMPT_BLOCK = 256
HG_CHUNK = 128
HG_SUB = 32
FF_BLOCK = 1024
SAMPLE_SEQS = 8
SAMPLE_KEYS = 256
VMEM_LIMIT = 58 * 1024 * 1024


def _dot(a, b):
    return jnp.dot(a, b, preferred_element_type=F32)


def _dot_nt(a, b):
    return lax.dot_general(a, b, (((1,), (1,)), ((), ())), preferred_element_type=F32)


def _rms(x, g_row):
    ms = jnp.mean(x * x, axis=-1, keepdims=True)
    return (x * lax.rsqrt(ms + EPS)) * g_row


def _rope(x, cos, sin_dn, sin_up):
    return x * cos + pltpu.roll(x, ROT_HALF, 1) * sin_dn + pltpu.roll(x, LANES - ROT_HALF, 1) * sin_up


def _lower_bound(lb_all, layer):
    m = jnp.max(lb_all, axis=0, keepdims=True)
    e = jnp.exp(lb_all - m)
    p = e / jnp.sum(e, axis=0, keepdims=True)
    cs = p[0:1]
    for l in range(1, layer + 1):
        cs = cs + p[l:l + 1]
    return jnp.maximum(cs - p[0:1], 0.0)


def _hgrn_gates(z, lb):
    e = jnp.exp(-jnp.abs(z))
    log_sig = jnp.minimum(z, 0.0) - jnp.log1p(e)
    a1 = jnp.log(lb)
    a2 = jnp.log1p(-lb) + log_sig
    log_f = jnp.maximum(a1, a2) + jnp.log1p(jnp.exp(-jnp.abs(a1 - a2)))
    k_in = (1.0 - lb) * (jnp.where(z >= 0.0, e, 1.0) / (1.0 + e))
    return log_f, k_in


def _silu(x):
    return x / (1.0 + jnp.exp(-x))


def _softmax_sink_pv(s, allowed, sink, v_bf16):
    s = jnp.where(allowed, s, -jnp.inf)
    m = jnp.maximum(jnp.max(s, axis=-1, keepdims=True), sink)
    p = jnp.exp(s - m)
    denom = jnp.sum(p, axis=-1, keepdims=True) + jnp.exp(sink - m)
    return _dot(p.astype(BF16), v_bf16) / denom


def _cumsum_rows(tri, g):
    g1 = g.astype(BF16)
    r1 = g - g1.astype(F32)
    g2 = r1.astype(BF16)
    g3 = (r1 - g2.astype(F32)).astype(BF16)
    return _dot(tri, g1) + _dot(tri, g2) + _dot(tri, g3)


def _hgrn_chunk(q, k, v, b, st):
    c = q.shape[0]
    v_bf = v.astype(BF16)
    parts = []
    for i in range(c // HG_SUB):
        lo, hi = i * HG_SUB, (i + 1) * HG_SUB
        r = b[lo - 1:lo] if i > 0 else jnp.zeros((1, HG_DK), F32)
        qh = (q[lo:hi] * jnp.exp(b[lo:hi] - r)).astype(BF16)
        kh = (k[:hi] * jnp.exp(r - b[:hi])).astype(BF16)
        a = _dot_nt(qh, kh)
        row = lax.broadcasted_iota(jnp.int32, a.shape, 0)
        col = lax.broadcasted_iota(jnp.int32, a.shape, 1)
        a = jnp.where(col <= row + lo, a, 0.0)
        parts.append(_dot(a.astype(BF16), v_bf[:hi]))
    o_intra = jnp.concatenate(parts, axis=0)
    o_inter = _dot_nt((q * jnp.exp(b)).astype(BF16), st.astype(BF16))
    b_last = b[c - 1:c]
    k_hat = (k * jnp.exp(b_last - b)).astype(BF16)
    st_new = st * jnp.exp(b_last) + _dot(v.T.astype(BF16), k_hat)
    return o_intra + o_inter, st_new


def _dense_out(x, mix_bf16, wo_ref, mnorm_ref, wup_ref, wdn_ref, fnorm_ref, last):
    x1 = x + _dot(mix_bf16, wo_ref[...])
    h2 = _rms(x1, mnorm_ref[...]).astype(BF16)
    acc = x1
    for c in range(D_FF // FF_BLOCK):
        u = _dot(h2, wup_ref[:, c * FF_BLOCK:(c + 1) * FF_BLOCK])
        u = jnp.square(jnp.maximum(u, 0.0)).astype(BF16)
        acc = acc + _dot(u, wdn_ref[c * FF_BLOCK:(c + 1) * FF_BLOCK, :])
    if last:
        acc = _rms(acc, fnorm_ref[...])
    return acc


def _prompt_layer_kernel(layer, last,
                         x_ref, cos_ref, sdn_ref, sup_ref, tri_ref, sinks_ref, lb_ref, anorm_ref,
                         win_ref, onorm_ref, wo_ref, mnorm_ref, wup_ref, wdn_ref, fnorm_ref,
                         y_ref, nk_ref, nv_ref, ns_ref,
                         proj_ref, mix_ref, kprev_ref, vprev_ref, st_ref):
    t_blk = pl.program_id(1)
    n_blk = pl.num_programs(1)
    tb = PROMPT_BLOCK

    @pl.when(t_blk == 0)
    def _():
        kprev_ref[...] = jnp.zeros_like(kprev_ref)
        vprev_ref[...] = jnp.zeros_like(vprev_ref)
        st_ref[...] = jnp.zeros_like(st_ref)

    x = x_ref[0]
    h = _rms(x, anorm_ref[...]).astype(BF16)
    proj_ref[...] = _dot(h, win_ref[...])

    cos, sdn, sup = cos_ref[...], sdn_ref[...], sup_ref[...]

    k_rot = _rope(proj_ref[:, OFF_K:OFF_K + KV_WIDTH], cos, sdn, sup)
    v_new = proj_ref[:, OFF_V:OFF_V + KV_WIDTH]
    k_all = jnp.concatenate([kprev_ref[...], k_rot], axis=0).astype(BF16)
    v_all = jnp.concatenate([vprev_ref[...], v_new], axis=0).astype(BF16)
    kprev_ref[...] = k_rot[tb - WINDOW:]
    vprev_ref[...] = v_new[tb - WINDOW:]
    nk_ref[0] = k_rot[tb - WINDOW:]
    nv_ref[0] = v_new[tb - WINDOW:]

    row = lax.broadcasted_iota(jnp.int32, (WINDOW, 2 * WINDOW), 0)
    col = lax.broadcasted_iota(jnp.int32, (WINDOW, 2 * WINDOW), 1)
    in_window = (col >= row) & (col <= row + WINDOW)
    first_lo = jnp.where(t_blk == 0, WINDOW, 0)
    for hp in range(ATT_HEADS // 2):
        q_slab = _rope(proj_ref[:, OFF_Q + hp * LANES:OFF_Q + (hp + 1) * LANES], cos, sdn, sup)
        q_slab = (q_slab * (HEAD_DIM ** -0.5)).astype(BF16)
        outs = []
        for sub in range(2):
            head = 2 * hp + sub
            kvh = head // GROUP
            sink = sinks_ref[head]
            q_h = q_slab[:, sub * HEAD_DIM:(sub + 1) * HEAD_DIM]
            blocks = []
            for n in range(tb // WINDOW):
                kb = k_all[n * WINDOW:(n + 2) * WINDOW, kvh * HEAD_DIM:(kvh + 1) * HEAD_DIM]
                vb = v_all[n * WINDOW:(n + 2) * WINDOW, kvh * HEAD_DIM:(kvh + 1) * HEAD_DIM]
                s = _dot_nt(q_h[n * WINDOW:(n + 1) * WINDOW], kb)
                allowed = in_window & (col >= first_lo) if n == 0 else in_window
                blocks.append(_softmax_sink_pv(s, allowed, sink, vb))
            outs.append(jnp.concatenate(blocks, axis=0))
        mix_ref[:, hp * LANES:(hp + 1) * LANES] = jnp.concatenate(outs, axis=1).astype(BF16)

    lb = _lower_bound(lb_ref[...], layer)
    log_f, k_in = _hgrn_gates(proj_ref[:, OFF_HF:OFF_HF + HG_WIDTH], lb)
    b_all = _cumsum_rows(tri_ref[...], log_f)
    onorm = onorm_ref[...]
    for hh in range(HG_HEADS):
        sl = slice(hh * HG_DK, (hh + 1) * HG_DK)
        q_in = _silu(proj_ref[:, OFF_HQ + hh * HG_DK:OFF_HQ + (hh + 1) * HG_DK])
        v_in = proj_ref[:, OFF_HI + hh * HG_DV:OFF_HI + (hh + 1) * HG_DV]
        st = st_ref[hh]
        o_chunks = []
        for c in range(tb // HG_CHUNK):
            rs = slice(c * HG_CHUNK, (c + 1) * HG_CHUNK)
            o_c, st = _hgrn_chunk(q_in[rs], k_in[rs, sl], v_in[rs], b_all[rs, sl], st)
            o_chunks.append(o_c)
        st_ref[hh] = st
        o = jnp.concatenate(o_chunks, axis=0)
        gate = _silu(proj_ref[:, OFF_HG + hh * HG_DV:OFF_HG + (hh + 1) * HG_DV])
        mix_ref[:, ATT_WIDTH + hh * HG_DV:ATT_WIDTH + (hh + 1) * HG_DV] = (
            _rms(o, onorm) * gate).astype(BF16)

    @pl.when(t_blk == n_blk - 1)
    def _():
        for hh in range(HG_HEADS):
            ns_ref[0, hh] = st_ref[hh].T

    y_ref[0] = _dense_out(x, mix_ref[...], wo_ref, mnorm_ref, wup_ref, wdn_ref, fnorm_ref, last)


def _const_spec(shape):
    nd = len(shape)
    return pl.BlockSpec(shape, lambda *_: (0,) * nd, pipeline_mode=pl.Buffered(1))


def _prompt_layer(layer, x, rope, tri, sinks, lb_all, anorm, w_in, onorm, w_o, mnorm, w_up, w_dn, fnorm):
    batch, seq, _ = x.shape
    tb = PROMPT_BLOCK
    last = layer == DEPTH - 1
    rope_spec = pl.BlockSpec((tb, LANES), lambda b, t: (t, 0))
    carry_spec = pl.BlockSpec((1, WINDOW, KV_WIDTH), lambda b, t: (b, 0, 0))
    return pl.pallas_call(
        functools.partial(_prompt_layer_kernel, layer, last),
        grid=(batch, seq // tb),
        in_specs=[
            pl.BlockSpec((1, tb, D_MODEL), lambda b, t: (b, t, 0)),
            rope_spec, rope_spec, rope_spec,
            _const_spec((tb, tb)),
            pl.BlockSpec(memory_space=pltpu.SMEM),
            _const_spec((DEPTH, HG_WIDTH)),
            _const_spec((1, D_MODEL)),
            _const_spec((D_MODEL, IN_WIDTH)),
            _const_spec((1, HG_DV)),
            _const_spec((MIX_WIDTH, D_MODEL)),
            _const_spec((1, D_MODEL)),
            _const_spec((D_MODEL, D_FF)),
            _const_spec((D_FF, D_MODEL)),
            _const_spec((1, D_MODEL)),
        ],
        out_specs=[
            pl.BlockSpec((1, tb, D_MODEL), lambda b, t: (b, t, 0)),
            carry_spec, carry_spec,
            pl.BlockSpec((1, HG_HEADS, HG_DK, HG_DV), lambda b, t: (b, 0, 0, 0)),
        ],
        out_shape=[
            jax.ShapeDtypeStruct((batch, seq, D_MODEL), F32),
            jax.ShapeDtypeStruct((batch, WINDOW, KV_WIDTH), F32),
            jax.ShapeDtypeStruct((batch, WINDOW, KV_WIDTH), F32),
            jax.ShapeDtypeStruct((batch, HG_HEADS, HG_DK, HG_DV), F32),
        ],
        scratch_shapes=[
            pltpu.VMEM((tb, IN_WIDTH), F32),
            pltpu.VMEM((tb, MIX_WIDTH), BF16),
            pltpu.VMEM((WINDOW, KV_WIDTH), F32),
            pltpu.VMEM((WINDOW, KV_WIDTH), F32),
            pltpu.VMEM((HG_HEADS, HG_DV, HG_DK), F32),
        ],
        compiler_params=pltpu.CompilerParams(
            dimension_semantics=("arbitrary", "arbitrary"), vmem_limit_bytes=VMEM_LIMIT),
        name=f"prompt_layer{layer}",
    )(x, *rope, tri, sinks, lb_all, anorm, w_in, onorm, w_o, mnorm, w_up, w_dn, fnorm)


def _sample_in_kernel(x_ref, anorm_ref, win_ref, proj_ref):
    h = _rms(x_ref[...], anorm_ref[...]).astype(BF16)
    proj_ref[...] = _dot(h, win_ref[...])


def _sample_out_kernel(last, x_ref, mix_ref, wo_ref, mnorm_ref, wup_ref, wdn_ref, fnorm_ref, y_ref):
    y_ref[...] = _dense_out(x_ref[...], mix_ref[...].astype(BF16), wo_ref, mnorm_ref, wup_ref,
                            wdn_ref, fnorm_ref, last)


def _sample_mixer_kernel(layer,
                         p_ref, cos_ref, sdn_ref, sup_ref, ck_ref, cv_ref, s0_ref, sinks_ref, lb_ref,
                         onorm_ref,
                         mix_ref, nk_ref, nv_ref, ns_ref,
                         kf_ref, vf_ref):
    n_new = p_ref.shape[1]
    kf_ref[...] = jnp.zeros_like(kf_ref)
    vf_ref[...] = jnp.zeros_like(vf_ref)
    cos, sdn, sup = cos_ref[...], sdn_ref[...], sup_ref[...]
    lb = _lower_bound(lb_ref[...], layer)
    onorm = onorm_ref[...]
    row = lax.broadcasted_iota(jnp.int32, (n_new, SAMPLE_KEYS), 0)
    col = lax.broadcasted_iota(jnp.int32, (n_new, SAMPLE_KEYS), 1)
    allowed = (col >= row) & (col <= row + WINDOW)
    trow = lax.broadcasted_iota(jnp.int32, (n_new, 1), 0)

    def per_seq(s, carry):
        k_rot = _rope(p_ref[s, :, OFF_K:OFF_K + KV_WIDTH], cos, sdn, sup)
        v_new = p_ref[s, :, OFF_V:OFF_V + KV_WIDTH]
        kf_ref[0:WINDOW] = ck_ref[s]
        vf_ref[0:WINDOW] = cv_ref[s]
        kf_ref[WINDOW:WINDOW + n_new] = k_rot
        vf_ref[WINDOW:WINDOW + n_new] = v_new
        nk_ref[s] = kf_ref[n_new:n_new + WINDOW]
        nv_ref[s] = vf_ref[n_new:n_new + WINDOW]
        k_all = kf_ref[...].astype(BF16)
        v_all = vf_ref[...].astype(BF16)
        for hp in range(ATT_HEADS // 2):
            q_slab = _rope(p_ref[s, :, OFF_Q + hp * LANES:OFF_Q + (hp + 1) * LANES], cos, sdn, sup)
            q_slab = (q_slab * (HEAD_DIM ** -0.5)).astype(BF16)
            outs = []
            for sub in range(2):
                head = 2 * hp + sub
                kvh = head // GROUP
                kb = k_all[:, kvh * HEAD_DIM:(kvh + 1) * HEAD_DIM]
                vb = v_all[:, kvh * HEAD_DIM:(kvh + 1) * HEAD_DIM]
                sc = _dot_nt(q_slab[:, sub * HEAD_DIM:(sub + 1) * HEAD_DIM], kb)
                outs.append(_softmax_sink_pv(sc, allowed, sinks_ref[head], vb))
            mix_ref[s, :, hp * LANES:(hp + 1) * LANES] = jnp.concatenate(outs, axis=1)

        log_f, k_in = _hgrn_gates(p_ref[s, :, OFF_HF:OFF_HF + HG_WIDTH], lb)
        for hh in range(HG_HEADS):
            sl = slice(hh * HG_DK, (hh + 1) * HG_DK)
            q_in = _silu(p_ref[s, :, OFF_HQ + hh * HG_DK:OFF_HQ + (hh + 1) * HG_DK])
            v_in = p_ref[s, :, OFF_HI + hh * HG_DV:OFF_HI + (hh + 1) * HG_DV]
            g = log_f[:, sl]
            k = k_in[:, sl]
            rows = [g[0:1]]
            for t in range(1, n_new):
                rows.append(rows[-1] + g[t:t + 1])
            b = jnp.concatenate(rows, axis=0)
            st = s0_ref[s, hh]
            o = _dot((q_in * jnp.exp(b)).astype(BF16), st.astype(BF16))
            for j in range(n_new):
                decay = jnp.exp(jnp.minimum(b - rows[j], 0.0))
                a_col = jnp.sum(q_in * decay * k[j:j + 1], axis=-1, keepdims=True)
                o = o + jnp.where(trow >= j, a_col, 0.0) * v_in[j:j + 1]
            b_last = rows[-1]
            k_hat = k * jnp.exp(b_last - b)
            upd = lax.dot_general(k_hat.astype(BF16), v_in.astype(BF16), (((0,), (0,)), ((), ())),
                                  preferred_element_type=F32)
            ns_ref[s, hh] = st * jnp.exp(b_last).T + upd
            gate = _silu(p_ref[s, :, OFF_HG + hh * HG_DV:OFF_HG + (hh + 1) * HG_DV])
            mix_ref[s, :, ATT_WIDTH + hh * HG_DV:ATT_WIDTH + (hh + 1) * HG_DV] = _rms(o, onorm) * gate
        return carry

    lax.fori_loop(0, p_ref.shape[0], per_seq, 0)


def _sample_layer(layer, x, rope, cache_k, cache_v, state, sinks, lb_all, anorm, w_in, onorm, w_o, mnorm,
                  w_up, w_dn, fnorm):
    n_seq, n_new, _ = x.shape
    n_tok = n_seq * n_new
    sb = SAMPLE_SEQS
    last = layer == DEPTH - 1
    x2 = x.reshape(n_tok, D_MODEL)
    params = pltpu.CompilerParams(vmem_limit_bytes=VMEM_LIMIT)

    proj = pl.pallas_call(
        _sample_in_kernel,
        out_shape=jax.ShapeDtypeStruct((n_tok, IN_WIDTH), F32),
        compiler_params=params,
        name=f"sample_in{layer}",
    )(x2, anorm, w_in)

    seq_spec = lambda *tail: pl.BlockSpec((sb,) + tail, lambda i: (i,) + (0,) * len(tail))
    rope_spec = _const_spec((n_new, LANES))
    mix, nk, nv, ns = pl.pallas_call(
        functools.partial(_sample_mixer_kernel, layer),
        grid=(n_seq // sb,),
        in_specs=[
            seq_spec(n_new, IN_WIDTH),
            rope_spec, rope_spec, rope_spec,
            seq_spec(WINDOW, KV_WIDTH), seq_spec(WINDOW, KV_WIDTH),
            seq_spec(HG_HEADS, HG_DK, HG_DV),
            pl.BlockSpec(memory_space=pltpu.SMEM),
            _const_spec((DEPTH, HG_WIDTH)),
            _const_spec((1, HG_DV)),
        ],
        out_specs=[
            seq_spec(n_new, MIX_WIDTH),
            seq_spec(WINDOW, KV_WIDTH), seq_spec(WINDOW, KV_WIDTH),
            seq_spec(HG_HEADS, HG_DK, HG_DV),
        ],
        out_shape=[
            jax.ShapeDtypeStruct((n_seq, n_new, MIX_WIDTH), F32),
            jax.ShapeDtypeStruct((n_seq, WINDOW, KV_WIDTH), F32),
            jax.ShapeDtypeStruct((n_seq, WINDOW, KV_WIDTH), F32),
            jax.ShapeDtypeStruct((n_seq, HG_HEADS, HG_DK, HG_DV), F32),
        ],
        scratch_shapes=[
            pltpu.VMEM((SAMPLE_KEYS, KV_WIDTH), F32),
            pltpu.VMEM((SAMPLE_KEYS, KV_WIDTH), F32),
        ],
        compiler_params=pltpu.CompilerParams(
            dimension_semantics=("arbitrary",), vmem_limit_bytes=VMEM_LIMIT),
        name=f"sample_mixer{layer}",
    )(proj.reshape(n_seq, n_new, IN_WIDTH), *rope, cache_k, cache_v, state, sinks, lb_all, onorm)

    y = pl.pallas_call(
        functools.partial(_sample_out_kernel, last),
        out_shape=jax.ShapeDtypeStruct((n_tok, D_MODEL), F32),
        compiler_params=params,
        name=f"sample_out{layer}",
    )(x2, mix.reshape(n_tok, MIX_WIDTH), w_o, mnorm, w_up, w_dn, fnorm)
    return y.reshape(n_seq, n_new, D_MODEL), nk, nv, ns


def _rope_tables(pos):
    n = pos.shape[0]
    inv_freq = jnp.power(ROPE_THETA, -jnp.arange(ROT_HALF, dtype=F32) * (2.0 / ROT_DIM))
    ang = pos.astype(F32)[:, None] * inv_freq[None, :]
    cos, sin = jnp.cos(ang), jnp.sin(ang)
    rest = HEAD_DIM - ROT_DIM
    zeros_h = jnp.zeros((n, ROT_HALF), F32)
    cos_t = jnp.concatenate([cos, cos, jnp.ones((n, rest), F32)], axis=1)
    sdn_t = jnp.concatenate([zeros_h, sin, jnp.zeros((n, rest), F32)], axis=1)
    sup_t = jnp.concatenate([-sin, zeros_h, jnp.zeros((n, rest), F32)], axis=1)
    reps = LANES // HEAD_DIM
    return tuple(jnp.tile(t, (1, reps)) for t in (cos_t, sdn_t, sup_t))


def _chunk_tri(n, chunk):
    r = jnp.arange(n)
    same = (r[:, None] // chunk) == (r[None, :] // chunk)
    return (same & (r[None, :] <= r[:, None])).astype(BF16)


def kernel(x_prompt, x_sample, cache_k, cache_v, state_hgrn, attn_norm, w_in, att_sinks, hgrn_lower_bounds,
           hgrn_out_norm, w_o, mlp_norm, w_up, w_down, final_norm):
    batch, seq, _ = x_prompt.shape
    n_seq, n_new, _ = x_sample.shape
    assert seq % PROMPT_BLOCK == 0 and PROMPT_BLOCK % HG_CHUNK == 0 and n_seq % SAMPLE_SEQS == 0
    assert n_new + WINDOW <= SAMPLE_KEYS

    rope_p = _rope_tables(jnp.arange(seq, dtype=jnp.int32))
    rope_s = _rope_tables(PAST_LEN + jnp.arange(n_new, dtype=jnp.int32))
    tri = _chunk_tri(PROMPT_BLOCK, HG_CHUNK)
    lb_all = hgrn_lower_bounds.astype(F32)
    fnorm = final_norm.reshape(1, D_MODEL)

    xp, xs = x_prompt, x_sample
    nk_p, nv_p, ns_p, nk_s, nv_s, ns_s = [], [], [], [], [], []
    for l in range(DEPTH):
        weights = (attn_norm[l].reshape(1, D_MODEL), w_in[l].astype(BF16), hgrn_out_norm[l].reshape(1, HG_DV),
                   w_o[l].astype(BF16), mlp_norm[l].reshape(1, D_MODEL), w_up[l].astype(BF16),
                   w_down[l].astype(BF16), fnorm)
        xp, k1, v1, s1 = _prompt_layer(l, xp, rope_p, tri, att_sinks[l], lb_all, *weights)
        xs, k2, v2, s2 = _sample_layer(
            l, xs, rope_s, cache_k[l].reshape(n_seq, WINDOW, KV_WIDTH),
            cache_v[l].reshape(n_seq, WINDOW, KV_WIDTH), state_hgrn[l], att_sinks[l], lb_all, *weights)
        nk_p.append(k1), nv_p.append(v1), ns_p.append(s1)
        nk_s.append(k2), nv_s.append(v2), ns_s.append(s2)

    kv_p = (DEPTH, batch, WINDOW, ATT_KV_HEADS, HEAD_DIM)
    kv_s = (DEPTH, n_seq, WINDOW, ATT_KV_HEADS, HEAD_DIM)
    return (xp, xs,
            jnp.stack(nk_p).reshape(kv_p), jnp.stack(nv_p).reshape(kv_p), jnp.stack(ns_p),
            jnp.stack(nk_s).reshape(kv_s), jnp.stack(nv_s).reshape(kv_s), jnp.stack(ns_s))
```

```python
import functools

import jax
import jax.numpy as jnp
from jax import lax
from jax.experimental import pallas as pl
from jax.experimental.pallas import tpu as pltpu

F32 = jnp.float32
BF16 = jnp.bfloat16

D_MODEL = 1024
DEPTH = 2
PAST_LEN = 16384
ATT_HEADS = 8
ATT_KV_HEADS = 2
HEAD_DIM = 64
GROUP = ATT_HEADS // ATT_KV_HEADS
ATT_WIDTH = ATT_HEADS * HEAD_DIM
KV_WIDTH = ATT_KV_HEADS * HEAD_DIM
WINDOW = 128
ROT_DIM = HEAD_DIM // 4
ROT_HALF = ROT_DIM // 2
ROPE_THETA = 500000.0
HG_HEADS = 4
HG_DK = 128
HG_DV = 128
HG_WIDTH = HG_HEADS * HG_DK
MIX_WIDTH = ATT_WIDTH + HG_WIDTH
IN_WIDTH = ATT_WIDTH + 2 * KV_WIDTH + 4 * HG_WIDTH
D_FF = 4 * D_MODEL
EPS = 1e-6

OFF_Q = 0
OFF_K = ATT_WIDTH
OFF_V = OFF_K + KV_WIDTH
OFF_HQ = OFF_V + KV_WIDTH
OFF_HF = OFF_HQ + HG_WIDTH
OFF_HI = OFF_HF + HG_WIDTH
OFF_HG = OFF_HI + HG_WIDTH

LANES = 128
PROMPT_BLOCK = 256
HG_CHUNK = 128
HG_SUB = 32
FF_BLOCK = 1024
SAMPLE_SEQS = 8
SAMPLE_KEYS = 256
VMEM_LIMIT = 58 * 1024 * 1024


def _dot(a, b):
    return jnp.dot(a, b, preferred_element_type=F32)


def _dot_nt(a, b):
    return lax.dot_general(a, b, (((1,), (1,)), ((), ())), preferred_element_type=F32)


def _rms(x, g_row):
    ms = jnp.mean(x * x, axis=-1, keepdims=True)
    return (x * lax.rsqrt(ms + EPS)) * g_row


def _rope(x, cos, sin_dn, sin_up):
    return x * cos + pltpu.roll(x, ROT_HALF, 1) * sin_dn + pltpu.roll(x, LANES - ROT_HALF, 1) * sin_up


def _lower_bound(lb_all, layer):
    m = jnp.max(lb_all, axis=0, keepdims=True)
    e = jnp.exp(lb_all - m)
    p = e / jnp.sum(e, axis=0, keepdims=True)
    cs = p[0:1]
    for l in range(1, layer + 1):
        cs = cs + p[l:l + 1]
    return jnp.maximum(cs - p[0:1], 0.0)


def _hgrn_gates(z, lb):
    e = jnp.exp(-jnp.abs(z))
    log_sig = jnp.minimum(z, 0.0) - jnp.log1p(e)
    a1 = jnp.log(lb)
    a2 = jnp.log1p(-lb) + log_sig
    log_f = jnp.maximum(a1, a2) + jnp.log1p(jnp.exp(-jnp.abs(a1 - a2)))
    k_in = (1.0 - lb) * (jnp.where(z >= 0.0, e, 1.0) / (1.0 + e))
    return log_f, k_in


def _silu(x):
    return x / (1.0 + jnp.exp(-x))


def _softmax_sink_pv(s, allowed, sink, v_bf16):
    s = jnp.where(allowed, s, -jnp.inf)
    m = jnp.maximum(jnp.max(s, axis=-1, keepdims=True), sink)
    p = jnp.exp(s - m)
    denom = jnp.sum(p, axis=-1, keepdims=True) + jnp.exp(sink - m)
    return _dot(p.astype(BF16), v_bf16) / denom


def _cumsum_rows(tri, g):
    g1 = g.astype(BF16)
    r1 = g - g1.astype(F32)
    g2 = r1.astype(BF16)
    g3 = (r1 - g2.astype(F32)).astype(BF16)
    return _dot(tri, g1) + _dot(tri, g2) + _dot(tri, g3)


def _hgrn_intra_operands(q, k, b):
    c = q.shape[0]
    q_parts, k_parts = [], []
    for i in range(c // HG_SUB):
        lo, hi = i * HG_SUB, (i + 1) * HG_SUB
        r = b[lo - 1:lo] if i > 0 else jnp.zeros((1, HG_DK), F32)
        q_parts.append(q[lo:hi] * jnp.exp(b[lo:hi] - r))
        k_parts.append(k[:hi] * jnp.exp(r - b[:hi]))
        if hi < c:
            k_parts.append(jnp.zeros((c - hi, HG_DK), F32))
    return (jnp.concatenate(q_parts, axis=0).astype(BF16),
            jnp.concatenate(k_parts, axis=0).astype(BF16))


def _hgrn_intra_scores(a_full):
    c = a_full.shape[0]
    a = jnp.concatenate(
        [a_full[i * HG_SUB:(i + 1) * HG_SUB, i * c:(i + 1) * c] for i in range(c // HG_SUB)], axis=0)
    row = lax.broadcasted_iota(jnp.int32, (c, c), 0)
    col = lax.broadcasted_iota(jnp.int32, (c, c), 1)
    return jnp.where(col <= row, a, 0.0).astype(BF16)


def _dense_out(x, mix_bf16, wo_ref, mnorm_ref, wup_ref, wdn_ref, fnorm_ref, last):
    x1 = x + _dot(mix_bf16, wo_ref[...])
    h2 = _rms(x1, mnorm_ref[...]).astype(BF16)
    acc = x1
    for c in range(D_FF // FF_BLOCK):
        u = _dot(h2, wup_ref[:, c * FF_BLOCK:(c + 1) * FF_BLOCK])
        u = jnp.square(jnp.maximum(u, 0.0)).astype(BF16)
        acc = acc + _dot(u, wdn_ref[c * FF_BLOCK:(c + 1) * FF_BLOCK, :])
    if last:
        acc = _rms(acc, fnorm_ref[...])
    return acc


def _prompt_layer_kernel(layer, last,
                         x_ref, cos_ref, sdn_ref, sup_ref, tri_ref, sinks_ref, lb_ref, anorm_ref,
                         win_ref, onorm_ref, wo_ref, mnorm_ref, wup_ref, wdn_ref, fnorm_ref,
                         y_ref, nk_ref, nv_ref, ns_ref,
                         proj_ref, mix_ref, kprev_ref, vprev_ref, st_ref):
    t_blk = pl.program_id(1)
    n_blk = pl.num_programs(1)
    tb = PROMPT_BLOCK

    @pl.when(t_blk == 0)
    def _():
        kprev_ref[...] = jnp.zeros_like(kprev_ref)
        vprev_ref[...] = jnp.zeros_like(vprev_ref)
        st_ref[...] = jnp.zeros_like(st_ref)

    x = x_ref[0]
    h = _rms(x, anorm_ref[...]).astype(BF16)
    proj_ref[...] = _dot(h, win_ref[...])

    cos, sdn, sup = cos_ref[...], sdn_ref[...], sup_ref[...]

    k_rot = _rope(proj_ref[:, OFF_K:OFF_K + KV_WIDTH], cos, sdn, sup)
    v_new = proj_ref[:, OFF_V:OFF_V + KV_WIDTH]
    k_all = jnp.concatenate([kprev_ref[...], k_rot], axis=0).astype(BF16)
    v_all = jnp.concatenate([vprev_ref[...], v_new], axis=0).astype(BF16)
    kprev_ref[...] = k_rot[tb - WINDOW:]
    vprev_ref[...] = v_new[tb - WINDOW:]
    nk_ref[0] = k_rot[tb - WINDOW:]
    nv_ref[0] = v_new[tb - WINDOW:]

    lo_half = lax.broadcasted_iota(jnp.int32, (tb, LANES), 1) < HEAD_DIM
    q_ext = []
    for hp in range(ATT_HEADS // 2):
        q_slab = _rope(proj_ref[:, OFF_Q + hp * LANES:OFF_Q + (hp + 1) * LANES], cos, sdn, sup)
        q_slab = q_slab * (HEAD_DIM ** -0.5)
        q_swap = pltpu.roll(q_slab, HEAD_DIM, 1)
        for sub in range(2):
            kvh = (2 * hp + sub) // GROUP
            src = q_slab if sub == kvh else q_swap
            keep = lo_half if kvh == 0 else ~lo_half
            q_ext.append(jnp.where(keep, src, 0.0).astype(BF16))

    row = lax.broadcasted_iota(jnp.int32, (WINDOW, 2 * WINDOW), 0)
    col = lax.broadcasted_iota(jnp.int32, (WINDOW, 2 * WINDOW), 1)
    in_window = (col >= row) & (col <= row + WINDOW)
    first_lo = jnp.where(t_blk == 0, WINDOW, 0)
    att = [[None] * (tb // WINDOW) for _ in range(ATT_HEADS)]
    for n in range(tb // WINDOW):
        rows_n = slice(n * WINDOW, (n + 1) * WINDOW)
        keys_n = slice(n * WINDOW, (n + 2) * WINDOW)
        q_stack = jnp.concatenate([q[rows_n] for q in q_ext], axis=0)
        s_all = _dot_nt(q_stack, k_all[keys_n])
        allowed = in_window & (col >= first_lo) if n == 0 else in_window
        probs, inv = [], []
        for head in range(ATT_HEADS):
            sink = sinks_ref[head]
            s = jnp.where(allowed, s_all[head * WINDOW:(head + 1) * WINDOW], -jnp.inf)
            m = jnp.maximum(jnp.max(s, axis=-1, keepdims=True), sink)
            p = jnp.exp(s - m)
            inv.append(1.0 / (jnp.sum(p, axis=-1, keepdims=True) + jnp.exp(sink - m)))
            probs.append(p.astype(BF16))
        o_all = _dot(jnp.concatenate(probs, axis=0), v_all[keys_n])
        for head in range(ATT_HEADS):
            att[head][n] = o_all[head * WINDOW:(head + 1) * WINDOW] * inv[head]
    for hp in range(ATT_HEADS // 2):
        a, b = (jnp.concatenate(att[2 * hp + sub], axis=0) for sub in range(2))
        kvh = (2 * hp) // GROUP
        slab = (jnp.where(lo_half, a, pltpu.roll(b, HEAD_DIM, 1)) if kvh == 0
                else jnp.where(lo_half, pltpu.roll(a, HEAD_DIM, 1), b))
        mix_ref[:, hp * LANES:(hp + 1) * LANES] = slab.astype(BF16)

    lb = _lower_bound(lb_ref[...], layer)
    log_f, k_in = _hgrn_gates(proj_ref[:, OFF_HF:OFF_HF + HG_WIDTH], lb)
    b_all = _cumsum_rows(tri_ref[...], log_f)
    onorm = onorm_ref[...]
    n_chunks = tb // HG_CHUNK
    units = [(hh, c) for c in range(n_chunks) for hh in range(HG_HEADS)]
    q_in, v_in = [], []
    for hh in range(HG_HEADS):
        q_in.append(_silu(proj_ref[:, OFF_HQ + hh * HG_DK:OFF_HQ + (hh + 1) * HG_DK]))
        v_in.append(proj_ref[:, OFF_HI + hh * HG_DV:OFF_HI + (hh + 1) * HG_DV])

    def piece(x, hh, c, lanes=False):
        x = x[:, hh * HG_DK:(hh + 1) * HG_DK] if lanes else x[hh]
        return x[c * HG_CHUNK:(c + 1) * HG_CHUNK]

    intra_ops = {u: _hgrn_intra_operands(piece(q_in, *u), piece(k_in, *u, lanes=True),
                                         piece(b_all, *u, lanes=True)) for u in units}
    a_full = {u: _dot_nt(*intra_ops[u]) for u in units}
    o_inter = {}
    for hh in range(HG_HEADS):
        st = st_ref[hh]
        for c in range(n_chunks):
            q, k, v, b = (piece(q_in, hh, c), piece(k_in, hh, c, True), piece(v_in, hh, c),
                          piece(b_all, hh, c, True))
            o_inter[(hh, c)] = _dot_nt((q * jnp.exp(b)).astype(BF16), st.astype(BF16))
            b_last = b[HG_CHUNK - 1:HG_CHUNK]
            k_hat = (k * jnp.exp(b_last - b)).astype(BF16)
            st = st * jnp.exp(b_last) + _dot(v.T.astype(BF16), k_hat)
        st_ref[hh] = st
    a_mask = {u: _hgrn_intra_scores(a_full[u]) for u in units}
    o_intra = {u: _dot(a_mask[u], piece(v_in, *u).astype(BF16)) for u in units}
    for hh in range(HG_HEADS):
        o = jnp.concatenate([o_intra[(hh, c)] + o_inter[(hh, c)] for c in range(n_chunks)], axis=0)
        gate = _silu(proj_ref[:, OFF_HG + hh * HG_DV:OFF_HG + (hh + 1) * HG_DV])
        mix_ref[:, ATT_WIDTH + hh * HG_DV:ATT_WIDTH + (hh + 1) * HG_DV] = (
            _rms(o, onorm) * gate).astype(BF16)

    @pl.when(t_blk == n_blk - 1)
    def _():
        for hh in range(HG_HEADS):
            ns_ref[0, hh] = st_ref[hh].T

    y_ref[0] = _dense_out(x, mix_ref[...], wo_ref, mnorm_ref, wup_ref, wdn_ref, fnorm_ref, last)


def _const_spec(shape):
    nd = len(shape)
    return pl.BlockSpec(shape, lambda *_: (0,) * nd, pipeline_mode=pl.Buffered(1))


def _prompt_layer(layer, x, rope, tri, sinks, lb_all, anorm, w_in, onorm, w_o, mnorm, w_up, w_dn, fnorm):
    batch, seq, _ = x.shape
    tb = PROMPT_BLOCK
    last = layer == DEPTH - 1
    rope_spec = pl.BlockSpec((tb, LANES), lambda b, t: (t, 0))
    carry_spec = pl.BlockSpec((1, WINDOW, KV_WIDTH), lambda b, t: (b, 0, 0))
    return pl.pallas_call(
        functools.partial(_prompt_layer_kernel, layer, last),
        grid=(batch, seq // tb),
        in_specs=[
            pl.BlockSpec((1, tb, D_MODEL), lambda b, t: (b, t, 0)),
            rope_spec, rope_spec, rope_spec,
            _const_spec((tb, tb)),
            pl.BlockSpec(memory_space=pltpu.SMEM),
            _const_spec((DEPTH, HG_WIDTH)),
            _const_spec((1, D_MODEL)),
            _const_spec((D_MODEL, IN_WIDTH)),
            _const_spec((1, HG_DV)),
            _const_spec((MIX_WIDTH, D_MODEL)),
            _const_spec((1, D_MODEL)),
            _const_spec((D_MODEL, D_FF)),
            _const_spec((D_FF, D_MODEL)),
            _const_spec((1, D_MODEL)),
        ],
        out_specs=[
            pl.BlockSpec((1, tb, D_MODEL), lambda b, t: (b, t, 0)),
            carry_spec, carry_spec,
            pl.BlockSpec((1, HG_HEADS, HG_DK, HG_DV), lambda b, t: (b, 0, 0, 0)),
        ],
        out_shape=[
            jax.ShapeDtypeStruct((batch, seq, D_MODEL), F32),
            jax.ShapeDtypeStruct((batch, WINDOW, KV_WIDTH), F32),
            jax.ShapeDtypeStruct((batch, WINDOW, KV_WIDTH), F32),
            jax.ShapeDtypeStruct((batch, HG_HEADS, HG_DK, HG_DV), F32),
        ],
        scratch_shapes=[
            pltpu.VMEM((tb, IN_WIDTH), F32),
            pltpu.VMEM((tb, MIX_WIDTH), BF16),
            pltpu.VMEM((WINDOW, KV_WIDTH), F32),
            pltpu.VMEM((WINDOW, KV_WIDTH), F32),
            pltpu.VMEM((HG_HEADS, HG_DV, HG_DK), F32),
        ],
        compiler_params=pltpu.CompilerParams(
            dimension_semantics=("arbitrary", "arbitrary"), vmem_limit_bytes=VMEM_LIMIT),
        name=f"prompt_layer{layer}",
    )(x, *rope, tri, sinks, lb_all, anorm, w_in, onorm, w_o, mnorm, w_up, w_dn, fnorm)


def _sample_in_kernel(x_ref, anorm_ref, win_ref, proj_ref):
    h = _rms(x_ref[...], anorm_ref[...]).astype(BF16)
    proj_ref[...] = _dot(h, win_ref[...])


def _sample_out_kernel(last, x_ref, mix_ref, wo_ref, mnorm_ref, wup_ref, wdn_ref, fnorm_ref, y_ref):
    y_ref[...] = _dense_out(x_ref[...], mix_ref[...].astype(BF16), wo_ref, mnorm_ref, wup_ref,
                            wdn_ref, fnorm_ref, last)


def _sample_mixer_kernel(layer,
                         p_ref, cos_ref, sdn_ref, sup_ref, ck_ref, cv_ref, s0_ref, sinks_ref, lb_ref,
                         onorm_ref,
                         mix_ref, nk_ref, nv_ref, ns_ref,
                         kf_ref, vf_ref):
    n_new = p_ref.shape[1]
    kf_ref[...] = jnp.zeros_like(kf_ref)
    vf_ref[...] = jnp.zeros_like(vf_ref)
    cos, sdn, sup = cos_ref[...], sdn_ref[...], sup_ref[...]
    lb = _lower_bound(lb_ref[...], layer)
    onorm = onorm_ref[...]
    row = lax.broadcasted_iota(jnp.int32, (n_new, SAMPLE_KEYS), 0)
    col = lax.broadcasted_iota(jnp.int32, (n_new, SAMPLE_KEYS), 1)
    allowed = (col >= row) & (col <= row + WINDOW)
    trow = lax.broadcasted_iota(jnp.int32, (n_new, 1), 0)

    def per_seq(s, carry):
        k_rot = _rope(p_ref[s, :, OFF_K:OFF_K + KV_WIDTH], cos, sdn, sup)
        v_new = p_ref[s, :, OFF_V:OFF_V + KV_WIDTH]
        kf_ref[0:WINDOW] = ck_ref[s]
        vf_ref[0:WINDOW] = cv_ref[s]
        kf_ref[WINDOW:WINDOW + n_new] = k_rot
        vf_ref[WINDOW:WINDOW + n_new] = v_new
        nk_ref[s] = kf_ref[n_new:n_new + WINDOW]
        nv_ref[s] = vf_ref[n_new:n_new + WINDOW]
        k_all = kf_ref[...].astype(BF16)
        v_all = vf_ref[...].astype(BF16)
        for hp in range(ATT_HEADS // 2):
            q_slab = _rope(p_ref[s, :, OFF_Q + hp * LANES:OFF_Q + (hp + 1) * LANES], cos, sdn, sup)
            q_slab = (q_slab * (HEAD_DIM ** -0.5)).astype(BF16)
            outs = []
            for sub in range(2):
                head = 2 * hp + sub
                kvh = head // GROUP
                kb = k_all[:, kvh * HEAD_DIM:(kvh + 1) * HEAD_DIM]
                vb = v_all[:, kvh * HEAD_DIM:(kvh + 1) * HEAD_DIM]
                sc = _dot_nt(q_slab[:, sub * HEAD_DIM:(sub + 1) * HEAD_DIM], kb)
                outs.append(_softmax_sink_pv(sc, allowed, sinks_ref[head], vb))
            mix_ref[s, :, hp * LANES:(hp + 1) * LANES] = jnp.concatenate(outs, axis=1)

        log_f, k_in = _hgrn_gates(p_ref[s, :, OFF_HF:OFF_HF + HG_WIDTH], lb)
        for hh in range(HG_HEADS):
            sl = slice(hh * HG_DK, (hh + 1) * HG_DK)
            q_in = _silu(p_ref[s, :, OFF_HQ + hh * HG_DK:OFF_HQ + (hh + 1) * HG_DK])
            v_in = p_ref[s, :, OFF_HI + hh * HG_DV:OFF_HI + (hh + 1) * HG_DV]
            g = log_f[:, sl]
            k = k_in[:, sl]
            rows = [g[0:1]]
            for t in range(1, n_new):
                rows.append(rows[-1] + g[t:t + 1])
            b = jnp.concatenate(rows, axis=0)
            st = s0_ref[s, hh]
            o = _dot((q_in * jnp.exp(b)).astype(BF16), st.astype(BF16))
            for j in range(n_new):
                decay = jnp.exp(jnp.minimum(b - rows[j], 0.0))
                a_col = jnp.sum(q_in * decay * k[j:j + 1], axis=-1, keepdims=True)
                o = o + jnp.where(trow >= j, a_col, 0.0) * v_in[j:j + 1]
            b_last = rows[-1]
            k_hat = k * jnp.exp(b_last - b)
            upd = lax.dot_general(k_hat.astype(BF16), v_in.astype(BF16), (((0,), (0,)), ((), ())),
                                  preferred_element_type=F32)
            ns_ref[s, hh] = st * jnp.exp(b_last).T + upd
            gate = _silu(p_ref[s, :, OFF_HG + hh * HG_DV:OFF_HG + (hh + 1) * HG_DV])
            mix_ref[s, :, ATT_WIDTH + hh * HG_DV:ATT_WIDTH + (hh + 1) * HG_DV] = _rms(o, onorm) * gate
        return carry

    lax.fori_loop(0, p_ref.shape[0], per_seq, 0)


def _sample_layer(layer, x, rope, cache_k, cache_v, state, sinks, lb_all, anorm, w_in, onorm, w_o, mnorm,
                  w_up, w_dn, fnorm):
    n_seq, n_new, _ = x.shape
    n_tok = n_seq * n_new
    sb = SAMPLE_SEQS
    last = layer == DEPTH - 1
    x2 = x.reshape(n_tok, D_MODEL)
    params = pltpu.CompilerParams(vmem_limit_bytes=VMEM_LIMIT)

    proj = pl.pallas_call(
        _sample_in_kernel,
        out_shape=jax.ShapeDtypeStruct((n_tok, IN_WIDTH), F32),
        compiler_params=params,
        name=f"sample_in{layer}",
    )(x2, anorm, w_in)

    seq_spec = lambda *tail: pl.BlockSpec((sb,) + tail, lambda i: (i,) + (0,) * len(tail))
    layer_spec = lambda *tail: pl.BlockSpec((None, sb) + tail, lambda i: (layer, i) + (0,) * len(tail))
    rope_spec = _const_spec((n_new, LANES))
    mix, nk, nv, ns = pl.pallas_call(
        functools.partial(_sample_mixer_kernel, layer),
        grid=(n_seq // sb,),
        in_specs=[
            seq_spec(n_new, IN_WIDTH),
            rope_spec, rope_spec, rope_spec,
            layer_spec(WINDOW, KV_WIDTH), layer_spec(WINDOW, KV_WIDTH),
            layer_spec(HG_HEADS, HG_DK, HG_DV),
            pl.BlockSpec(memory_space=pltpu.SMEM),
            _const_spec((DEPTH, HG_WIDTH)),
            _const_spec((1, HG_DV)),
        ],
        out_specs=[
            seq_spec(n_new, MIX_WIDTH),
            seq_spec(WINDOW, KV_WIDTH), seq_spec(WINDOW, KV_WIDTH),
            seq_spec(HG_HEADS, HG_DK, HG_DV),
        ],
        out_shape=[
            jax.ShapeDtypeStruct((n_seq, n_new, MIX_WIDTH), F32),
            jax.ShapeDtypeStruct((n_seq, WINDOW, KV_WIDTH), F32),
            jax.ShapeDtypeStruct((n_seq, WINDOW, KV_WIDTH), F32),
            jax.ShapeDtypeStruct((n_seq, HG_HEADS, HG_DK, HG_DV), F32),
        ],
        scratch_shapes=[
            pltpu.VMEM((SAMPLE_KEYS, KV_WIDTH), F32),
            pltpu.VMEM((SAMPLE_KEYS, KV_WIDTH), F32),
        ],
        compiler_params=pltpu.CompilerParams(
            dimension_semantics=("arbitrary",), vmem_limit_bytes=VMEM_LIMIT),
        name=f"sample_mixer{layer}",
    )(proj.reshape(n_seq, n_new, IN_WIDTH), *rope, cache_k, cache_v, state, sinks, lb_all, onorm)

    y = pl.pallas_call(
        functools.partial(_sample_out_kernel, last),
        out_shape=jax.ShapeDtypeStruct((n_tok, D_MODEL), F32),
        compiler_params=params,
        name=f"sample_out{layer}",
    )(x2, mix.reshape(n_tok, MIX_WIDTH), w_o, mnorm, w_up, w_dn, fnorm)
    return y.reshape(n_seq, n_new, D_MODEL), nk, nv, ns


def _rope_tables(pos):
    n = pos.shape[0]
    inv_freq = jnp.power(ROPE_THETA, -jnp.arange(ROT_HALF, dtype=F32) * (2.0 / ROT_DIM))
    ang = pos.astype(F32)[:, None] * inv_freq[None, :]
    cos, sin = jnp.cos(ang), jnp.sin(ang)
    rest = HEAD_DIM - ROT_DIM
    zeros_h = jnp.zeros((n, ROT_HALF), F32)
    cos_t = jnp.concatenate([cos, cos, jnp.ones((n, rest), F32)], axis=1)
    sdn_t = jnp.concatenate([zeros_h, sin, jnp.zeros((n, rest), F32)], axis=1)
    sup_t = jnp.concatenate([-sin, zeros_h, jnp.zeros((n, rest), F32)], axis=1)
    reps = LANES // HEAD_DIM
    return tuple(jnp.tile(t, (1, reps)) for t in (cos_t, sdn_t, sup_t))


def _chunk_tri(n, chunk):
    r = jnp.arange(n)
    same = (r[:, None] // chunk) == (r[None, :] // chunk)
    return (same & (r[None, :] <= r[:, None])).astype(BF16)


def kernel(x_prompt, x_sample, cache_k, cache_v, state_hgrn, attn_norm, w_in, att_sinks, hgrn_lower_bounds,
           hgrn_out_norm, w_o, mlp_norm, w_up, w_down, final_norm):
    batch, seq, _ = x_prompt.shape
    n_seq, n_new, _ = x_sample.shape
    assert seq % PROMPT_BLOCK == 0 and PROMPT_BLOCK % HG_CHUNK == 0 and n_seq % SAMPLE_SEQS == 0
    assert n_new + WINDOW <= SAMPLE_KEYS

    rope_p = _rope_tables(jnp.arange(seq, dtype=jnp.int32))
    rope_s = _rope_tables(PAST_LEN + jnp.arange(n_new, dtype=jnp.int32))
    tri = _chunk_tri(PROMPT_BLOCK, HG_CHUNK)
    lb_all = hgrn_lower_bounds.astype(F32)
    fnorm = final_norm.reshape(1, D_MODEL)

    ck = cache_k.reshape(DEPTH, n_seq, WINDOW, KV_WIDTH)
    cv = cache_v.reshape(DEPTH, n_seq, WINDOW, KV_WIDTH)
    xp, xs = x_prompt, x_sample
    nk_p, nv_p, ns_p, nk_s, nv_s, ns_s = [], [], [], [], [], []
    for l in range(DEPTH):
        weights = (attn_norm[l].reshape(1, D_MODEL), w_in[l].astype(BF16), hgrn_out_norm[l].reshape(1, HG_DV),
                   w_o[l].astype(BF16), mlp_norm[l].reshape(1, D_MODEL), w_up[l].astype(BF16),
                   w_down[l].astype(BF16), fnorm)
        xp, k1, v1, s1 = _prompt_layer(l, xp, rope_p, tri, att_sinks[l], lb_all, *weights)
        xs, k2, v2, s2 = _sample_layer(
            l, xs, rope_s, ck, cv, state_hgrn, att_sinks[l], lb_all, *weights)
        nk_p.append(k1), nv_p.append(v1), ns_p.append(s1)
        nk_s.append(k2), nv_s.append(v2), ns_s.append(s2)

    kv_p = (DEPTH, batch, WINDOW, ATT_KV_HEADS, HEAD_DIM)
    kv_s = (DEPTH, n_seq, WINDOW, ATT_KV_HEADS, HEAD_DIM)
    return (xp, xs,
            jnp.stack(nk_p).reshape(kv_p), jnp.stack(nv_p).reshape(kv_p), jnp.stack(ns_p),
            jnp.stack(nk_s).reshape(kv_s), jnp.stack(nv_s).reshape(kv_s), jnp.stack(ns_s))
```

```python
import functools

import jax
import jax.numpy as jnp
from jax import lax
from jax.experimental import pallas as pl
from jax.experimental.pallas import tpu as pltpu

F32 = jnp.float32
BF16 = jnp.bfloat16

D_MODEL = 1024
DEPTH = 2
PAST_LEN = 16384
ATT_HEADS = 8
ATT_KV_HEADS = 2
HEAD_DIM = 64
GROUP = ATT_HEADS // ATT_KV_HEADS
ATT_WIDTH = ATT_HEADS * HEAD_DIM
KV_WIDTH = ATT_KV_HEADS * HEAD_DIM
WINDOW = 128
ROT_DIM = HEAD_DIM // 4
ROT_HALF = ROT_DIM // 2
ROPE_THETA = 500000.0
HG_HEADS = 4
HG_DK = 128
HG_DV = 128
HG_WIDTH = HG_HEADS * HG_DK
MIX_WIDTH = ATT_WIDTH + HG_WIDTH
IN_WIDTH = ATT_WIDTH + 2 * KV_WIDTH + 4 * HG_WIDTH
D_FF = 4 * D_MODEL
EPS = 1e-6

OFF_Q = 0
OFF_K = ATT_WIDTH
OFF_V = OFF_K + KV_WIDTH
OFF_HQ = OFF_V + KV_WIDTH
OFF_HF = OFF_HQ + HG_WIDTH
OFF_HI = OFF_HF + HG_WIDTH
OFF_HG = OFF_HI + HG_WIDTH

LANES = 128
PROMPT_BLOCK = 256
HG_CHUNK = 128
HG_SUB = 32
FF_BLOCK = 1024
SUBLANES = 8
VMEM_LIMIT = 58 * 1024 * 1024


def _dot(a, b):
    return jnp.dot(a, b, preferred_element_type=F32)


def _dot_nt(a, b):
    return lax.dot_general(a, b, (((1,), (1,)), ((), ())), preferred_element_type=F32)


def _rms(x, g_row):
    ms = jnp.mean(x * x, axis=-1, keepdims=True)
    return (x * lax.rsqrt(ms + EPS)) * g_row


def _rope(x, cos, sin_dn, sin_up):
    return x * cos + pltpu.roll(x, ROT_HALF, 1) * sin_dn + pltpu.roll(x, LANES - ROT_HALF, 1) * sin_up


def _lower_bound(lb_all, layer):
    m = jnp.max(lb_all, axis=0, keepdims=True)
    e = jnp.exp(lb_all - m)
    p = e / jnp.sum(e, axis=0, keepdims=True)
    cs = p[0:1]
    for l in range(1, layer + 1):
        cs = cs + p[l:l + 1]
    return jnp.maximum(cs - p[0:1], 0.0)


def _hgrn_gates(z, lb):
    e = jnp.exp(-jnp.abs(z))
    log_sig = jnp.minimum(z, 0.0) - jnp.log1p(e)
    a1 = jnp.log(lb)
    a2 = jnp.log1p(-lb) + log_sig
    log_f = jnp.maximum(a1, a2) + jnp.log1p(jnp.exp(-jnp.abs(a1 - a2)))
    k_in = (1.0 - lb) * (jnp.where(z >= 0.0, e, 1.0) / (1.0 + e))
    return log_f, k_in


def _silu(x):
    return x / (1.0 + jnp.exp(-x))


def _stacked_queries(proj_ref, cos, sdn, sup):
    t = proj_ref.shape[0]
    lo_half = lax.broadcasted_iota(jnp.int32, (t, LANES), 1) < HEAD_DIM
    q_ext = []
    for hp in range(ATT_HEADS // 2):
        q_slab = _rope(proj_ref[:, OFF_Q + hp * LANES:OFF_Q + (hp + 1) * LANES], cos, sdn, sup)
        q_slab = q_slab * (HEAD_DIM ** -0.5)
        q_swap = pltpu.roll(q_slab, HEAD_DIM, 1)
        for sub in range(2):
            kvh = (2 * hp + sub) // GROUP
            src = q_slab if sub == kvh else q_swap
            keep = lo_half if kvh == 0 else ~lo_half
            q_ext.append(jnp.where(keep, src, 0.0).astype(BF16))
    return q_ext


def _softmax_sink(s, allowed, sink):
    s = jnp.where(allowed, s, -jnp.inf)
    m = jnp.maximum(jnp.max(s, axis=-1, keepdims=True), sink)
    p = jnp.exp(s - m)
    inv = 1.0 / (jnp.sum(p, axis=-1, keepdims=True) + jnp.exp(sink - m))
    return p.astype(BF16), inv


def _merge_head_pair(a, b, kvh):
    lo_half = lax.broadcasted_iota(jnp.int32, a.shape, 1) < HEAD_DIM
    if kvh == 0:
        return jnp.where(lo_half, a, pltpu.roll(b, HEAD_DIM, 1))
    return jnp.where(lo_half, pltpu.roll(a, HEAD_DIM, 1), b)


def _cumsum_rows(tri, g):
    g1 = g.astype(BF16)
    r1 = g - g1.astype(F32)
    g2 = r1.astype(BF16)
    g3 = (r1 - g2.astype(F32)).astype(BF16)
    return _dot(tri, g1) + _dot(tri, g2) + _dot(tri, g3)


def _hgrn_intra_operands(q, k, b):
    c = q.shape[0]
    q_parts, k_parts = [], []
    for i in range(c // HG_SUB):
        lo, hi = i * HG_SUB, (i + 1) * HG_SUB
        r = b[lo - 1:lo] if i > 0 else jnp.zeros((1, HG_DK), F32)
        q_parts.append(q[lo:hi] * jnp.exp(b[lo:hi] - r))
        k_parts.append(k[:hi] * jnp.exp(r - b[:hi]))
        if hi < c:
            k_parts.append(jnp.zeros((c - hi, HG_DK), F32))
    return (jnp.concatenate(q_parts, axis=0).astype(BF16),
            jnp.concatenate(k_parts, axis=0).astype(BF16))


def _hgrn_intra_scores(a_full):
    c = a_full.shape[0]
    a = jnp.concatenate(
        [a_full[i * HG_SUB:(i + 1) * HG_SUB, i * c:(i + 1) * c] for i in range(c // HG_SUB)], axis=0)
    row = lax.broadcasted_iota(jnp.int32, (c, c), 0)
    col = lax.broadcasted_iota(jnp.int32, (c, c), 1)
    return jnp.where(col <= row, a, 0.0).astype(BF16)


def _dense_out(x, mix_bf16, wo_ref, mnorm_ref, wup_ref, wdn_ref, fnorm_ref, last):
    x1 = x + _dot(mix_bf16, wo_ref[...])
    h2 = _rms(x1, mnorm_ref[...]).astype(BF16)
    acc = x1
    for c in range(D_FF // FF_BLOCK):
        u = _dot(h2, wup_ref[:, c * FF_BLOCK:(c + 1) * FF_BLOCK])
        u = jnp.square(jnp.maximum(u, 0.0)).astype(BF16)
        acc = acc + _dot(u, wdn_ref[c * FF_BLOCK:(c + 1) * FF_BLOCK, :])
    if last:
        acc = _rms(acc, fnorm_ref[...])
    return acc


def _prompt_layer_kernel(layer, last,
                         x_ref, cos_ref, sdn_ref, sup_ref, tri_ref, sinks_ref, lb_ref, anorm_ref,
                         win_ref, onorm_ref, wo_ref, mnorm_ref, wup_ref, wdn_ref, fnorm_ref,
                         y_ref, nk_ref, nv_ref, ns_ref,
                         proj_ref, mix_ref, kprev_ref, vprev_ref, st_ref):
    t_blk = pl.program_id(1)
    n_blk = pl.num_programs(1)
    tb = PROMPT_BLOCK

    @pl.when(t_blk == 0)
    def _():
        kprev_ref[...] = jnp.zeros_like(kprev_ref)
        vprev_ref[...] = jnp.zeros_like(vprev_ref)
        st_ref[...] = jnp.zeros_like(st_ref)

    x = x_ref[0]
    h = _rms(x, anorm_ref[...]).astype(BF16)
    proj_ref[...] = _dot(h, win_ref[...])

    cos, sdn, sup = cos_ref[...], sdn_ref[...], sup_ref[...]

    k_rot = _rope(proj_ref[:, OFF_K:OFF_K + KV_WIDTH], cos, sdn, sup)
    v_new = proj_ref[:, OFF_V:OFF_V + KV_WIDTH]
    k_all = jnp.concatenate([kprev_ref[...], k_rot], axis=0).astype(BF16)
    v_all = jnp.concatenate([vprev_ref[...], v_new], axis=0).astype(BF16)
    kprev_ref[...] = k_rot[tb - WINDOW:]
    vprev_ref[...] = v_new[tb - WINDOW:]
    nk_ref[0] = k_rot[tb - WINDOW:]
    nv_ref[0] = v_new[tb - WINDOW:]

    q_ext = _stacked_queries(proj_ref, cos, sdn, sup)
    row = lax.broadcasted_iota(jnp.int32, (WINDOW, 2 * WINDOW), 0)
    col = lax.broadcasted_iota(jnp.int32, (WINDOW, 2 * WINDOW), 1)
    in_window = (col >= row) & (col <= row + WINDOW)
    first_lo = jnp.where(t_blk == 0, WINDOW, 0)
    att = [[None] * (tb // WINDOW) for _ in range(ATT_HEADS)]
    for n in range(tb // WINDOW):
        rows_n = slice(n * WINDOW, (n + 1) * WINDOW)
        keys_n = slice(n * WINDOW, (n + 2) * WINDOW)
        q_stack = jnp.concatenate([q[rows_n] for q in q_ext], axis=0)
        s_all = _dot_nt(q_stack, k_all[keys_n])
        allowed = in_window & (col >= first_lo) if n == 0 else in_window
        soft = [_softmax_sink(s_all[head * WINDOW:(head + 1) * WINDOW], allowed, sinks_ref[head])
                for head in range(ATT_HEADS)]
        o_all = _dot(jnp.concatenate([p for p, _ in soft], axis=0), v_all[keys_n])
        for head in range(ATT_HEADS):
            att[head][n] = o_all[head * WINDOW:(head + 1) * WINDOW] * soft[head][1]
    for hp in range(ATT_HEADS // 2):
        a, b = (jnp.concatenate(att[2 * hp + sub], axis=0) for sub in range(2))
        mix_ref[:, hp * LANES:(hp + 1) * LANES] = _merge_head_pair(a, b, (2 * hp) // GROUP).astype(BF16)

    lb = _lower_bound(lb_ref[...], layer)
    log_f, k_in = _hgrn_gates(proj_ref[:, OFF_HF:OFF_HF + HG_WIDTH], lb)
    b_all = _cumsum_rows(tri_ref[...], log_f)
    onorm = onorm_ref[...]
    n_chunks = tb // HG_CHUNK
    units = [(hh, c) for c in range(n_chunks) for hh in range(HG_HEADS)]
    q_in, v_in = [], []
    for hh in range(HG_HEADS):
        q_in.append(_silu(proj_ref[:, OFF_HQ + hh * HG_DK:OFF_HQ + (hh + 1) * HG_DK]))
        v_in.append(proj_ref[:, OFF_HI + hh * HG_DV:OFF_HI + (hh + 1) * HG_DV])

    def piece(x, hh, c, lanes=False):
        x = x[:, hh * HG_DK:(hh + 1) * HG_DK] if lanes else x[hh]
        return x[c * HG_CHUNK:(c + 1) * HG_CHUNK]

    intra_ops = {u: _hgrn_intra_operands(piece(q_in, *u), piece(k_in, *u, lanes=True),
                                         piece(b_all, *u, lanes=True)) for u in units}
    a_full = {u: _dot_nt(*intra_ops[u]) for u in units}
    o_inter = {}
    for hh in range(HG_HEADS):
        st = st_ref[hh]
        for c in range(n_chunks):
            q, k, v, b = (piece(q_in, hh, c), piece(k_in, hh, c, True), piece(v_in, hh, c),
                          piece(b_all, hh, c, True))
            o_inter[(hh, c)] = _dot_nt((q * jnp.exp(b)).astype(BF16), st.astype(BF16))
            b_last = b[HG_CHUNK - 1:HG_CHUNK]
            k_hat = (k * jnp.exp(b_last - b)).astype(BF16)
            st = st * jnp.exp(b_last) + _dot(v.T.astype(BF16), k_hat)
        st_ref[hh] = st
    a_mask = {u: _hgrn_intra_scores(a_full[u]) for u in units}
    o_intra = {u: _dot(a_mask[u], piece(v_in, *u).astype(BF16)) for u in units}
    for hh in range(HG_HEADS):
        o = jnp.concatenate([o_intra[(hh, c)] + o_inter[(hh, c)] for c in range(n_chunks)], axis=0)
        gate = _silu(proj_ref[:, OFF_HG + hh * HG_DV:OFF_HG + (hh + 1) * HG_DV])
        mix_ref[:, ATT_WIDTH + hh * HG_DV:ATT_WIDTH + (hh + 1) * HG_DV] = (
            _rms(o, onorm) * gate).astype(BF16)

    @pl.when(t_blk == n_blk - 1)
    def _():
        for hh in range(HG_HEADS):
            ns_ref[0, hh] = st_ref[hh].T

    y_ref[0] = _dense_out(x, mix_ref[...], wo_ref, mnorm_ref, wup_ref, wdn_ref, fnorm_ref, last)


def _const_spec(shape):
    nd = len(shape)
    return pl.BlockSpec(shape, lambda *_: (0,) * nd, pipeline_mode=pl.Buffered(1))


def _prompt_layer(layer, x, rope, tri, sinks, lb_all, anorm, w_in, onorm, w_o, mnorm, w_up, w_dn, fnorm):
    batch, seq, _ = x.shape
    tb = PROMPT_BLOCK
    last = layer == DEPTH - 1
    rope_spec = pl.BlockSpec((tb, LANES), lambda b, t: (t, 0))
    carry_spec = pl.BlockSpec((1, WINDOW, KV_WIDTH), lambda b, t: (b, 0, 0))
    return pl.pallas_call(
        functools.partial(_prompt_layer_kernel, layer, last),
        grid=(batch, seq // tb),
        in_specs=[
            pl.BlockSpec((1, tb, D_MODEL), lambda b, t: (b, t, 0)),
            rope_spec, rope_spec, rope_spec,
            _const_spec((tb, tb)),
            pl.BlockSpec(memory_space=pltpu.SMEM),
            _const_spec((DEPTH, HG_WIDTH)),
            _const_spec((1, D_MODEL)),
            _const_spec((D_MODEL, IN_WIDTH)),
            _const_spec((1, HG_DV)),
            _const_spec((MIX_WIDTH, D_MODEL)),
            _const_spec((1, D_MODEL)),
            _const_spec((D_MODEL, D_FF)),
            _const_spec((D_FF, D_MODEL)),
            _const_spec((1, D_MODEL)),
        ],
        out_specs=[
            pl.BlockSpec((1, tb, D_MODEL), lambda b, t: (b, t, 0)),
            carry_spec, carry_spec,
            pl.BlockSpec((1, HG_HEADS, HG_DK, HG_DV), lambda b, t: (b, 0, 0, 0)),
        ],
        out_shape=[
            jax.ShapeDtypeStruct((batch, seq, D_MODEL), F32),
            jax.ShapeDtypeStruct((batch, WINDOW, KV_WIDTH), F32),
            jax.ShapeDtypeStruct((batch, WINDOW, KV_WIDTH), F32),
            jax.ShapeDtypeStruct((batch, HG_HEADS, HG_DK, HG_DV), F32),
        ],
        scratch_shapes=[
            pltpu.VMEM((tb, IN_WIDTH), F32),
            pltpu.VMEM((tb, MIX_WIDTH), BF16),
            pltpu.VMEM((WINDOW, KV_WIDTH), F32),
            pltpu.VMEM((WINDOW, KV_WIDTH), F32),
            pltpu.VMEM((HG_HEADS, HG_DV, HG_DK), F32),
        ],
        compiler_params=pltpu.CompilerParams(
            dimension_semantics=("arbitrary", "arbitrary"), vmem_limit_bytes=VMEM_LIMIT),
        name=f"prompt_layer{layer}",
    )(x, *rope, tri, sinks, lb_all, anorm, w_in, onorm, w_o, mnorm, w_up, w_dn, fnorm)


def _sample_in_kernel(x_ref, anorm_ref, win_ref, proj_ref):
    h = _rms(x_ref[...], anorm_ref[...]).astype(BF16)
    proj_ref[...] = _dot(h, win_ref[...])


def _sample_out_kernel(last, x_ref, mix_ref, wo_ref, mnorm_ref, wup_ref, wdn_ref, fnorm_ref, y_ref):
    y_ref[...] = _dense_out(x_ref[...], mix_ref[...].astype(BF16), wo_ref, mnorm_ref, wup_ref,
                            wdn_ref, fnorm_ref, last)


def _sample_mixer_kernel(layer, n_new,
                         p_ref, cos_ref, sdn_ref, sup_ref, tri_ref, rev_ref, sel_ref, ck_ref, cv_ref,
                         s0_ref, sinks_ref, lb_ref, onorm_ref,
                         mix_ref, nk_ref, nv_ref, ns_ref,
                         pad_ref):
    sb = ck_ref.shape[0]
    ts = p_ref.shape[0]
    tok_shift = n_new.bit_length() - 1
    key_shift = WINDOW.bit_length() - 1
    n_cache = sb * WINDOW
    cos, sdn, sup = cos_ref[...], sdn_ref[...], sup_ref[...]

    k_new = _rope(p_ref[:, OFF_K:OFF_K + KV_WIDTH], cos, sdn, sup)
    v_new = p_ref[:, OFF_V:OFF_V + KV_WIDTH]
    pad_keys = jnp.zeros((LANES - ts, KV_WIDTH), F32)
    k_all = jnp.concatenate([ck_ref[...].reshape(n_cache, KV_WIDTH), k_new, pad_keys], axis=0).astype(BF16)
    v_all = jnp.concatenate([cv_ref[...].reshape(n_cache, KV_WIDTH), v_new, pad_keys], axis=0).astype(BF16)
    n_keys = n_cache + LANES
    row = lax.broadcasted_iota(jnp.int32, (ts, n_keys), 0)
    col = lax.broadcasted_iota(jnp.int32, (ts, n_keys), 1)
    new_idx = col - n_cache
    cached_ok = ((col >> key_shift) == (row >> tok_shift)) & ((col & (WINDOW - 1)) >= (row & (n_new - 1)))
    new_ok = ((new_idx < ts) & ((new_idx >> tok_shift) == (row >> tok_shift))
              & ((new_idx & (n_new - 1)) <= (row & (n_new - 1))))
    allowed = ((col < n_cache) & cached_ok) | ((col >= n_cache) & new_ok)

    q_ext = _stacked_queries(p_ref, cos, sdn, sup)
    s_all = _dot_nt(jnp.concatenate(q_ext, axis=0), k_all)
    soft = [_softmax_sink(s_all[head * ts:(head + 1) * ts], allowed, sinks_ref[head])
            for head in range(ATT_HEADS)]
    o_all = _dot(jnp.concatenate([p for p, _ in soft], axis=0), v_all)
    att = [o_all[head * ts:(head + 1) * ts] * soft[head][1] for head in range(ATT_HEADS)]
    for hp in range(ATT_HEADS // 2):
        mix_ref[:, hp * LANES:(hp + 1) * LANES] = _merge_head_pair(
            att[2 * hp], att[2 * hp + 1], (2 * hp) // GROUP)
    for s in range(sb):
        nk_ref[s, 0:WINDOW - n_new] = ck_ref[s, n_new:WINDOW]
        nv_ref[s, 0:WINDOW - n_new] = cv_ref[s, n_new:WINDOW]
        nk_ref[s, WINDOW - n_new:WINDOW] = k_new[s * n_new:(s + 1) * n_new]
        nv_ref[s, WINDOW - n_new:WINDOW] = v_new[s * n_new:(s + 1) * n_new]

    lb = _lower_bound(lb_ref[...], layer)
    log_f, k_in = _hgrn_gates(p_ref[:, OFF_HF:OFF_HF + HG_WIDTH], lb)
    b = _cumsum_rows(tri_ref[...], log_f)
    later = _cumsum_rows(rev_ref[...], log_f)
    q_in = _silu(p_ref[:, OFF_HQ:OFF_HQ + HG_WIDTH])
    v_in = p_ref[:, OFF_HI:OFF_HI + HG_WIDTH]
    gate = _silu(p_ref[:, OFF_HG:OFF_HG + HG_WIDTH])
    onorm = onorm_ref[...]
    heads = [slice(hh * HG_DK, (hh + 1) * HG_DK) for hh in range(HG_HEADS)]

    pad_ref[...] = jnp.zeros_like(pad_ref)
    for i, x in enumerate((k_in, b, v_in)):
        pad_ref[i, SUBLANES:SUBLANES + ts] = x
    tok = lax.broadcasted_iota(jnp.int32, (ts, 1), 0) & (n_new - 1)
    o_intra = [jnp.zeros((ts, HG_DV), F32) for _ in heads]
    for d in range(n_new):
        k_d, b_d, v_d = (pad_ref[i, SUBLANES - d:SUBLANES - d + ts] for i in range(3))
        w = q_in * k_d * jnp.exp(jnp.minimum(b - b_d, 0.0))
        for hh, sl in enumerate(heads):
            a_col = jnp.sum(w[:, sl], axis=-1, keepdims=True)
            o_intra[hh] = o_intra[hh] + jnp.where(tok >= d, a_col, 0.0) * v_d[:, sl]

    q_dec = (q_in * jnp.exp(b)).astype(BF16)
    seq_of_row = lax.broadcasted_iota(jnp.int32, (ts, 1), 0) >> tok_shift
    for hh, sl in enumerate(heads):
        o = o_intra[hh]
        for s in range(sb):
            o_s = _dot(q_dec[:, sl], s0_ref[s, hh].astype(BF16))
            o = o + jnp.where(seq_of_row == s, o_s, 0.0)
        mix_ref[:, ATT_WIDTH + hh * HG_DV:ATT_WIDTH + (hh + 1) * HG_DV] = _rms(o, onorm) * gate[:, sl]

    k_hat = k_in * jnp.exp(later)
    e_b = jnp.exp(b)
    e1 = e_b.astype(BF16).astype(F32)
    e2 = (e_b - e1).astype(BF16).astype(F32)
    e3 = e_b - e1 - e2
    own = ((lax.broadcasted_iota(jnp.int32, (ts, sb * HG_DV), 1) >> key_shift)
           == (lax.broadcasted_iota(jnp.int32, (ts, sb * HG_DV), 0) >> tok_shift))
    no_v = jnp.zeros((3 * ts, sb * HG_DV), F32)
    for hh, sl in enumerate(heads):
        lhs = jnp.concatenate([k_hat[:, sl], e1[:, sl], e2[:, sl], e3[:, sl]], axis=0).T.astype(BF16)
        v_rep = jnp.concatenate([v_in[:, sl]] * sb, axis=1)
        v_diag = jnp.concatenate([jnp.where(own, v_rep, 0.0), no_v], axis=0).astype(BF16)
        upd = _dot(lhs, v_diag)
        decay = _dot(lhs, sel_ref[...])
        for s in range(sb):
            cols = slice(s * HG_DV, (s + 1) * HG_DV)
            ns_ref[s, hh] = decay[:, cols] * s0_ref[s, hh] + upd[:, cols]


def _sample_layer(layer, x, rope, cache_k, cache_v, state, sinks, lb_all, anorm, w_in, onorm, w_o, mnorm,
                  w_up, w_dn, fnorm):
    n_seq, n_new, _ = x.shape
    n_tok = n_seq * n_new
    ts = LANES // 4
    sb = ts // n_new
    assert sb * n_new == ts and n_seq % sb == 0 and n_new <= SUBLANES
    assert n_new & (n_new - 1) == 0 and WINDOW & (WINDOW - 1) == 0 and HG_DV == WINDOW
    last = layer == DEPTH - 1
    x2 = x.reshape(n_tok, D_MODEL)
    params = pltpu.CompilerParams(vmem_limit_bytes=VMEM_LIMIT)

    proj = pl.pallas_call(
        _sample_in_kernel,
        out_shape=jax.ShapeDtypeStruct((n_tok, IN_WIDTH), F32),
        compiler_params=params,
        name=f"sample_in{layer}",
    )(x2, anorm, w_in)

    r = jnp.arange(ts)
    same_seq = (r[:, None] // n_new) == (r[None, :] // n_new)
    tri = (same_seq & (r[None, :] <= r[:, None])).astype(BF16)
    rev = (same_seq & (r[None, :] > r[:, None])).astype(BF16)
    c = jnp.arange(sb * HG_DV)
    is_last = ((r[:, None] % n_new) == n_new - 1) & ((r[:, None] // n_new) == (c[None, :] // HG_DV))
    sel = jnp.concatenate([jnp.zeros_like(is_last)] + [is_last] * 3, axis=0).astype(BF16)

    tok_spec = lambda width: pl.BlockSpec((ts, width), lambda i: (i, 0))
    seq_spec = lambda *tail: pl.BlockSpec((sb,) + tail, lambda i: (i,) + (0,) * len(tail))
    layer_spec = lambda *tail: pl.BlockSpec((None, sb) + tail, lambda i: (layer, i) + (0,) * len(tail))
    rope_spec = _const_spec((ts, LANES))
    mix, nk, nv, ns = pl.pallas_call(
        functools.partial(_sample_mixer_kernel, layer, n_new),
        grid=(n_seq // sb,),
        in_specs=[
            tok_spec(IN_WIDTH),
            rope_spec, rope_spec, rope_spec,
            _const_spec((ts, ts)), _const_spec((ts, ts)), _const_spec((4 * ts, sb * HG_DV)),
            layer_spec(WINDOW, KV_WIDTH), layer_spec(WINDOW, KV_WIDTH),
            layer_spec(HG_HEADS, HG_DK, HG_DV),
            pl.BlockSpec(memory_space=pltpu.SMEM),
            _const_spec((DEPTH, HG_WIDTH)),
            _const_spec((1, HG_DV)),
        ],
        out_specs=[
            tok_spec(MIX_WIDTH),
            seq_spec(WINDOW, KV_WIDTH), seq_spec(WINDOW, KV_WIDTH),
            seq_spec(HG_HEADS, HG_DK, HG_DV),
        ],
        out_shape=[
            jax.ShapeDtypeStruct((n_tok, MIX_WIDTH), F32),
            jax.ShapeDtypeStruct((n_seq, WINDOW, KV_WIDTH), F32),
            jax.ShapeDtypeStruct((n_seq, WINDOW, KV_WIDTH), F32),
            jax.ShapeDtypeStruct((n_seq, HG_HEADS, HG_DK, HG_DV), F32),
        ],
        scratch_shapes=[pltpu.VMEM((3, SUBLANES + ts, HG_WIDTH), F32)],
        compiler_params=pltpu.CompilerParams(
            dimension_semantics=("arbitrary",), vmem_limit_bytes=VMEM_LIMIT),
        name=f"sample_mixer{layer}",
    )(proj, *(jnp.tile(t, (sb, 1)) for t in rope), tri, rev, sel, cache_k, cache_v, state, sinks, lb_all,
      onorm)

    y = pl.pallas_call(
        functools.partial(_sample_out_kernel, last),
        out_shape=jax.ShapeDtypeStruct((n_tok, D_MODEL), F32),
        compiler_params=params,
        name=f"sample_out{layer}",
    )(x2, mix, w_o, mnorm, w_up, w_dn, fnorm)
    return y.reshape(n_seq, n_new, D_MODEL), nk, nv, ns


def _rope_tables(pos):
    n = pos.shape[0]
    inv_freq = jnp.power(ROPE_THETA, -jnp.arange(ROT_HALF, dtype=F32) * (2.0 / ROT_DIM))
    ang = pos.astype(F32)[:, None] * inv_freq[None, :]
    cos, sin = jnp.cos(ang), jnp.sin(ang)
    rest = HEAD_DIM - ROT_DIM
    zeros_h = jnp.zeros((n, ROT_HALF), F32)
    cos_t = jnp.concatenate([cos, cos, jnp.ones((n, rest), F32)], axis=1)
    sdn_t = jnp.concatenate([zeros_h, sin, jnp.zeros((n, rest), F32)], axis=1)
    sup_t = jnp.concatenate([-sin, zeros_h, jnp.zeros((n, rest), F32)], axis=1)
    reps = LANES // HEAD_DIM
    return tuple(jnp.tile(t, (1, reps)) for t in (cos_t, sdn_t, sup_t))


def _chunk_tri(n, chunk):
    r = jnp.arange(n)
    same = (r[:, None] // chunk) == (r[None, :] // chunk)
    return (same & (r[None, :] <= r[:, None])).astype(BF16)


def kernel(x_prompt, x_sample, cache_k, cache_v, state_hgrn, attn_norm, w_in, att_sinks, hgrn_lower_bounds,
           hgrn_out_norm, w_o, mlp_norm, w_up, w_down, final_norm):
    batch, seq, _ = x_prompt.shape
    n_seq, n_new, _ = x_sample.shape
    assert seq % PROMPT_BLOCK == 0 and PROMPT_BLOCK % HG_CHUNK == 0

    rope_p = _rope_tables(jnp.arange(seq, dtype=jnp.int32))
    rope_s = _rope_tables(PAST_LEN + jnp.arange(n_new, dtype=jnp.int32))
    tri = _chunk_tri(PROMPT_BLOCK, HG_CHUNK)
    lb_all = hgrn_lower_bounds.astype(F32)
    fnorm = final_norm.reshape(1, D_MODEL)

    ck = cache_k.reshape(DEPTH, n_seq, WINDOW, KV_WIDTH)
    cv = cache_v.reshape(DEPTH, n_seq, WINDOW, KV_WIDTH)
    xp, xs = x_prompt, x_sample
    nk_p, nv_p, ns_p, nk_s, nv_s, ns_s = [], [], [], [], [], []
    for l in range(DEPTH):
        weights = (attn_norm[l].reshape(1, D_MODEL), w_in[l].astype(BF16), hgrn_out_norm[l].reshape(1, HG_DV),
                   w_o[l].astype(BF16), mlp_norm[l].reshape(1, D_MODEL), w_up[l].astype(BF16),
                   w_down[l].astype(BF16), fnorm)
        xp, k1, v1, s1 = _prompt_layer(l, xp, rope_p, tri, att_sinks[l], lb_all, *weights)
        xs, k2, v2, s2 = _sample_layer(
            l, xs, rope_s, ck, cv, state_hgrn, att_sinks[l], lb_all, *weights)
        nk_p.append(k1), nv_p.append(v1), ns_p.append(s1)
        nk_s.append(k2), nv_s.append(v2), ns_s.append(s2)

    kv_p = (DEPTH, batch, WINDOW, ATT_KV_HEADS, HEAD_DIM)
    kv_s = (DEPTH, n_seq, WINDOW, ATT_KV_HEADS, HEAD_DIM)
    return (xp, xs,
            jnp.stack(nk_p).reshape(kv_p), jnp.stack(nv_p).reshape(kv_p), jnp.stack(ns_p),
            jnp.stack(nk_s).reshape(kv_s), jnp.stack(nv_s).reshape(kv_s), jnp.stack(ns_s))
```

```python
import functools

import jax
import jax.numpy as jnp
from jax import lax
from jax.experimental import pallas as pl
from jax.experimental.pallas import tpu as pltpu

F32 = jnp.float32
BF16 = jnp.bfloat16

D_MODEL = 1024
DEPTH = 2
PAST_LEN = 16384
ATT_HEADS = 8
ATT_KV_HEADS = 2
HEAD_DIM = 64
GROUP = ATT_HEADS // ATT_KV_HEADS
ATT_WIDTH = ATT_HEADS * HEAD_DIM
KV_WIDTH = ATT_KV_HEADS * HEAD_DIM
WINDOW = 128
ROT_DIM = HEAD_DIM // 4
ROT_HALF = ROT_DIM // 2
ROPE_THETA = 500000.0
HG_HEADS = 4
HG_DK = 128
HG_DV = 128
HG_WIDTH = HG_HEADS * HG_DK
MIX_WIDTH = ATT_WIDTH + HG_WIDTH
IN_WIDTH = ATT_WIDTH + 2 * KV_WIDTH + 4 * HG_WIDTH
D_FF = 4 * D_MODEL
EPS = 1e-6

OFF_Q = 0
OFF_K = ATT_WIDTH
OFF_V = OFF_K + KV_WIDTH
OFF_HQ = OFF_V + KV_WIDTH
OFF_HF = OFF_HQ + HG_WIDTH
OFF_HI = OFF_HF + HG_WIDTH
OFF_HG = OFF_HI + HG_WIDTH

LANES = 128
PROMPT_BLOCK = 256
HG_CHUNK = 128
HG_SUB = 32
FF_BLOCK = 1024
SUBLANES = 8
VMEM_LIMIT = 58 * 1024 * 1024


def _dot(a, b):
    return jnp.dot(a, b, preferred_element_type=F32)


def _dot_nt(a, b):
    return lax.dot_general(a, b, (((1,), (1,)), ((), ())), preferred_element_type=F32)


def _rms(x, g_row):
    ms = jnp.mean(x * x, axis=-1, keepdims=True)
    return (x * lax.rsqrt(ms + EPS)) * g_row


def _rope(x, cos, sin_dn, sin_up):
    return x * cos + pltpu.roll(x, ROT_HALF, 1) * sin_dn + pltpu.roll(x, LANES - ROT_HALF, 1) * sin_up


def _lower_bound(lb_all, layer):
    m = jnp.max(lb_all, axis=0, keepdims=True)
    e = jnp.exp(lb_all - m)
    p = e / jnp.sum(e, axis=0, keepdims=True)
    cs = p[0:1]
    for l in range(1, layer + 1):
        cs = cs + p[l:l + 1]
    return jnp.maximum(cs - p[0:1], 0.0)


def _hgrn_gates(z, lb):
    e = jnp.exp(-jnp.abs(z))
    log_sig = jnp.minimum(z, 0.0) - jnp.log1p(e)
    a1 = jnp.log(lb)
    a2 = jnp.log1p(-lb) + log_sig
    log_f = jnp.maximum(a1, a2) + jnp.log1p(jnp.exp(-jnp.abs(a1 - a2)))
    k_in = (1.0 - lb) * (jnp.where(z >= 0.0, e, 1.0) / (1.0 + e))
    return log_f, k_in


def _silu(x):
    return x / (1.0 + jnp.exp(-x))


def _stacked_queries(proj_ref, cos, sdn, sup):
    t = proj_ref.shape[0]
    lo_half = lax.broadcasted_iota(jnp.int32, (t, LANES), 1) < HEAD_DIM
    q_ext = []
    for hp in range(ATT_HEADS // 2):
        q_slab = _rope(proj_ref[:, OFF_Q + hp * LANES:OFF_Q + (hp + 1) * LANES], cos, sdn, sup)
        q_slab = q_slab * (HEAD_DIM ** -0.5)
        q_swap = pltpu.roll(q_slab, HEAD_DIM, 1)
        for sub in range(2):
            kvh = (2 * hp + sub) // GROUP
            src = q_slab if sub == kvh else q_swap
            keep = lo_half if kvh == 0 else ~lo_half
            q_ext.append(jnp.where(keep, src, 0.0).astype(BF16))
    return q_ext


def _softmax_sink(s, allowed, sink):
    s = jnp.where(allowed, s, -jnp.inf)
    m = jnp.maximum(jnp.max(s, axis=-1, keepdims=True), sink)
    p = jnp.exp(s - m)
    inv = 1.0 / (jnp.sum(p, axis=-1, keepdims=True) + jnp.exp(sink - m))
    return p.astype(BF16), inv


def _merge_head_pair(a, b, kvh):
    lo_half = lax.broadcasted_iota(jnp.int32, a.shape, 1) < HEAD_DIM
    if kvh == 0:
        return jnp.where(lo_half, a, pltpu.roll(b, HEAD_DIM, 1))
    return jnp.where(lo_half, pltpu.roll(a, HEAD_DIM, 1), b)


def _cumsum_rows(tri, g):
    g1 = g.astype(BF16)
    r1 = g - g1.astype(F32)
    g2 = r1.astype(BF16)
    g3 = (r1 - g2.astype(F32)).astype(BF16)
    return _dot(tri, g1) + _dot(tri, g2) + _dot(tri, g3)


def _hgrn_intra_operands(q, k, b):
    c = q.shape[0]
    q_parts, k_parts = [], []
    for i in range(c // HG_SUB):
        lo, hi = i * HG_SUB, (i + 1) * HG_SUB
        r = b[lo - 1:lo] if i > 0 else jnp.zeros((1, HG_DK), F32)
        q_parts.append(q[lo:hi] * jnp.exp(b[lo:hi] - r))
        k_parts.append(k[:hi] * jnp.exp(r - b[:hi]))
        if hi < c:
            k_parts.append(jnp.zeros((c - hi, HG_DK), F32))
    return (jnp.concatenate(q_parts, axis=0).astype(BF16),
            jnp.concatenate(k_parts, axis=0).astype(BF16))


def _hgrn_intra_scores(a_full):
    c = a_full.shape[0]
    a = jnp.concatenate(
        [a_full[i * HG_SUB:(i + 1) * HG_SUB, i * c:(i + 1) * c] for i in range(c // HG_SUB)], axis=0)
    row = lax.broadcasted_iota(jnp.int32, (c, c), 0)
    col = lax.broadcasted_iota(jnp.int32, (c, c), 1)
    return jnp.where(col <= row, a, 0.0).astype(BF16)


def _out_proj(x, mix_bf16, wo_ref, mnorm_ref):
    x1 = x + _dot(mix_bf16, wo_ref[...])
    return x1, _rms(x1, mnorm_ref[...]).astype(BF16)


def _mlp_block(acc, h2, c, wup_ref, wdn_ref):
    u = _dot(h2, wup_ref[:, c * FF_BLOCK:(c + 1) * FF_BLOCK])
    u = jnp.square(jnp.maximum(u, 0.0)).astype(BF16)
    return acc + _dot(u, wdn_ref[c * FF_BLOCK:(c + 1) * FF_BLOCK, :])


def _dense_out(x, mix_bf16, wo_ref, mnorm_ref, wup_ref, wdn_ref, fnorm_ref, last):
    acc, h2 = _out_proj(x, mix_bf16, wo_ref, mnorm_ref)
    for c in range(D_FF // FF_BLOCK):
        acc = _mlp_block(acc, h2, c, wup_ref, wdn_ref)
    return _rms(acc, fnorm_ref[...]) if last else acc


def _prompt_layer_kernel(layer, last, n_tblk,
                         x_ref, cos_ref, sdn_ref, sup_ref, tri_ref, sinks_ref, lb_ref, anorm_ref,
                         win_ref, onorm_ref, wo_ref, mnorm_ref, wup_ref, wdn_ref, fnorm_ref,
                         y_ref, nk_ref, nv_ref, ns_ref,
                         proj_ref, mix_ref, xprev_ref, kprev_ref, vprev_ref, st_ref):
    j = pl.program_id(0)
    n_steps = pl.num_programs(0)
    live = j < n_steps - 1
    t_blk = lax.rem(jnp.minimum(j, n_steps - 2), n_tblk)
    tb = PROMPT_BLOCK

    @pl.when(j == 0)
    def _():
        xprev_ref[...] = jnp.zeros_like(xprev_ref)
        mix_ref[...] = jnp.zeros_like(mix_ref)

    @pl.when(t_blk == 0)
    def _():
        kprev_ref[...] = jnp.zeros_like(kprev_ref)
        vprev_ref[...] = jnp.zeros_like(vprev_ref)
        st_ref[...] = jnp.zeros_like(st_ref)

    acc, h2 = _out_proj(xprev_ref[...], mix_ref[...], wo_ref, mnorm_ref)

    x = x_ref[0]
    xprev_ref[...] = x
    h = _rms(x, anorm_ref[...]).astype(BF16)
    proj_ref[...] = _dot(h, win_ref[...])
    acc = _mlp_block(acc, h2, 0, wup_ref, wdn_ref)

    cos, sdn, sup = cos_ref[...], sdn_ref[...], sup_ref[...]

    k_rot = _rope(proj_ref[:, OFF_K:OFF_K + KV_WIDTH], cos, sdn, sup)
    v_new = proj_ref[:, OFF_V:OFF_V + KV_WIDTH]
    k_all = jnp.concatenate([kprev_ref[...], k_rot], axis=0).astype(BF16)
    v_all = jnp.concatenate([vprev_ref[...], v_new], axis=0).astype(BF16)

    q_ext = _stacked_queries(proj_ref, cos, sdn, sup)
    row = lax.broadcasted_iota(jnp.int32, (WINDOW, 2 * WINDOW), 0)
    col = lax.broadcasted_iota(jnp.int32, (WINDOW, 2 * WINDOW), 1)
    in_window = (col >= row) & (col <= row + WINDOW)
    first_lo = jnp.where(t_blk == 0, WINDOW, 0)
    att = [[None] * (tb // WINDOW) for _ in range(ATT_HEADS)]
    for n in range(tb // WINDOW):
        rows_n = slice(n * WINDOW, (n + 1) * WINDOW)
        keys_n = slice(n * WINDOW, (n + 2) * WINDOW)
        q_stack = jnp.concatenate([q[rows_n] for q in q_ext], axis=0)
        s_all = _dot_nt(q_stack, k_all[keys_n])
        allowed = in_window & (col >= first_lo) if n == 0 else in_window
        soft = [_softmax_sink(s_all[head * WINDOW:(head + 1) * WINDOW], allowed, sinks_ref[head])
                for head in range(ATT_HEADS)]
        o_all = _dot(jnp.concatenate([p for p, _ in soft], axis=0), v_all[keys_n])
        for head in range(ATT_HEADS):
            att[head][n] = o_all[head * WINDOW:(head + 1) * WINDOW] * soft[head][1]
    for hp in range(ATT_HEADS // 2):
        a, b = (jnp.concatenate(att[2 * hp + sub], axis=0) for sub in range(2))
        mix_ref[:, hp * LANES:(hp + 1) * LANES] = _merge_head_pair(a, b, (2 * hp) // GROUP).astype(BF16)
    acc = _mlp_block(acc, h2, 1, wup_ref, wdn_ref)

    lb = _lower_bound(lb_ref[...], layer)
    log_f, k_in = _hgrn_gates(proj_ref[:, OFF_HF:OFF_HF + HG_WIDTH], lb)
    b_all = _cumsum_rows(tri_ref[...], log_f)
    onorm = onorm_ref[...]
    n_chunks = tb // HG_CHUNK
    units = [(hh, c) for c in range(n_chunks) for hh in range(HG_HEADS)]
    q_in, v_in = [], []
    for hh in range(HG_HEADS):
        q_in.append(_silu(proj_ref[:, OFF_HQ + hh * HG_DK:OFF_HQ + (hh + 1) * HG_DK]))
        v_in.append(proj_ref[:, OFF_HI + hh * HG_DV:OFF_HI + (hh + 1) * HG_DV])

    def piece(x, hh, c, lanes=False):
        x = x[:, hh * HG_DK:(hh + 1) * HG_DK] if lanes else x[hh]
        return x[c * HG_CHUNK:(c + 1) * HG_CHUNK]

    intra_ops = {u: _hgrn_intra_operands(piece(q_in, *u), piece(k_in, *u, lanes=True),
                                         piece(b_all, *u, lanes=True)) for u in units}
    a_full = {u: _dot_nt(*intra_ops[u]) for u in units}
    acc = _mlp_block(acc, h2, 2, wup_ref, wdn_ref)
    o_inter, st_new = {}, []
    for hh in range(HG_HEADS):
        st = st_ref[hh]
        for c in range(n_chunks):
            q, k, v, b = (piece(q_in, hh, c), piece(k_in, hh, c, True), piece(v_in, hh, c),
                          piece(b_all, hh, c, True))
            o_inter[(hh, c)] = _dot_nt((q * jnp.exp(b)).astype(BF16), st.astype(BF16))
            b_last = b[HG_CHUNK - 1:HG_CHUNK]
            k_hat = (k * jnp.exp(b_last - b)).astype(BF16)
            st = st * jnp.exp(b_last) + _dot(v.T.astype(BF16), k_hat)
        st_new.append(st)
    a_mask = {u: _hgrn_intra_scores(a_full[u]) for u in units}
    o_intra = {u: _dot(a_mask[u], piece(v_in, *u).astype(BF16)) for u in units}
    for hh in range(HG_HEADS):
        o = jnp.concatenate([o_intra[(hh, c)] + o_inter[(hh, c)] for c in range(n_chunks)], axis=0)
        gate = _silu(proj_ref[:, OFF_HG + hh * HG_DV:OFF_HG + (hh + 1) * HG_DV])
        mix_ref[:, ATT_WIDTH + hh * HG_DV:ATT_WIDTH + (hh + 1) * HG_DV] = (
            _rms(o, onorm) * gate).astype(BF16)
    acc = _mlp_block(acc, h2, 3, wup_ref, wdn_ref)
    y_ref[0] = _rms(acc, fnorm_ref[...]) if last else acc

    @pl.when(live)
    def _():
        kprev_ref[...] = k_rot[tb - WINDOW:]
        vprev_ref[...] = v_new[tb - WINDOW:]
        nk_ref[0] = k_rot[tb - WINDOW:]
        nv_ref[0] = v_new[tb - WINDOW:]
        for hh in range(HG_HEADS):
            st_ref[hh] = st_new[hh]

    @pl.when(live & (t_blk == n_tblk - 1))
    def _():
        for hh in range(HG_HEADS):
            ns_ref[0, hh] = st_new[hh].T


def _const_spec(shape):
    nd = len(shape)
    return pl.BlockSpec(shape, lambda *_: (0,) * nd, pipeline_mode=pl.Buffered(1))


def _prompt_layer(layer, x, rope, tri, sinks, lb_all, anorm, w_in, onorm, w_o, mnorm, w_up, w_dn, fnorm):
    batch, seq, _ = x.shape
    tb = PROMPT_BLOCK
    n_tblk = seq // tb
    n_blocks = batch * n_tblk
    last = layer == DEPTH - 1
    mixer_blk = lambda j: jnp.minimum(j, n_blocks - 1)
    tail_blk = lambda j: jnp.maximum(j - 1, 0)
    rope_spec = pl.BlockSpec((tb, LANES), lambda j: (mixer_blk(j) % n_tblk, 0))
    carry_spec = pl.BlockSpec((1, WINDOW, KV_WIDTH), lambda j: (mixer_blk(j) // n_tblk, 0, 0))
    return pl.pallas_call(
        functools.partial(_prompt_layer_kernel, layer, last, n_tblk),
        grid=(n_blocks + 1,),
        in_specs=[
            pl.BlockSpec((1, tb, D_MODEL), lambda j: (mixer_blk(j) // n_tblk, mixer_blk(j) % n_tblk, 0)),
            rope_spec, rope_spec, rope_spec,
            _const_spec((tb, tb)),
            pl.BlockSpec(memory_space=pltpu.SMEM),
            _const_spec((DEPTH, HG_WIDTH)),
            _const_spec((1, D_MODEL)),
            _const_spec((D_MODEL, IN_WIDTH)),
            _const_spec((1, HG_DV)),
            _const_spec((MIX_WIDTH, D_MODEL)),
            _const_spec((1, D_MODEL)),
            _const_spec((D_MODEL, D_FF)),
            _const_spec((D_FF, D_MODEL)),
            _const_spec((1, D_MODEL)),
        ],
        out_specs=[
            pl.BlockSpec((1, tb, D_MODEL), lambda j: (tail_blk(j) // n_tblk, tail_blk(j) % n_tblk, 0)),
            carry_spec, carry_spec,
            pl.BlockSpec((1, HG_HEADS, HG_DK, HG_DV), lambda j: (mixer_blk(j) // n_tblk, 0, 0, 0)),
        ],
        out_shape=[
            jax.ShapeDtypeStruct((batch, seq, D_MODEL), F32),
            jax.ShapeDtypeStruct((batch, WINDOW, KV_WIDTH), F32),
            jax.ShapeDtypeStruct((batch, WINDOW, KV_WIDTH), F32),
            jax.ShapeDtypeStruct((batch, HG_HEADS, HG_DK, HG_DV), F32),
        ],
        scratch_shapes=[
            pltpu.VMEM((tb, IN_WIDTH), F32),
            pltpu.VMEM((tb, MIX_WIDTH), BF16),
            pltpu.VMEM((tb, D_MODEL), F32),
            pltpu.VMEM((WINDOW, KV_WIDTH), F32),
            pltpu.VMEM((WINDOW, KV_WIDTH), F32),
            pltpu.VMEM((HG_HEADS, HG_DV, HG_DK), F32),
        ],
        compiler_params=pltpu.CompilerParams(
            dimension_semantics=("arbitrary",), vmem_limit_bytes=VMEM_LIMIT),
        name=f"prompt_layer{layer}",
    )(x, *rope, tri, sinks, lb_all, anorm, w_in, onorm, w_o, mnorm, w_up, w_dn, fnorm)


def _sample_in_kernel(x_ref, anorm_ref, win_ref, proj_ref):
    h = _rms(x_ref[...], anorm_ref[...]).astype(BF16)
    proj_ref[...] = _dot(h, win_ref[...])


def _sample_out_kernel(last, x_ref, mix_ref, wo_ref, mnorm_ref, wup_ref, wdn_ref, fnorm_ref, y_ref):
    y_ref[...] = _dense_out(x_ref[...], mix_ref[...].astype(BF16), wo_ref, mnorm_ref, wup_ref,
                            wdn_ref, fnorm_ref, last)


def _sample_mixer_kernel(layer, n_new,
                         p_ref, cos_ref, sdn_ref, sup_ref, tri_ref, rev_ref, sel_ref, ck_ref, cv_ref,
                         s0_ref, sinks_ref, lb_ref, onorm_ref,
                         mix_ref, nk_ref, nv_ref, ns_ref,
                         pad_ref):
    sb = ck_ref.shape[0]
    ts = p_ref.shape[0]
    tok_shift = n_new.bit_length() - 1
    key_shift = WINDOW.bit_length() - 1
    n_cache = sb * WINDOW
    cos, sdn, sup = cos_ref[...], sdn_ref[...], sup_ref[...]

    k_new = _rope(p_ref[:, OFF_K:OFF_K + KV_WIDTH], cos, sdn, sup)
    v_new = p_ref[:, OFF_V:OFF_V + KV_WIDTH]
    pad_keys = jnp.zeros((LANES - ts, KV_WIDTH), F32)
    k_all = jnp.concatenate([ck_ref[...].reshape(n_cache, KV_WIDTH), k_new, pad_keys], axis=0).astype(BF16)
    v_all = jnp.concatenate([cv_ref[...].reshape(n_cache, KV_WIDTH), v_new, pad_keys], axis=0).astype(BF16)
    n_keys = n_cache + LANES
    row = lax.broadcasted_iota(jnp.int32, (ts, n_keys), 0)
    col = lax.broadcasted_iota(jnp.int32, (ts, n_keys), 1)
    new_idx = col - n_cache
    cached_ok = ((col >> key_shift) == (row >> tok_shift)) & ((col & (WINDOW - 1)) >= (row & (n_new - 1)))
    new_ok = ((new_idx < ts) & ((new_idx >> tok_shift) == (row >> tok_shift))
              & ((new_idx & (n_new - 1)) <= (row & (n_new - 1))))
    allowed = ((col < n_cache) & cached_ok) | ((col >= n_cache) & new_ok)

    q_ext = _stacked_queries(p_ref, cos, sdn, sup)
    s_all = _dot_nt(jnp.concatenate(q_ext, axis=0), k_all)
    soft = [_softmax_sink(s_all[head * ts:(head + 1) * ts], allowed, sinks_ref[head])
            for head in range(ATT_HEADS)]
    o_all = _dot(jnp.concatenate([p for p, _ in soft], axis=0), v_all)
    att = [o_all[head * ts:(head + 1) * ts] * soft[head][1] for head in range(ATT_HEADS)]
    for hp in range(ATT_HEADS // 2):
        mix_ref[:, hp * LANES:(hp + 1) * LANES] = _merge_head_pair(
            att[2 * hp], att[2 * hp + 1], (2 * hp) // GROUP)
    for s in range(sb):
        nk_ref[s, 0:WINDOW - n_new] = ck_ref[s, n_new:WINDOW]
        nv_ref[s, 0:WINDOW - n_new] = cv_ref[s, n_new:WINDOW]
        nk_ref[s, WINDOW - n_new:WINDOW] = k_new[s * n_new:(s + 1) * n_new]
        nv_ref[s, WINDOW - n_new:WINDOW] = v_new[s * n_new:(s + 1) * n_new]

    lb = _lower_bound(lb_ref[...], layer)
    log_f, k_in = _hgrn_gates(p_ref[:, OFF_HF:OFF_HF + HG_WIDTH], lb)
    b = _cumsum_rows(tri_ref[...], log_f)
    later = _cumsum_rows(rev_ref[...], log_f)
    q_in = _silu(p_ref[:, OFF_HQ:OFF_HQ + HG_WIDTH])
    v_in = p_ref[:, OFF_HI:OFF_HI + HG_WIDTH]
    gate = _silu(p_ref[:, OFF_HG:OFF_HG + HG_WIDTH])
    onorm = onorm_ref[...]
    heads = [slice(hh * HG_DK, (hh + 1) * HG_DK) for hh in range(HG_HEADS)]

    pad_ref[...] = jnp.zeros_like(pad_ref)
    for i, x in enumerate((k_in, b, v_in)):
        pad_ref[i, SUBLANES:SUBLANES + ts] = x
    tok = lax.broadcasted_iota(jnp.int32, (ts, 1), 0) & (n_new - 1)
    o_intra = [jnp.zeros((ts, HG_DV), F32) for _ in heads]
    for d in range(n_new):
        k_d, b_d, v_d = (pad_ref[i, SUBLANES - d:SUBLANES - d + ts] for i in range(3))
        w = q_in * k_d * jnp.exp(jnp.minimum(b - b_d, 0.0))
        for hh, sl in enumerate(heads):
            a_col = jnp.sum(w[:, sl], axis=-1, keepdims=True)
            o_intra[hh] = o_intra[hh] + jnp.where(tok >= d, a_col, 0.0) * v_d[:, sl]

    q_dec = (q_in * jnp.exp(b)).astype(BF16)
    seq_of_row = lax.broadcasted_iota(jnp.int32, (ts, 1), 0) >> tok_shift
    for hh, sl in enumerate(heads):
        o = o_intra[hh]
        for s in range(sb):
            o_s = _dot(q_dec[:, sl], s0_ref[s, hh].astype(BF16))
            o = o + jnp.where(seq_of_row == s, o_s, 0.0)
        mix_ref[:, ATT_WIDTH + hh * HG_DV:ATT_WIDTH + (hh + 1) * HG_DV] = _rms(o, onorm) * gate[:, sl]

    k_hat = k_in * jnp.exp(later)
    e_b = jnp.exp(b)
    e1 = e_b.astype(BF16).astype(F32)
    e2 = (e_b - e1).astype(BF16).astype(F32)
    e3 = e_b - e1 - e2
    own = ((lax.broadcasted_iota(jnp.int32, (ts, sb * HG_DV), 1) >> key_shift)
           == (lax.broadcasted_iota(jnp.int32, (ts, sb * HG_DV), 0) >> tok_shift))
    no_v = jnp.zeros((3 * ts, sb * HG_DV), F32)
    for hh, sl in enumerate(heads):
        lhs = jnp.concatenate([k_hat[:, sl], e1[:, sl], e2[:, sl], e3[:, sl]], axis=0).T.astype(BF16)
        v_rep = jnp.concatenate([v_in[:, sl]] * sb, axis=1)
        v_diag = jnp.concatenate([jnp.where(own, v_rep, 0.0), no_v], axis=0).astype(BF16)
        upd = _dot(lhs, v_diag)
        decay = _dot(lhs, sel_ref[...])
        for s in range(sb):
            cols = slice(s * HG_DV, (s + 1) * HG_DV)
            ns_ref[s, hh] = decay[:, cols] * s0_ref[s, hh] + upd[:, cols]


def _sample_layer(layer, x, rope, cache_k, cache_v, state, sinks, lb_all, anorm, w_in, onorm, w_o, mnorm,
                  w_up, w_dn, fnorm):
    n_seq, n_new, _ = x.shape
    n_tok = n_seq * n_new
    ts = LANES // 4
    sb = ts // n_new
    assert sb * n_new == ts and n_seq % sb == 0 and n_new <= SUBLANES
    assert n_new & (n_new - 1) == 0 and WINDOW & (WINDOW - 1) == 0 and HG_DV == WINDOW
    last = layer == DEPTH - 1
    x2 = x.reshape(n_tok, D_MODEL)
    params = pltpu.CompilerParams(vmem_limit_bytes=VMEM_LIMIT)

    proj = pl.pallas_call(
        _sample_in_kernel,
        out_shape=jax.ShapeDtypeStruct((n_tok, IN_WIDTH), F32),
        compiler_params=params,
        name=f"sample_in{layer}",
    )(x2, anorm, w_in)

    r = jnp.arange(ts)
    same_seq = (r[:, None] // n_new) == (r[None, :] // n_new)
    tri = (same_seq & (r[None, :] <= r[:, None])).astype(BF16)
    rev = (same_seq & (r[None, :] > r[:, None])).astype(BF16)
    c = jnp.arange(sb * HG_DV)
    is_last = ((r[:, None] % n_new) == n_new - 1) & ((r[:, None] // n_new) == (c[None, :] // HG_DV))
    sel = jnp.concatenate([jnp.zeros_like(is_last)] + [is_last] * 3, axis=0).astype(BF16)

    tok_spec = lambda width: pl.BlockSpec((ts, width), lambda i: (i, 0))
    seq_spec = lambda *tail: pl.BlockSpec((sb,) + tail, lambda i: (i,) + (0,) * len(tail))
    layer_spec = lambda *tail: pl.BlockSpec((None, sb) + tail, lambda i: (layer, i) + (0,) * len(tail))
    rope_spec = _const_spec((ts, LANES))
    mix, nk, nv, ns = pl.pallas_call(
        functools.partial(_sample_mixer_kernel, layer, n_new),
        grid=(n_seq // sb,),
        in_specs=[
            tok_spec(IN_WIDTH),
            rope_spec, rope_spec, rope_spec,
            _const_spec((ts, ts)), _const_spec((ts, ts)), _const_spec((4 * ts, sb * HG_DV)),
            layer_spec(WINDOW, KV_WIDTH), layer_spec(WINDOW, KV_WIDTH),
            layer_spec(HG_HEADS, HG_DK, HG_DV),
            pl.BlockSpec(memory_space=pltpu.SMEM),
            _const_spec((DEPTH, HG_WIDTH)),
            _const_spec((1, HG_DV)),
        ],
        out_specs=[
            tok_spec(MIX_WIDTH),
            seq_spec(WINDOW, KV_WIDTH), seq_spec(WINDOW, KV_WIDTH),
            seq_spec(HG_HEADS, HG_DK, HG_DV),
        ],
        out_shape=[
            jax.ShapeDtypeStruct((n_tok, MIX_WIDTH), F32),
            jax.ShapeDtypeStruct((n_seq, WINDOW, KV_WIDTH), F32),
            jax.ShapeDtypeStruct((n_seq, WINDOW, KV_WIDTH), F32),
            jax.ShapeDtypeStruct((n_seq, HG_HEADS, HG_DK, HG_DV), F32),
        ],
        scratch_shapes=[pltpu.VMEM((3, SUBLANES + ts, HG_WIDTH), F32)],
        compiler_params=pltpu.CompilerParams(
            dimension_semantics=("arbitrary",), vmem_limit_bytes=VMEM_LIMIT),
        name=f"sample_mixer{layer}",
    )(proj, *(jnp.tile(t, (sb, 1)) for t in rope), tri, rev, sel, cache_k, cache_v, state, sinks, lb_all,
      onorm)

    y = pl.pallas_call(
        functools.partial(_sample_out_kernel, last),
        out_shape=jax.ShapeDtypeStruct((n_tok, D_MODEL), F32),
        compiler_params=params,
        name=f"sample_out{layer}",
    )(x2, mix, w_o, mnorm, w_up, w_dn, fnorm)
    return y.reshape(n_seq, n_new, D_MODEL), nk, nv, ns


def _rope_tables(pos):
    n = pos.shape[0]
    inv_freq = jnp.power(ROPE_THETA, -jnp.arange(ROT_HALF, dtype=F32) * (2.0 / ROT_DIM))
    ang = pos.astype(F32)[:, None] * inv_freq[None, :]
    cos, sin = jnp.cos(ang), jnp.sin(ang)
    rest = HEAD_DIM - ROT_DIM
    zeros_h = jnp.zeros((n, ROT_HALF), F32)
    cos_t = jnp.concatenate([cos, cos, jnp.ones((n, rest), F32)], axis=1)
    sdn_t = jnp.concatenate([zeros_h, sin, jnp.zeros((n, rest), F32)], axis=1)
    sup_t = jnp.concatenate([-sin, zeros_h, jnp.zeros((n, rest), F32)], axis=1)
    reps = LANES // HEAD_DIM
    return tuple(jnp.tile(t, (1, reps)) for t in (cos_t, sdn_t, sup_t))


def _chunk_tri(n, chunk):
    r = jnp.arange(n)
    same = (r[:, None] // chunk) == (r[None, :] // chunk)
    return (same & (r[None, :] <= r[:, None])).astype(BF16)


def kernel(x_prompt, x_sample, cache_k, cache_v, state_hgrn, attn_norm, w_in, att_sinks, hgrn_lower_bounds,
           hgrn_out_norm, w_o, mlp_norm, w_up, w_down, final_norm):
    batch, seq, _ = x_prompt.shape
    n_seq, n_new, _ = x_sample.shape
    assert seq % PROMPT_BLOCK == 0 and PROMPT_BLOCK % HG_CHUNK == 0

    rope_p = _rope_tables(jnp.arange(seq, dtype=jnp.int32))
    rope_s = _rope_tables(PAST_LEN + jnp.arange(n_new, dtype=jnp.int32))
    tri = _chunk_tri(PROMPT_BLOCK, HG_CHUNK)
    lb_all = hgrn_lower_bounds.astype(F32)
    fnorm = final_norm.reshape(1, D_MODEL)

    ck = cache_k.reshape(DEPTH, n_seq, WINDOW, KV_WIDTH)
    cv = cache_v.reshape(DEPTH, n_seq, WINDOW, KV_WIDTH)
    xp, xs = x_prompt, x_sample
    nk_p, nv_p, ns_p, nk_s, nv_s, ns_s = [], [], [], [], [], []
    for l in range(DEPTH):
        weights = (attn_norm[l].reshape(1, D_MODEL), w_in[l].astype(BF16), hgrn_out_norm[l].reshape(1, HG_DV),
                   w_o[l].astype(BF16), mlp_norm[l].reshape(1, D_MODEL), w_up[l].astype(BF16),
                   w_down[l].astype(BF16), fnorm)
        xp, k1, v1, s1 = _prompt_layer(l, xp, rope_p, tri, att_sinks[l], lb_all, *weights)
        xs, k2, v2, s2 = _sample_layer(
            l, xs, rope_s, ck, cv, state_hgrn, att_sinks[l], lb_all, *weights)
        nk_p.append(k1), nv_p.append(v1), ns_p.append(s1)
        nk_s.append(k2), nv_s.append(v2), ns_s.append(s2)

    kv_p = (DEPTH, batch, WINDOW, ATT_KV_HEADS, HEAD_DIM)
    kv_s = (DEPTH, n_seq, WINDOW, ATT_KV_HEADS, HEAD_DIM)
    return (xp, xs,
            jnp.stack(nk_p).reshape(kv_p), jnp.stack(nv_p).reshape(kv_p), jnp.stack(ns_p),
            jnp.stack(nk_s).reshape(kv_s), jnp.stack(nv_s).reshape(kv_s), jnp.stack(ns_s))
```

```python
import functools

import jax
import jax.numpy as jnp
from jax import lax
from jax.experimental import pallas as pl
from jax.experimental.pallas import tpu as pltpu

F32 = jnp.float32
BF16 = jnp.bfloat16

D_MODEL = 1024
DEPTH = 2
PAST_LEN = 16384
ATT_HEADS = 8
ATT_KV_HEADS = 2
HEAD_DIM = 64
GROUP = ATT_HEADS // ATT_KV_HEADS
ATT_WIDTH = ATT_HEADS * HEAD_DIM
KV_WIDTH = ATT_KV_HEADS * HEAD_DIM
WINDOW = 128
ROT_DIM = HEAD_DIM // 4
ROT_HALF = ROT_DIM // 2
ROPE_THETA = 500000.0
HG_HEADS = 4
HG_DK = 128
HG_DV = 128
HG_WIDTH = HG_HEADS * HG_DK
MIX_WIDTH = ATT_WIDTH + HG_WIDTH
IN_WIDTH = ATT_WIDTH + 2 * KV_WIDTH + 4 * HG_WIDTH
D_FF = 4 * D_MODEL
EPS = 1e-6

OFF_Q = 0
OFF_K = ATT_WIDTH
OFF_V = OFF_K + KV_WIDTH
OFF_HQ = OFF_V + KV_WIDTH
OFF_HF = OFF_HQ + HG_WIDTH
OFF_HI = OFF_HF + HG_WIDTH
OFF_HG = OFF_HI + HG_WIDTH

LANES = 128
PROMPT_BLOCK = 256
HG_CHUNK = 128
HG_SUB = 32
HG_GUARD = 80.0
FF_BLOCK = 1024
SUBLANES = 8
VMEM_LIMIT = 58 * 1024 * 1024


def _dot(a, b):
    return jnp.dot(a, b, preferred_element_type=F32)


def _dot_nt(a, b):
    return lax.dot_general(a, b, (((1,), (1,)), ((), ())), preferred_element_type=F32)


def _rms(x, g_row):
    ms = jnp.mean(x * x, axis=-1, keepdims=True)
    return (x * lax.rsqrt(ms + EPS)) * g_row


def _rope(x, cos, sin_dn, sin_up):
    return x * cos + pltpu.roll(x, ROT_HALF, 1) * sin_dn + pltpu.roll(x, LANES - ROT_HALF, 1) * sin_up


def _lower_bound(lb_all, layer):
    m = jnp.max(lb_all, axis=0, keepdims=True)
    e = jnp.exp(lb_all - m)
    p = e / jnp.sum(e, axis=0, keepdims=True)
    cs = p[0:1]
    for l in range(1, layer + 1):
        cs = cs + p[l:l + 1]
    return jnp.maximum(cs - p[0:1], 0.0)


def _hgrn_gates(z, lb):
    e = jnp.exp(-jnp.abs(z))
    log_sig = jnp.minimum(z, 0.0) - jnp.log1p(e)
    a1 = jnp.log(lb)
    a2 = jnp.log1p(-lb) + log_sig
    log_f = jnp.maximum(a1, a2) + jnp.log1p(jnp.exp(-jnp.abs(a1 - a2)))
    k_in = (1.0 - lb) * (jnp.where(z >= 0.0, e, 1.0) / (1.0 + e))
    return log_f, k_in


def _silu(x):
    return x / (1.0 + jnp.exp(-x))


def _stacked_queries(proj_ref, cos, sdn, sup):
    t = proj_ref.shape[0]
    lo_half = lax.broadcasted_iota(jnp.int32, (t, LANES), 1) < HEAD_DIM
    q_ext = []
    for hp in range(ATT_HEADS // 2):
        q_slab = _rope(proj_ref[:, OFF_Q + hp * LANES:OFF_Q + (hp + 1) * LANES], cos, sdn, sup)
        q_slab = q_slab * (HEAD_DIM ** -0.5)
        q_swap = pltpu.roll(q_slab, HEAD_DIM, 1)
        for sub in range(2):
            kvh = (2 * hp + sub) // GROUP
            src = q_slab if sub == kvh else q_swap
            keep = lo_half if kvh == 0 else ~lo_half
            q_ext.append(jnp.where(keep, src, 0.0).astype(BF16))
    return q_ext


def _softmax_sink(s, allowed, sink):
    s = jnp.where(allowed, s, -jnp.inf)
    m = jnp.maximum(jnp.max(s, axis=-1, keepdims=True), sink)
    p = jnp.exp(s - m)
    inv = 1.0 / (jnp.sum(p, axis=-1, keepdims=True) + jnp.exp(sink - m))
    return p.astype(BF16), inv


def _merge_head_pair(a, b, kvh):
    lo_half = lax.broadcasted_iota(jnp.int32, a.shape, 1) < HEAD_DIM
    if kvh == 0:
        return jnp.where(lo_half, a, pltpu.roll(b, HEAD_DIM, 1))
    return jnp.where(lo_half, pltpu.roll(a, HEAD_DIM, 1), b)


def _cumsum_rows(tri, g):
    g1 = g.astype(BF16)
    r1 = g - g1.astype(F32)
    g2 = r1.astype(BF16)
    g3 = (r1 - g2.astype(F32)).astype(BF16)
    return _dot(tri, g1) + _dot(tri, g2) + _dot(tri, g3)


def _hgrn_intra_operands(q, k, b):
    c = q.shape[0]
    q_parts, k_parts = [], []
    for i in range(c // HG_SUB):
        lo, hi = i * HG_SUB, (i + 1) * HG_SUB
        r = b[lo - 1:lo] if i > 0 else jnp.zeros((1, HG_DK), F32)
        q_parts.append(q[lo:hi] * jnp.exp(b[lo:hi] - r))
        k_parts.append(k[:hi] * jnp.exp(r - b[:hi]))
        if hi < c:
            k_parts.append(jnp.zeros((c - hi, HG_DK), F32))
    return (jnp.concatenate(q_parts, axis=0).astype(BF16),
            jnp.concatenate(k_parts, axis=0).astype(BF16))


def _hgrn_worst_subblock_decay(b_all):
    worst = None
    for lo in range(0, b_all.shape[0], HG_SUB):
        tot = b_all[lo + HG_SUB - 1:lo + HG_SUB]
        if lo % HG_CHUNK:
            tot = tot - b_all[lo - 1:lo]
        worst = tot if worst is None else jnp.minimum(worst, tot)
    return worst


def _hgrn_intra_exact(q, k, v, b, work_ref, out_ref):
    for i, x in enumerate((q, k, v, b)):
        work_ref[i] = x
    s_idx = lax.broadcasted_iota(jnp.int32, (HG_CHUNK, 1), 0)

    def row(t, carry):
        c0 = pl.multiple_of((t // HG_CHUNK) * HG_CHUNK, HG_CHUNK)
        q_t = work_ref[0, pl.ds(t, 1), :]
        b_t = work_ref[3, pl.ds(t, 1), :]
        k_c = work_ref[1, pl.ds(c0, HG_CHUNK), :]
        v_c = work_ref[2, pl.ds(c0, HG_CHUNK), :]
        b_c = work_ref[3, pl.ds(c0, HG_CHUNK), :]
        w = k_c * jnp.exp(jnp.minimum(b_t - b_c, 0.0)) * q_t
        a = jnp.where(s_idx <= t - c0, jnp.sum(w, axis=-1, keepdims=True), 0.0)
        out_ref[pl.ds(t, 1), :] = jnp.sum(a * v_c, axis=0, keepdims=True)
        return carry

    lax.fori_loop(0, q.shape[0], row, 0)
    return out_ref[...]


def _hgrn_intra_scores(a_full):
    c = a_full.shape[0]
    a = jnp.concatenate(
        [a_full[i * HG_SUB:(i + 1) * HG_SUB, i * c:(i + 1) * c] for i in range(c // HG_SUB)], axis=0)
    row = lax.broadcasted_iota(jnp.int32, (c, c), 0)
    col = lax.broadcasted_iota(jnp.int32, (c, c), 1)
    return jnp.where(col <= row, a, 0.0).astype(BF16)


def _out_proj(x, mix_bf16, wo_ref, mnorm_ref):
    x1 = x + _dot(mix_bf16, wo_ref[...])
    return x1, _rms(x1, mnorm_ref[...]).astype(BF16)


def _mlp_block(acc, h2, c, wup_ref, wdn_ref):
    u = _dot(h2, wup_ref[:, c * FF_BLOCK:(c + 1) * FF_BLOCK])
    u = jnp.square(jnp.maximum(u, 0.0)).astype(BF16)
    return acc + _dot(u, wdn_ref[c * FF_BLOCK:(c + 1) * FF_BLOCK, :])


def _dense_out(x, mix_bf16, wo_ref, mnorm_ref, wup_ref, wdn_ref, fnorm_ref, last):
    acc, h2 = _out_proj(x, mix_bf16, wo_ref, mnorm_ref)
    for c in range(D_FF // FF_BLOCK):
        acc = _mlp_block(acc, h2, c, wup_ref, wdn_ref)
    return _rms(acc, fnorm_ref[...]) if last else acc


def _prompt_layer_kernel(layer, last, n_tblk,
                         x_ref, cos_ref, sdn_ref, sup_ref, tri_ref, sinks_ref, lb_ref, anorm_ref,
                         win_ref, onorm_ref, wo_ref, mnorm_ref, wup_ref, wdn_ref, fnorm_ref,
                         y_ref, nk_ref, nv_ref, ns_ref,
                         proj_ref, mix_ref, xprev_ref, kprev_ref, vprev_ref, st_ref, work_ref, exact_ref):
    j = pl.program_id(0)
    n_steps = pl.num_programs(0)
    live = j < n_steps - 1
    t_blk = lax.rem(jnp.minimum(j, n_steps - 2), n_tblk)
    tb = PROMPT_BLOCK

    @pl.when(j == 0)
    def _():
        xprev_ref[...] = jnp.zeros_like(xprev_ref)
        mix_ref[...] = jnp.zeros_like(mix_ref)

    @pl.when(t_blk == 0)
    def _():
        kprev_ref[...] = jnp.zeros_like(kprev_ref)
        vprev_ref[...] = jnp.zeros_like(vprev_ref)
        st_ref[...] = jnp.zeros_like(st_ref)

    acc, h2 = _out_proj(xprev_ref[...], mix_ref[...], wo_ref, mnorm_ref)

    x = x_ref[0]
    xprev_ref[...] = x
    h = _rms(x, anorm_ref[...]).astype(BF16)
    proj_ref[...] = _dot(h, win_ref[...])
    acc = _mlp_block(acc, h2, 0, wup_ref, wdn_ref)

    cos, sdn, sup = cos_ref[...], sdn_ref[...], sup_ref[...]

    k_rot = _rope(proj_ref[:, OFF_K:OFF_K + KV_WIDTH], cos, sdn, sup)
    v_new = proj_ref[:, OFF_V:OFF_V + KV_WIDTH]
    k_all = jnp.concatenate([kprev_ref[...], k_rot], axis=0).astype(BF16)
    v_all = jnp.concatenate([vprev_ref[...], v_new], axis=0).astype(BF16)

    q_ext = _stacked_queries(proj_ref, cos, sdn, sup)
    row = lax.broadcasted_iota(jnp.int32, (WINDOW, 2 * WINDOW), 0)
    col = lax.broadcasted_iota(jnp.int32, (WINDOW, 2 * WINDOW), 1)
    in_window = (col >= row) & (col <= row + WINDOW)
    first_lo = jnp.where(t_blk == 0, WINDOW, 0)
    att = [[None] * (tb // WINDOW) for _ in range(ATT_HEADS)]
    for n in range(tb // WINDOW):
        rows_n = slice(n * WINDOW, (n + 1) * WINDOW)
        keys_n = slice(n * WINDOW, (n + 2) * WINDOW)
        q_stack = jnp.concatenate([q[rows_n] for q in q_ext], axis=0)
        s_all = _dot_nt(q_stack, k_all[keys_n])
        allowed = in_window & (col >= first_lo) if n == 0 else in_window
        soft = [_softmax_sink(s_all[head * WINDOW:(head + 1) * WINDOW], allowed, sinks_ref[head])
                for head in range(ATT_HEADS)]
        o_all = _dot(jnp.concatenate([p for p, _ in soft], axis=0), v_all[keys_n])
        for head in range(ATT_HEADS):
            att[head][n] = o_all[head * WINDOW:(head + 1) * WINDOW] * soft[head][1]
    for hp in range(ATT_HEADS // 2):
        a, b = (jnp.concatenate(att[2 * hp + sub], axis=0) for sub in range(2))
        mix_ref[:, hp * LANES:(hp + 1) * LANES] = _merge_head_pair(a, b, (2 * hp) // GROUP).astype(BF16)
    acc = _mlp_block(acc, h2, 1, wup_ref, wdn_ref)

    lb = _lower_bound(lb_ref[...], layer)
    log_f, k_in = _hgrn_gates(proj_ref[:, OFF_HF:OFF_HF + HG_WIDTH], lb)
    b_all = _cumsum_rows(tri_ref[...], log_f)
    onorm = onorm_ref[...]
    n_chunks = tb // HG_CHUNK
    units = [(hh, c) for c in range(n_chunks) for hh in range(HG_HEADS)]
    q_in, v_in = [], []
    for hh in range(HG_HEADS):
        q_in.append(_silu(proj_ref[:, OFF_HQ + hh * HG_DK:OFF_HQ + (hh + 1) * HG_DK]))
        v_in.append(proj_ref[:, OFF_HI + hh * HG_DV:OFF_HI + (hh + 1) * HG_DV])

    def piece(x, hh, c, lanes=False):
        x = x[:, hh * HG_DK:(hh + 1) * HG_DK] if lanes else x[hh]
        return x[c * HG_CHUNK:(c + 1) * HG_CHUNK]

    intra_ops = {u: _hgrn_intra_operands(piece(q_in, *u), piece(k_in, *u, lanes=True),
                                         piece(b_all, *u, lanes=True)) for u in units}
    a_full = {u: _dot_nt(*intra_ops[u]) for u in units}
    acc = _mlp_block(acc, h2, 2, wup_ref, wdn_ref)
    o_inter, st_new = {}, []
    for hh in range(HG_HEADS):
        st = st_ref[hh]
        for c in range(n_chunks):
            q, k, v, b = (piece(q_in, hh, c), piece(k_in, hh, c, True), piece(v_in, hh, c),
                          piece(b_all, hh, c, True))
            o_inter[(hh, c)] = _dot_nt((q * jnp.exp(b)).astype(BF16), st.astype(BF16))
            b_last = b[HG_CHUNK - 1:HG_CHUNK]
            k_hat = (k * jnp.exp(b_last - b)).astype(BF16)
            st = st * jnp.exp(b_last) + _dot(v.T.astype(BF16), k_hat)
        st_new.append(st)
    a_mask = {u: _hgrn_intra_scores(a_full[u]) for u in units}
    o_intra = {u: _dot(a_mask[u], piece(v_in, *u).astype(BF16)) for u in units}

    def emit_head(hh, o_intra_h):
        o = o_intra_h + jnp.concatenate([o_inter[(hh, c)] for c in range(n_chunks)], axis=0)
        gate = _silu(proj_ref[:, OFF_HG + hh * HG_DV:OFF_HG + (hh + 1) * HG_DV])
        mix_ref[:, ATT_WIDTH + hh * HG_DV:ATT_WIDTH + (hh + 1) * HG_DV] = (
            _rms(o, onorm) * gate).astype(BF16)

    for hh in range(HG_HEADS):
        emit_head(hh, jnp.concatenate([o_intra[(hh, c)] for c in range(n_chunks)], axis=0))
    acc = _mlp_block(acc, h2, 3, wup_ref, wdn_ref)
    y_ref[0] = _rms(acc, fnorm_ref[...]) if last else acc

    @pl.when(jnp.min(_hgrn_worst_subblock_decay(b_all)) < -HG_GUARD)
    def _():
        for hh in range(HG_HEADS):
            sl = slice(hh * HG_DK, (hh + 1) * HG_DK)
            emit_head(hh, _hgrn_intra_exact(q_in[hh], k_in[:, sl], v_in[hh], b_all[:, sl],
                                            work_ref, exact_ref))

    @pl.when(live)
    def _():
        kprev_ref[...] = k_rot[tb - WINDOW:]
        vprev_ref[...] = v_new[tb - WINDOW:]
        nk_ref[0] = k_rot[tb - WINDOW:]
        nv_ref[0] = v_new[tb - WINDOW:]
        for hh in range(HG_HEADS):
            st_ref[hh] = st_new[hh]

    @pl.when(live & (t_blk == n_tblk - 1))
    def _():
        for hh in range(HG_HEADS):
            ns_ref[0, hh] = st_new[hh].T


def _const_spec(shape):
    nd = len(shape)
    return pl.BlockSpec(shape, lambda *_: (0,) * nd, pipeline_mode=pl.Buffered(1))


def _prompt_layer(layer, x, rope, tri, sinks, lb_all, anorm, w_in, onorm, w_o, mnorm, w_up, w_dn, fnorm):
    batch, seq, _ = x.shape
    tb = PROMPT_BLOCK
    n_tblk = seq // tb
    n_blocks = batch * n_tblk
    last = layer == DEPTH - 1
    mixer_blk = lambda j: jnp.minimum(j, n_blocks - 1)
    tail_blk = lambda j: jnp.maximum(j - 1, 0)
    rope_spec = pl.BlockSpec((tb, LANES), lambda j: (mixer_blk(j) % n_tblk, 0))
    carry_spec = pl.BlockSpec((1, WINDOW, KV_WIDTH), lambda j: (mixer_blk(j) // n_tblk, 0, 0))
    return pl.pallas_call(
        functools.partial(_prompt_layer_kernel, layer, last, n_tblk),
        grid=(n_blocks + 1,),
        in_specs=[
            pl.BlockSpec((1, tb, D_MODEL), lambda j: (mixer_blk(j) // n_tblk, mixer_blk(j) % n_tblk, 0)),
            rope_spec, rope_spec, rope_spec,
            _const_spec((tb, tb)),
            pl.BlockSpec(memory_space=pltpu.SMEM),
            _const_spec((DEPTH, HG_WIDTH)),
            _const_spec((1, D_MODEL)),
            _const_spec((D_MODEL, IN_WIDTH)),
            _const_spec((1, HG_DV)),
            _const_spec((MIX_WIDTH, D_MODEL)),
            _const_spec((1, D_MODEL)),
            _const_spec((D_MODEL, D_FF)),
            _const_spec((D_FF, D_MODEL)),
            _const_spec((1, D_MODEL)),
        ],
        out_specs=[
            pl.BlockSpec((1, tb, D_MODEL), lambda j: (tail_blk(j) // n_tblk, tail_blk(j) % n_tblk, 0)),
            carry_spec, carry_spec,
            pl.BlockSpec((1, HG_HEADS, HG_DK, HG_DV), lambda j: (mixer_blk(j) // n_tblk, 0, 0, 0)),
        ],
        out_shape=[
            jax.ShapeDtypeStruct((batch, seq, D_MODEL), F32),
            jax.ShapeDtypeStruct((batch, WINDOW, KV_WIDTH), F32),
            jax.ShapeDtypeStruct((batch, WINDOW, KV_WIDTH), F32),
            jax.ShapeDtypeStruct((batch, HG_HEADS, HG_DK, HG_DV), F32),
        ],
        scratch_shapes=[
            pltpu.VMEM((tb, IN_WIDTH), F32),
            pltpu.VMEM((tb, MIX_WIDTH), BF16),
            pltpu.VMEM((tb, D_MODEL), F32),
            pltpu.VMEM((WINDOW, KV_WIDTH), F32),
            pltpu.VMEM((WINDOW, KV_WIDTH), F32),
            pltpu.VMEM((HG_HEADS, HG_DV, HG_DK), F32),
            pltpu.VMEM((4, tb, HG_DK), F32),
            pltpu.VMEM((tb, HG_DV), F32),
        ],
        compiler_params=pltpu.CompilerParams(
            dimension_semantics=("arbitrary",), vmem_limit_bytes=VMEM_LIMIT),
        name=f"prompt_layer{layer}",
    )(x, *rope, tri, sinks, lb_all, anorm, w_in, onorm, w_o, mnorm, w_up, w_dn, fnorm)


def _sample_in_kernel(x_ref, anorm_ref, win_ref, proj_ref):
    h = _rms(x_ref[...], anorm_ref[...]).astype(BF16)
    proj_ref[...] = _dot(h, win_ref[...])


def _sample_out_kernel(last, x_ref, mix_ref, wo_ref, mnorm_ref, wup_ref, wdn_ref, fnorm_ref, y_ref):
    y_ref[...] = _dense_out(x_ref[...], mix_ref[...].astype(BF16), wo_ref, mnorm_ref, wup_ref,
                            wdn_ref, fnorm_ref, last)


def _sample_mixer_kernel(layer, n_new,
                         p_ref, cos_ref, sdn_ref, sup_ref, tri_ref, rev_ref, sel_ref, ck_ref, cv_ref,
                         s0_ref, sinks_ref, lb_ref, onorm_ref,
                         mix_ref, nk_ref, nv_ref, ns_ref,
                         pad_ref):
    sb = ck_ref.shape[0]
    ts = p_ref.shape[0]
    tok_shift = n_new.bit_length() - 1
    key_shift = WINDOW.bit_length() - 1
    n_cache = sb * WINDOW
    cos, sdn, sup = cos_ref[...], sdn_ref[...], sup_ref[...]

    k_new = _rope(p_ref[:, OFF_K:OFF_K + KV_WIDTH], cos, sdn, sup)
    v_new = p_ref[:, OFF_V:OFF_V + KV_WIDTH]
    pad_keys = jnp.zeros((LANES - ts, KV_WIDTH), F32)
    k_all = jnp.concatenate([ck_ref[...].reshape(n_cache, KV_WIDTH), k_new, pad_keys], axis=0).astype(BF16)
    v_all = jnp.concatenate([cv_ref[...].reshape(n_cache, KV_WIDTH), v_new, pad_keys], axis=0).astype(BF16)
    n_keys = n_cache + LANES
    row = lax.broadcasted_iota(jnp.int32, (ts, n_keys), 0)
    col = lax.broadcasted_iota(jnp.int32, (ts, n_keys), 1)
    new_idx = col - n_cache
    cached_ok = ((col >> key_shift) == (row >> tok_shift)) & ((col & (WINDOW - 1)) >= (row & (n_new - 1)))
    new_ok = ((new_idx < ts) & ((new_idx >> tok_shift) == (row >> tok_shift))
              & ((new_idx & (n_new - 1)) <= (row & (n_new - 1))))
    allowed = ((col < n_cache) & cached_ok) | ((col >= n_cache) & new_ok)

    q_ext = _stacked_queries(p_ref, cos, sdn, sup)
    s_all = _dot_nt(jnp.concatenate(q_ext, axis=0), k_all)
    soft = [_softmax_sink(s_all[head * ts:(head + 1) * ts], allowed, sinks_ref[head])
            for head in range(ATT_HEADS)]
    o_all = _dot(jnp.concatenate([p for p, _ in soft], axis=0), v_all)
    att = [o_all[head * ts:(head + 1) * ts] * soft[head][1] for head in range(ATT_HEADS)]
    for hp in range(ATT_HEADS // 2):
        mix_ref[:, hp * LANES:(hp + 1) * LANES] = _merge_head_pair(
            att[2 * hp], att[2 * hp + 1], (2 * hp) // GROUP)
    for s in range(sb):
        nk_ref[s, 0:WINDOW - n_new] = ck_ref[s, n_new:WINDOW]
        nv_ref[s, 0:WINDOW - n_new] = cv_ref[s, n_new:WINDOW]
        nk_ref[s, WINDOW - n_new:WINDOW] = k_new[s * n_new:(s + 1) * n_new]
        nv_ref[s, WINDOW - n_new:WINDOW] = v_new[s * n_new:(s + 1) * n_new]

    lb = _lower_bound(lb_ref[...], layer)
    log_f, k_in = _hgrn_gates(p_ref[:, OFF_HF:OFF_HF + HG_WIDTH], lb)
    b = _cumsum_rows(tri_ref[...], log_f)
    later = _cumsum_rows(rev_ref[...], log_f)
    q_in = _silu(p_ref[:, OFF_HQ:OFF_HQ + HG_WIDTH])
    v_in = p_ref[:, OFF_HI:OFF_HI + HG_WIDTH]
    gate = _silu(p_ref[:, OFF_HG:OFF_HG + HG_WIDTH])
    onorm = onorm_ref[...]
    heads = [slice(hh * HG_DK, (hh + 1) * HG_DK) for hh in range(HG_HEADS)]

    pad_ref[...] = jnp.zeros_like(pad_ref)
    for i, x in enumerate((k_in, b, v_in)):
        pad_ref[i, SUBLANES:SUBLANES + ts] = x
    tok = lax.broadcasted_iota(jnp.int32, (ts, 1), 0) & (n_new - 1)
    o_intra = [jnp.zeros((ts, HG_DV), F32) for _ in heads]
    for d in range(n_new):
        k_d, b_d, v_d = (pad_ref[i, SUBLANES - d:SUBLANES - d + ts] for i in range(3))
        w = q_in * k_d * jnp.exp(jnp.minimum(b - b_d, 0.0))
        for hh, sl in enumerate(heads):
            a_col = jnp.sum(w[:, sl], axis=-1, keepdims=True)
            o_intra[hh] = o_intra[hh] + jnp.where(tok >= d, a_col, 0.0) * v_d[:, sl]

    q_dec = (q_in * jnp.exp(b)).astype(BF16)
    seq_of_row = lax.broadcasted_iota(jnp.int32, (ts, 1), 0) >> tok_shift
    for hh, sl in enumerate(heads):
        o = o_intra[hh]
        for s in range(sb):
            o_s = _dot(q_dec[:, sl], s0_ref[s, hh].astype(BF16))
            o = o + jnp.where(seq_of_row == s, o_s, 0.0)
        mix_ref[:, ATT_WIDTH + hh * HG_DV:ATT_WIDTH + (hh + 1) * HG_DV] = _rms(o, onorm) * gate[:, sl]

    k_hat = k_in * jnp.exp(later)
    e_b = jnp.exp(b)
    e1 = e_b.astype(BF16).astype(F32)
    e2 = (e_b - e1).astype(BF16).astype(F32)
    e3 = e_b - e1 - e2
    own = ((lax.broadcasted_iota(jnp.int32, (ts, sb * HG_DV), 1) >> key_shift)
           == (lax.broadcasted_iota(jnp.int32, (ts, sb * HG_DV), 0) >> tok_shift))
    no_v = jnp.zeros((3 * ts, sb * HG_DV), F32)
    for hh, sl in enumerate(heads):
        lhs = jnp.concatenate([k_hat[:, sl], e1[:, sl], e2[:, sl], e3[:, sl]], axis=0).T.astype(BF16)
        v_rep = jnp.concatenate([v_in[:, sl]] * sb, axis=1)
        v_diag = jnp.concatenate([jnp.where(own, v_rep, 0.0), no_v], axis=0).astype(BF16)
        upd = _dot(lhs, v_diag)
        decay = _dot(lhs, sel_ref[...])
        for s in range(sb):
            cols = slice(s * HG_DV, (s + 1) * HG_DV)
            ns_ref[s, hh] = decay[:, cols] * s0_ref[s, hh] + upd[:, cols]


def _sample_layer(layer, x, rope, cache_k, cache_v, state, sinks, lb_all, anorm, w_in, onorm, w_o, mnorm,
                  w_up, w_dn, fnorm):
    n_seq, n_new, _ = x.shape
    n_tok = n_seq * n_new
    ts = LANES // 4
    sb = ts // n_new
    assert sb * n_new == ts and n_seq % sb == 0 and n_new <= SUBLANES
    assert n_new & (n_new - 1) == 0 and WINDOW & (WINDOW - 1) == 0 and HG_DV == WINDOW
    last = layer == DEPTH - 1
    x2 = x.reshape(n_tok, D_MODEL)
    params = pltpu.CompilerParams(vmem_limit_bytes=VMEM_LIMIT)

    proj = pl.pallas_call(
        _sample_in_kernel,
        out_shape=jax.ShapeDtypeStruct((n_tok, IN_WIDTH), F32),
        compiler_params=params,
        name=f"sample_in{layer}",
    )(x2, anorm, w_in)

    r = jnp.arange(ts)
    same_seq = (r[:, None] // n_new) == (r[None, :] // n_new)
    tri = (same_seq & (r[None, :] <= r[:, None])).astype(BF16)
    rev = (same_seq & (r[None, :] > r[:, None])).astype(BF16)
    c = jnp.arange(sb * HG_DV)
    is_last = ((r[:, None] % n_new) == n_new - 1) & ((r[:, None] // n_new) == (c[None, :] // HG_DV))
    sel = jnp.concatenate([jnp.zeros_like(is_last)] + [is_last] * 3, axis=0).astype(BF16)

    tok_spec = lambda width: pl.BlockSpec((ts, width), lambda i: (i, 0))
    seq_spec = lambda *tail: pl.BlockSpec((sb,) + tail, lambda i: (i,) + (0,) * len(tail))
    layer_spec = lambda *tail: pl.BlockSpec((None, sb) + tail, lambda i: (layer, i) + (0,) * len(tail))
    rope_spec = _const_spec((ts, LANES))
    mix, nk, nv, ns = pl.pallas_call(
        functools.partial(_sample_mixer_kernel, layer, n_new),
        grid=(n_seq // sb,),
        in_specs=[
            tok_spec(IN_WIDTH),
            rope_spec, rope_spec, rope_spec,
            _const_spec((ts, ts)), _const_spec((ts, ts)), _const_spec((4 * ts, sb * HG_DV)),
            layer_spec(WINDOW, KV_WIDTH), layer_spec(WINDOW, KV_WIDTH),
            layer_spec(HG_HEADS, HG_DK, HG_DV),
            pl.BlockSpec(memory_space=pltpu.SMEM),
            _const_spec((DEPTH, HG_WIDTH)),
            _const_spec((1, HG_DV)),
        ],
        out_specs=[
            tok_spec(MIX_WIDTH),
            seq_spec(WINDOW, KV_WIDTH), seq_spec(WINDOW, KV_WIDTH),
            seq_spec(HG_HEADS, HG_DK, HG_DV),
        ],
        out_shape=[
            jax.ShapeDtypeStruct((n_tok, MIX_WIDTH), F32),
            jax.ShapeDtypeStruct((n_seq, WINDOW, KV_WIDTH), F32),
            jax.ShapeDtypeStruct((n_seq, WINDOW, KV_WIDTH), F32),
            jax.ShapeDtypeStruct((n_seq, HG_HEADS, HG_DK, HG_DV), F32),
        ],
        scratch_shapes=[pltpu.VMEM((3, SUBLANES + ts, HG_WIDTH), F32)],
        compiler_params=pltpu.CompilerParams(
            dimension_semantics=("arbitrary",), vmem_limit_bytes=VMEM_LIMIT),
        name=f"sample_mixer{layer}",
    )(proj, *(jnp.tile(t, (sb, 1)) for t in rope), tri, rev, sel, cache_k, cache_v, state, sinks, lb_all,
      onorm)

    y = pl.pallas_call(
        functools.partial(_sample_out_kernel, last),
        out_shape=jax.ShapeDtypeStruct((n_tok, D_MODEL), F32),
        compiler_params=params,
        name=f"sample_out{layer}",
    )(x2, mix, w_o, mnorm, w_up, w_dn, fnorm)
    return y.reshape(n_seq, n_new, D_MODEL), nk, nv, ns


def _rope_tables(pos):
    n = pos.shape[0]
    inv_freq = jnp.power(ROPE_THETA, -jnp.arange(ROT_HALF, dtype=F32) * (2.0 / ROT_DIM))
    ang = pos.astype(F32)[:, None] * inv_freq[None, :]
    cos, sin = jnp.cos(ang), jnp.sin(ang)
    rest = HEAD_DIM - ROT_DIM
    zeros_h = jnp.zeros((n, ROT_HALF), F32)
    cos_t = jnp.concatenate([cos, cos, jnp.ones((n, rest), F32)], axis=1)
    sdn_t = jnp.concatenate([zeros_h, sin, jnp.zeros((n, rest), F32)], axis=1)
    sup_t = jnp.concatenate([-sin, zeros_h, jnp.zeros((n, rest), F32)], axis=1)
    reps = LANES // HEAD_DIM
    return tuple(jnp.tile(t, (1, reps)) for t in (cos_t, sdn_t, sup_t))


def _chunk_tri(n, chunk):
    r = jnp.arange(n)
    same = (r[:, None] // chunk) == (r[None, :] // chunk)
    return (same & (r[None, :] <= r[:, None])).astype(BF16)


def kernel(x_prompt, x_sample, cache_k, cache_v, state_hgrn, attn_norm, w_in, att_sinks, hgrn_lower_bounds,
           hgrn_out_norm, w_o, mlp_norm, w_up, w_down, final_norm):
    batch, seq, _ = x_prompt.shape
    n_seq, n_new, _ = x_sample.shape
    assert seq % PROMPT_BLOCK == 0 and PROMPT_BLOCK % HG_CHUNK == 0

    rope_p = _rope_tables(jnp.arange(seq, dtype=jnp.int32))
    rope_s = _rope_tables(PAST_LEN + jnp.arange(n_new, dtype=jnp.int32))
    tri = _chunk_tri(PROMPT_BLOCK, HG_CHUNK)
    lb_all = hgrn_lower_bounds.astype(F32)
    fnorm = final_norm.reshape(1, D_MODEL)

    ck = cache_k.reshape(DEPTH, n_seq, WINDOW, KV_WIDTH)
    cv = cache_v.reshape(DEPTH, n_seq, WINDOW, KV_WIDTH)
    xp, xs = x_prompt, x_sample
    nk_p, nv_p, ns_p, nk_s, nv_s, ns_s = [], [], [], [], [], []
    for l in range(DEPTH):
        weights = (attn_norm[l].reshape(1, D_MODEL), w_in[l].astype(BF16), hgrn_out_norm[l].reshape(1, HG_DV),
                   w_o[l].astype(BF16), mlp_norm[l].reshape(1, D_MODEL), w_up[l].astype(BF16),
                   w_down[l].astype(BF16), fnorm)
        xp, k1, v1, s1 = _prompt_layer(l, xp, rope_p, tri, att_sinks[l], lb_all, *weights)
        xs, k2, v2, s2 = _sample_layer(
            l, xs, rope_s, ck, cv, state_hgrn, att_sinks[l], lb_all, *weights)
        nk_p.append(k1), nv_p.append(v1), ns_p.append(s1)
        nk_s.append(k2), nv_s.append(v2), ns_s.append(s2)

    kv_p = (DEPTH, batch, WINDOW, ATT_KV_HEADS, HEAD_DIM)
    kv_s = (DEPTH, n_seq, WINDOW, ATT_KV_HEADS, HEAD_DIM)
    return (xp, xs,
            jnp.stack(nk_p).reshape(kv_p), jnp.stack(nv_p).reshape(kv_p), jnp.stack(ns_p),
            jnp.stack(nk_s).reshape(kv_s), jnp.stack(nv_s).reshape(kv_s), jnp.stack(ns_s))
```

```python
import functools

import jax
import jax.numpy as jnp
from jax import lax
from jax.experimental import pallas as pl
from jax.experimental.pallas import tpu as pltpu

F32 = jnp.float32
BF16 = jnp.bfloat16

D_MODEL = 1024
DEPTH = 2
PAST_LEN = 16384
ATT_HEADS = 8
ATT_KV_HEADS = 2
HEAD_DIM = 64
GROUP = ATT_HEADS // ATT_KV_HEADS
ATT_WIDTH = ATT_HEADS * HEAD_DIM
KV_WIDTH = ATT_KV_HEADS * HEAD_DIM
WINDOW = 128
ROT_DIM = HEAD_DIM // 4
ROT_HALF = ROT_DIM // 2
ROPE_THETA = 500000.0
HG_HEADS = 4
HG_DK = 128
HG_DV = 128
HG_WIDTH = HG_HEADS * HG_DK
MIX_WIDTH = ATT_WIDTH + HG_WIDTH
IN_WIDTH = ATT_WIDTH + 2 * KV_WIDTH + 4 * HG_WIDTH
D_FF = 4 * D_MODEL
EPS = 1e-6

OFF_Q = 0
OFF_K = ATT_WIDTH
OFF_V = OFF_K + KV_WIDTH
OFF_HQ = OFF_V + KV_WIDTH
OFF_HF = OFF_HQ + HG_WIDTH
OFF_HI = OFF_HF + HG_WIDTH
OFF_HG = OFF_HI + HG_WIDTH

LANES = 128
PROMPT_BLOCK = 256
HG_CHUNK = 128
HG_SUB = 32
HG_GUARD = 80.0
FF_BLOCK = 2048
SUBLANES = 8
SAMPLE_MIXER_INPUTS = 13
VMEM_LIMIT = 58 * 1024 * 1024


def _dot(a, b):
    return jnp.dot(a, b, preferred_element_type=F32)


def _dot_nt(a, b):
    return lax.dot_general(a, b, (((1,), (1,)), ((), ())), preferred_element_type=F32)


def _rms(x, g_row):
    ms = jnp.mean(x * x, axis=-1, keepdims=True)
    return (x * lax.rsqrt(ms + EPS)) * g_row


def _rope(x, cos, sin_dn, sin_up):
    return x * cos + pltpu.roll(x, ROT_HALF, 1) * sin_dn + pltpu.roll(x, LANES - ROT_HALF, 1) * sin_up


def _lower_bound(lb_all, layer):
    m = jnp.max(lb_all, axis=0, keepdims=True)
    e = jnp.exp(lb_all - m)
    p = e / jnp.sum(e, axis=0, keepdims=True)
    cs = p[0:1]
    for l in range(1, layer + 1):
        cs = cs + p[l:l + 1]
    return jnp.maximum(cs - p[0:1], 0.0)


def _hgrn_gates(z, lb):
    e = jnp.exp(-jnp.abs(z))
    log_sig = jnp.minimum(z, 0.0) - jnp.log1p(e)
    a1 = jnp.log(lb)
    a2 = jnp.log1p(-lb) + log_sig
    log_f = jnp.maximum(a1, a2) + jnp.log1p(jnp.exp(-jnp.abs(a1 - a2)))
    k_in = (1.0 - lb) * (jnp.where(z >= 0.0, e, 1.0) / (1.0 + e))
    return log_f, k_in


def _silu(x):
    return x / (1.0 + jnp.exp(-x))


def _stacked_queries(proj_ref, cos, sdn, sup):
    t = proj_ref.shape[0]
    lo_half = lax.broadcasted_iota(jnp.int32, (t, LANES), 1) < HEAD_DIM
    q_ext = []
    for hp in range(ATT_HEADS // 2):
        q_slab = _rope(proj_ref[:, OFF_Q + hp * LANES:OFF_Q + (hp + 1) * LANES], cos, sdn, sup)
        q_slab = q_slab * (HEAD_DIM ** -0.5)
        q_swap = pltpu.roll(q_slab, HEAD_DIM, 1)
        for sub in range(2):
            kvh = (2 * hp + sub) // GROUP
            src = q_slab if sub == kvh else q_swap
            keep = lo_half if kvh == 0 else ~lo_half
            q_ext.append(jnp.where(keep, src, 0.0).astype(BF16))
    return q_ext


def _softmax_sink(s, allowed, sink):
    s = jnp.where(allowed, s, -jnp.inf)
    m = jnp.maximum(jnp.max(s, axis=-1, keepdims=True), sink)
    p = jnp.exp(s - m)
    inv = 1.0 / (jnp.sum(p, axis=-1, keepdims=True) + jnp.exp(sink - m))
    return p.astype(BF16), inv


def _merge_head_pair(a, b, kvh):
    lo_half = lax.broadcasted_iota(jnp.int32, a.shape, 1) < HEAD_DIM
    if kvh == 0:
        return jnp.where(lo_half, a, pltpu.roll(b, HEAD_DIM, 1))
    return jnp.where(lo_half, pltpu.roll(a, HEAD_DIM, 1), b)


def _cumsum_rows(tri, g):
    g1 = g.astype(BF16)
    r1 = g - g1.astype(F32)
    g2 = r1.astype(BF16)
    g3 = (r1 - g2.astype(F32)).astype(BF16)
    return _dot(tri, g1) + _dot(tri, g2) + _dot(tri, g3)


def _hgrn_intra_operands(q, k, b):
    c = q.shape[0]
    q_parts, k_parts = [], []
    for i in range(c // HG_SUB):
        lo, hi = i * HG_SUB, (i + 1) * HG_SUB
        r = b[lo - 1:lo] if i > 0 else jnp.zeros((1, HG_DK), F32)
        q_parts.append(q[lo:hi] * jnp.exp(b[lo:hi] - r))
        k_parts.append(k[:hi] * jnp.exp(r - b[:hi]))
        if hi < c:
            k_parts.append(jnp.zeros((c - hi, HG_DK), F32))
    return (jnp.concatenate(q_parts, axis=0).astype(BF16),
            jnp.concatenate(k_parts, axis=0).astype(BF16))


def _hgrn_worst_subblock_decay(b_all):
    worst = None
    for lo in range(0, b_all.shape[0], HG_SUB):
        tot = b_all[lo + HG_SUB - 1:lo + HG_SUB]
        if lo % HG_CHUNK:
            tot = tot - b_all[lo - 1:lo]
        worst = tot if worst is None else jnp.minimum(worst, tot)
    return worst


def _hgrn_intra_exact(q, k, v, b, work_ref, out_ref):
    for i, x in enumerate((q, k, v, b)):
        work_ref[i] = x
    s_idx = lax.broadcasted_iota(jnp.int32, (HG_CHUNK, 1), 0)

    def row(t, carry):
        c0 = pl.multiple_of((t // HG_CHUNK) * HG_CHUNK, HG_CHUNK)
        q_t = work_ref[0, pl.ds(t, 1), :]
        b_t = work_ref[3, pl.ds(t, 1), :]
        k_c = work_ref[1, pl.ds(c0, HG_CHUNK), :]
        v_c = work_ref[2, pl.ds(c0, HG_CHUNK), :]
        b_c = work_ref[3, pl.ds(c0, HG_CHUNK), :]
        w = k_c * jnp.exp(jnp.minimum(b_t - b_c, 0.0)) * q_t
        a = jnp.where(s_idx <= t - c0, jnp.sum(w, axis=-1, keepdims=True), 0.0)
        out_ref[pl.ds(t, 1), :] = jnp.sum(a * v_c, axis=0, keepdims=True)
        return carry

    lax.fori_loop(0, q.shape[0], row, 0)
    return out_ref[...]


def _hgrn_intra_scores(a_full):
    c = a_full.shape[0]
    a = jnp.concatenate(
        [a_full[i * HG_SUB:(i + 1) * HG_SUB, i * c:(i + 1) * c] for i in range(c // HG_SUB)], axis=0)
    row = lax.broadcasted_iota(jnp.int32, (c, c), 0)
    col = lax.broadcasted_iota(jnp.int32, (c, c), 1)
    return jnp.where(col <= row, a, 0.0).astype(BF16)


def _out_proj(x, mix_bf16, wo_ref, mnorm_ref):
    x1 = x + _dot(mix_bf16, wo_ref[...])
    return x1, _rms(x1, mnorm_ref[...]).astype(BF16)


def _mlp_block(acc, h2, c, wup_ref, wdn_ref):
    u = _dot(h2, wup_ref[:, c * FF_BLOCK:(c + 1) * FF_BLOCK])
    u = jnp.square(jnp.maximum(u, 0.0)).astype(BF16)
    return acc + _dot(u, wdn_ref[c * FF_BLOCK:(c + 1) * FF_BLOCK, :])


def _dense_out(x, mix_bf16, wo_ref, mnorm_ref, wup_ref, wdn_ref, fnorm_ref, last):
    acc, h2 = _out_proj(x, mix_bf16, wo_ref, mnorm_ref)
    for c in range(D_FF // FF_BLOCK):
        acc = _mlp_block(acc, h2, c, wup_ref, wdn_ref)
    return _rms(acc, fnorm_ref[...]) if last else acc


def _prompt_layer_kernel(layer, last, n_tblk,
                         x_ref, cos_ref, sdn_ref, sup_ref, tri_ref, sinks_ref, lb_ref, anorm_ref,
                         win_ref, onorm_ref, wo_ref, mnorm_ref, wup_ref, wdn_ref, fnorm_ref,
                         y_ref, nk_ref, nv_ref, ns_ref,
                         proj_ref, mix_ref, xprev_ref, kprev_ref, vprev_ref, st_ref, work_ref, exact_ref):
    j = pl.program_id(0)
    n_steps = pl.num_programs(0)
    live = j < n_steps - 1
    t_blk = lax.rem(jnp.minimum(j, n_steps - 2), n_tblk)
    tb = PROMPT_BLOCK

    @pl.when(j == 0)
    def _():
        xprev_ref[...] = jnp.zeros_like(xprev_ref)
        mix_ref[...] = jnp.zeros_like(mix_ref)

    @pl.when(t_blk == 0)
    def _():
        kprev_ref[...] = jnp.zeros_like(kprev_ref)
        vprev_ref[...] = jnp.zeros_like(vprev_ref)
        st_ref[...] = jnp.zeros_like(st_ref)

    acc, h2 = _out_proj(xprev_ref[...], mix_ref[...], wo_ref, mnorm_ref)

    x = x_ref[0]
    xprev_ref[...] = x
    h = _rms(x, anorm_ref[...]).astype(BF16)
    proj_ref[...] = _dot(h, win_ref[...])

    cos, sdn, sup = cos_ref[...], sdn_ref[...], sup_ref[...]

    k_rot = _rope(proj_ref[:, OFF_K:OFF_K + KV_WIDTH], cos, sdn, sup)
    v_new = proj_ref[:, OFF_V:OFF_V + KV_WIDTH]
    k_all = jnp.concatenate([kprev_ref[...], k_rot], axis=0).astype(BF16)
    v_all = jnp.concatenate([vprev_ref[...], v_new], axis=0).astype(BF16)

    q_ext = _stacked_queries(proj_ref, cos, sdn, sup)
    row = lax.broadcasted_iota(jnp.int32, (WINDOW, 2 * WINDOW), 0)
    col = lax.broadcasted_iota(jnp.int32, (WINDOW, 2 * WINDOW), 1)
    in_window = (col >= row) & (col <= row + WINDOW)
    first_lo = jnp.where(t_blk == 0, WINDOW, 0)
    att = [[None] * (tb // WINDOW) for _ in range(ATT_HEADS)]
    for n in range(tb // WINDOW):
        rows_n = slice(n * WINDOW, (n + 1) * WINDOW)
        keys_n = slice(n * WINDOW, (n + 2) * WINDOW)
        q_stack = jnp.concatenate([q[rows_n] for q in q_ext], axis=0)
        s_all = _dot_nt(q_stack, k_all[keys_n])
        allowed = in_window & (col >= first_lo) if n == 0 else in_window
        soft = [_softmax_sink(s_all[head * WINDOW:(head + 1) * WINDOW], allowed, sinks_ref[head])
                for head in range(ATT_HEADS)]
        o_all = _dot(jnp.concatenate([p for p, _ in soft], axis=0), v_all[keys_n])
        for head in range(ATT_HEADS):
            att[head][n] = o_all[head * WINDOW:(head + 1) * WINDOW] * soft[head][1]
    for hp in range(ATT_HEADS // 2):
        a, b = (jnp.concatenate(att[2 * hp + sub], axis=0) for sub in range(2))
        mix_ref[:, hp * LANES:(hp + 1) * LANES] = _merge_head_pair(a, b, (2 * hp) // GROUP).astype(BF16)
    acc = _mlp_block(acc, h2, 0, wup_ref, wdn_ref)

    lb = _lower_bound(lb_ref[...], layer)
    log_f, k_in = _hgrn_gates(proj_ref[:, OFF_HF:OFF_HF + HG_WIDTH], lb)
    b_all = _cumsum_rows(tri_ref[...], log_f)
    onorm = onorm_ref[...]
    n_chunks = tb // HG_CHUNK
    units = [(hh, c) for c in range(n_chunks) for hh in range(HG_HEADS)]
    q_in, v_in = [], []
    for hh in range(HG_HEADS):
        q_in.append(_silu(proj_ref[:, OFF_HQ + hh * HG_DK:OFF_HQ + (hh + 1) * HG_DK]))
        v_in.append(proj_ref[:, OFF_HI + hh * HG_DV:OFF_HI + (hh + 1) * HG_DV])

    def piece(x, hh, c, lanes=False):
        x = x[:, hh * HG_DK:(hh + 1) * HG_DK] if lanes else x[hh]
        return x[c * HG_CHUNK:(c + 1) * HG_CHUNK]

    intra_ops = {u: _hgrn_intra_operands(piece(q_in, *u), piece(k_in, *u, lanes=True),
                                         piece(b_all, *u, lanes=True)) for u in units}
    a_full = {u: _dot_nt(*intra_ops[u]) for u in units}
    o_inter, st_new = {}, []
    for hh in range(HG_HEADS):
        st = st_ref[hh]
        for c in range(n_chunks):
            q, k, v, b = (piece(q_in, hh, c), piece(k_in, hh, c, True), piece(v_in, hh, c),
                          piece(b_all, hh, c, True))
            o_inter[(hh, c)] = _dot_nt((q * jnp.exp(b)).astype(BF16), st.astype(BF16))
            b_last = b[HG_CHUNK - 1:HG_CHUNK]
            k_hat = (k * jnp.exp(b_last - b)).astype(BF16)
            st = st * jnp.exp(b_last) + _dot(v.T.astype(BF16), k_hat)
        st_new.append(st)
    a_mask = {u: _hgrn_intra_scores(a_full[u]) for u in units}
    o_intra = {u: _dot(a_mask[u], piece(v_in, *u).astype(BF16)) for u in units}

    def emit_head(hh, o_intra_h):
        o = o_intra_h + jnp.concatenate([o_inter[(hh, c)] for c in range(n_chunks)], axis=0)
        gate = _silu(proj_ref[:, OFF_HG + hh * HG_DV:OFF_HG + (hh + 1) * HG_DV])
        mix_ref[:, ATT_WIDTH + hh * HG_DV:ATT_WIDTH + (hh + 1) * HG_DV] = (
            _rms(o, onorm) * gate).astype(BF16)

    for hh in range(HG_HEADS):
        emit_head(hh, jnp.concatenate([o_intra[(hh, c)] for c in range(n_chunks)], axis=0))
    acc = _mlp_block(acc, h2, 1, wup_ref, wdn_ref)
    y_ref[0] = _rms(acc, fnorm_ref[...]) if last else acc

    @pl.when(jnp.min(_hgrn_worst_subblock_decay(b_all)) < -HG_GUARD)
    def _():
        for hh in range(HG_HEADS):
            sl = slice(hh * HG_DK, (hh + 1) * HG_DK)
            emit_head(hh, _hgrn_intra_exact(q_in[hh], k_in[:, sl], v_in[hh], b_all[:, sl],
                                            work_ref, exact_ref))

    @pl.when(live)
    def _():
        kprev_ref[...] = k_rot[tb - WINDOW:]
        vprev_ref[...] = v_new[tb - WINDOW:]
        nk_ref[0] = k_rot[tb - WINDOW:]
        nv_ref[0] = v_new[tb - WINDOW:]
        for hh in range(HG_HEADS):
            st_ref[hh] = st_new[hh]

    @pl.when(live & (t_blk == n_tblk - 1))
    def _():
        for hh in range(HG_HEADS):
            ns_ref[0, hh] = st_new[hh].T


def _const_spec(shape):
    nd = len(shape)
    return pl.BlockSpec(shape, lambda *_: (0,) * nd, pipeline_mode=pl.Buffered(1))


def _layer_spec(layer, shape):
    nd = len(shape)
    return pl.BlockSpec((None,) + shape, lambda *_: (layer,) + (0,) * nd, pipeline_mode=pl.Buffered(1))


def _prompt_layer(layer, x, rope, tri, sinks, lb_all, anorm, w_in, onorm, w_o, mnorm, w_up, w_dn, fnorm):
    batch, seq, _ = x.shape
    tb = PROMPT_BLOCK
    n_tblk = seq // tb
    n_blocks = batch * n_tblk
    last = layer == DEPTH - 1
    mixer_blk = lambda j: jnp.minimum(j, n_blocks - 1)
    tail_blk = lambda j: jnp.maximum(j - 1, 0)
    rope_spec = pl.BlockSpec((tb, LANES), lambda j: (mixer_blk(j) % n_tblk, 0))
    carry_spec = pl.BlockSpec((1, WINDOW, KV_WIDTH), lambda j: (mixer_blk(j) // n_tblk, 0, 0))
    return pl.pallas_call(
        functools.partial(_prompt_layer_kernel, layer, last, n_tblk),
        grid=(n_blocks + 1,),
        in_specs=[
            pl.BlockSpec((1, tb, D_MODEL), lambda j: (mixer_blk(j) // n_tblk, mixer_blk(j) % n_tblk, 0)),
            rope_spec, rope_spec, rope_spec,
            _const_spec((tb, tb)),
            pl.BlockSpec(memory_space=pltpu.SMEM),
            _const_spec((DEPTH, HG_WIDTH)),
            _layer_spec(layer, (1, D_MODEL)),
            _layer_spec(layer, (D_MODEL, IN_WIDTH)),
            _layer_spec(layer, (1, HG_DV)),
            _layer_spec(layer, (MIX_WIDTH, D_MODEL)),
            _layer_spec(layer, (1, D_MODEL)),
            _layer_spec(layer, (D_MODEL, D_FF)),
            _layer_spec(layer, (D_FF, D_MODEL)),
            _const_spec((1, D_MODEL)),
        ],
        out_specs=[
            pl.BlockSpec((1, tb, D_MODEL), lambda j: (tail_blk(j) // n_tblk, tail_blk(j) % n_tblk, 0)),
            carry_spec, carry_spec,
            pl.BlockSpec((1, HG_HEADS, HG_DK, HG_DV), lambda j: (mixer_blk(j) // n_tblk, 0, 0, 0)),
        ],
        out_shape=[
            jax.ShapeDtypeStruct((batch, seq, D_MODEL), F32),
            jax.ShapeDtypeStruct((batch, WINDOW, KV_WIDTH), F32),
            jax.ShapeDtypeStruct((batch, WINDOW, KV_WIDTH), F32),
            jax.ShapeDtypeStruct((batch, HG_HEADS, HG_DK, HG_DV), F32),
        ],
        scratch_shapes=[
            pltpu.VMEM((tb, IN_WIDTH), F32),
            pltpu.VMEM((tb, MIX_WIDTH), BF16),
            pltpu.VMEM((tb, D_MODEL), F32),
            pltpu.VMEM((WINDOW, KV_WIDTH), F32),
            pltpu.VMEM((WINDOW, KV_WIDTH), F32),
            pltpu.VMEM((HG_HEADS, HG_DV, HG_DK), F32),
            pltpu.VMEM((4, tb, HG_DK), F32),
            pltpu.VMEM((tb, HG_DV), F32),
        ],
        compiler_params=pltpu.CompilerParams(
            dimension_semantics=("arbitrary",), vmem_limit_bytes=VMEM_LIMIT),
        name=f"prompt_layer{layer}",
    )(x, *rope, tri, sinks, lb_all, anorm, w_in, onorm, w_o, mnorm, w_up, w_dn, fnorm)


def _sample_in_kernel(x_ref, anorm_ref, win_ref, proj_ref):
    h = _rms(x_ref[...], anorm_ref[...]).astype(BF16)
    proj_ref[...] = _dot(h, win_ref[...])


def _sample_out_kernel(last, x_ref, mix_ref, wo_ref, mnorm_ref, wup_ref, wdn_ref, fnorm_ref, y_ref):
    y_ref[...] = _dense_out(x_ref[...], mix_ref[...].astype(BF16), wo_ref, mnorm_ref, wup_ref,
                            wdn_ref, fnorm_ref, last)


def _sample_mixer_kernel(layer, n_new, n_aliased, *refs):
    (p_ref, cos_ref, sdn_ref, sup_ref, tri_ref, rev_ref, sel_ref, ck_ref, cv_ref, s0_ref, sinks_ref, lb_ref,
     onorm_ref) = refs[:SAMPLE_MIXER_INPUTS]
    mix_ref, nk_ref, nv_ref, ns_ref, pad_ref = refs[SAMPLE_MIXER_INPUTS + n_aliased:]
    slot = 0 if n_aliased else layer
    for out_ref in (nk_ref, nv_ref, ns_ref):
        for other in range(out_ref.shape[0]):
            if other != slot:
                out_ref[other] = jnp.zeros(out_ref.shape[1:], F32)
    sb = ck_ref.shape[0]
    ts = p_ref.shape[0]
    tok_shift = n_new.bit_length() - 1
    key_shift = WINDOW.bit_length() - 1
    n_cache = sb * WINDOW
    cos, sdn, sup = cos_ref[...], sdn_ref[...], sup_ref[...]

    k_new = _rope(p_ref[:, OFF_K:OFF_K + KV_WIDTH], cos, sdn, sup)
    v_new = p_ref[:, OFF_V:OFF_V + KV_WIDTH]
    pad_keys = jnp.zeros((LANES - ts, KV_WIDTH), F32)
    k_all = jnp.concatenate([ck_ref[...].reshape(n_cache, KV_WIDTH), k_new, pad_keys], axis=0).astype(BF16)
    v_all = jnp.concatenate([cv_ref[...].reshape(n_cache, KV_WIDTH), v_new, pad_keys], axis=0).astype(BF16)
    n_keys = n_cache + LANES
    row = lax.broadcasted_iota(jnp.int32, (ts, n_keys), 0)
    col = lax.broadcasted_iota(jnp.int32, (ts, n_keys), 1)
    new_idx = col - n_cache
    cached_ok = ((col >> key_shift) == (row >> tok_shift)) & ((col & (WINDOW - 1)) >= (row & (n_new - 1)))
    new_ok = ((new_idx < ts) & ((new_idx >> tok_shift) == (row >> tok_shift))
              & ((new_idx & (n_new - 1)) <= (row & (n_new - 1))))
    allowed = ((col < n_cache) & cached_ok) | ((col >= n_cache) & new_ok)

    q_ext = _stacked_queries(p_ref, cos, sdn, sup)
    s_all = _dot_nt(jnp.concatenate(q_ext, axis=0), k_all)
    soft = [_softmax_sink(s_all[head * ts:(head + 1) * ts], allowed, sinks_ref[head])
            for head in range(ATT_HEADS)]
    o_all = _dot(jnp.concatenate([p for p, _ in soft], axis=0), v_all)
    att = [o_all[head * ts:(head + 1) * ts] * soft[head][1] for head in range(ATT_HEADS)]
    for hp in range(ATT_HEADS // 2):
        mix_ref[:, hp * LANES:(hp + 1) * LANES] = _merge_head_pair(
            att[2 * hp], att[2 * hp + 1], (2 * hp) // GROUP)
    for s in range(sb):
        nk_ref[slot, s, 0:WINDOW - n_new] = ck_ref[s, n_new:WINDOW]
        nv_ref[slot, s, 0:WINDOW - n_new] = cv_ref[s, n_new:WINDOW]
        nk_ref[slot, s, WINDOW - n_new:WINDOW] = k_new[s * n_new:(s + 1) * n_new]
        nv_ref[slot, s, WINDOW - n_new:WINDOW] = v_new[s * n_new:(s + 1) * n_new]

    lb = _lower_bound(lb_ref[...], layer)
    log_f, k_in = _hgrn_gates(p_ref[:, OFF_HF:OFF_HF + HG_WIDTH], lb)
    b = _cumsum_rows(tri_ref[...], log_f)
    later = _cumsum_rows(rev_ref[...], log_f)
    q_in = _silu(p_ref[:, OFF_HQ:OFF_HQ + HG_WIDTH])
    v_in = p_ref[:, OFF_HI:OFF_HI + HG_WIDTH]
    gate = _silu(p_ref[:, OFF_HG:OFF_HG + HG_WIDTH])
    onorm = onorm_ref[...]
    heads = [slice(hh * HG_DK, (hh + 1) * HG_DK) for hh in range(HG_HEADS)]

    pad_ref[...] = jnp.zeros_like(pad_ref)
    for i, x in enumerate((k_in, b, v_in)):
        pad_ref[i, SUBLANES:SUBLANES + ts] = x
    tok = lax.broadcasted_iota(jnp.int32, (ts, 1), 0) & (n_new - 1)
    o_intra = [jnp.zeros((ts, HG_DV), F32) for _ in heads]
    for d in range(n_new):
        k_d, b_d, v_d = (pad_ref[i, SUBLANES - d:SUBLANES - d + ts] for i in range(3))
        w = q_in * k_d * jnp.exp(jnp.minimum(b - b_d, 0.0))
        for hh, sl in enumerate(heads):
            a_col = jnp.sum(w[:, sl], axis=-1, keepdims=True)
            o_intra[hh] = o_intra[hh] + jnp.where(tok >= d, a_col, 0.0) * v_d[:, sl]

    q_dec = (q_in * jnp.exp(b)).astype(BF16)
    seq_of_row = lax.broadcasted_iota(jnp.int32, (ts, 1), 0) >> tok_shift
    for hh, sl in enumerate(heads):
        o = o_intra[hh]
        for s in range(sb):
            o_s = _dot(q_dec[:, sl], s0_ref[s, hh].astype(BF16))
            o = o + jnp.where(seq_of_row == s, o_s, 0.0)
        mix_ref[:, ATT_WIDTH + hh * HG_DV:ATT_WIDTH + (hh + 1) * HG_DV] = _rms(o, onorm) * gate[:, sl]

    k_hat = k_in * jnp.exp(later)
    e_b = jnp.exp(b)
    e1 = e_b.astype(BF16).astype(F32)
    e2 = (e_b - e1).astype(BF16).astype(F32)
    e3 = e_b - e1 - e2
    own = ((lax.broadcasted_iota(jnp.int32, (ts, sb * HG_DV), 1) >> key_shift)
           == (lax.broadcasted_iota(jnp.int32, (ts, sb * HG_DV), 0) >> tok_shift))
    no_v = jnp.zeros((3 * ts, sb * HG_DV), F32)
    for hh, sl in enumerate(heads):
        lhs = jnp.concatenate([k_hat[:, sl], e1[:, sl], e2[:, sl], e3[:, sl]], axis=0).T.astype(BF16)
        v_rep = jnp.concatenate([v_in[:, sl]] * sb, axis=1)
        v_diag = jnp.concatenate([jnp.where(own, v_rep, 0.0), no_v], axis=0).astype(BF16)
        upd = _dot(lhs, v_diag)
        decay = _dot(lhs, sel_ref[...])
        for s in range(sb):
            cols = slice(s * HG_DV, (s + 1) * HG_DV)
            ns_ref[slot, s, hh] = decay[:, cols] * s0_ref[s, hh] + upd[:, cols]


def _sample_layer(layer, x, rope, cache_k, cache_v, state, new_caches, sinks, lb_all, anorm, w_in, onorm, w_o,
                  mnorm, w_up, w_dn, fnorm):
    n_seq, n_new, _ = x.shape
    n_tok = n_seq * n_new
    ts = LANES // 4
    sb = ts // n_new
    assert sb * n_new == ts and n_seq % sb == 0 and n_new <= SUBLANES
    assert n_new & (n_new - 1) == 0 and WINDOW & (WINDOW - 1) == 0 and HG_DV == WINDOW
    last = layer == DEPTH - 1
    x2 = x.reshape(n_tok, D_MODEL)
    params = pltpu.CompilerParams(vmem_limit_bytes=VMEM_LIMIT)
    whole = _const_spec
    slab = functools.partial(_layer_spec, layer)

    proj = pl.pallas_call(
        _sample_in_kernel,
        grid=(1,),
        in_specs=[whole((n_tok, D_MODEL)), slab((1, D_MODEL)), slab((D_MODEL, IN_WIDTH))],
        out_specs=pl.BlockSpec((n_tok, IN_WIDTH), lambda i: (0, 0)),
        out_shape=jax.ShapeDtypeStruct((n_tok, IN_WIDTH), F32),
        compiler_params=params,
        name=f"sample_in{layer}",
    )(x2, anorm, w_in)

    r = jnp.arange(ts)
    same_seq = (r[:, None] // n_new) == (r[None, :] // n_new)
    tri = (same_seq & (r[None, :] <= r[:, None])).astype(BF16)
    rev = (same_seq & (r[None, :] > r[:, None])).astype(BF16)
    c = jnp.arange(sb * HG_DV)
    is_last = ((r[:, None] % n_new) == n_new - 1) & ((r[:, None] // n_new) == (c[None, :] // HG_DV))
    sel = jnp.concatenate([jnp.zeros_like(is_last)] + [is_last] * 3, axis=0).astype(BF16)

    tok_spec = lambda width: pl.BlockSpec((ts, width), lambda i: (i, 0))
    seq_spec = lambda *tail: pl.BlockSpec((sb,) + tail, lambda i: (i,) + (0,) * len(tail))
    layer_spec = lambda *tail: pl.BlockSpec((None, sb) + tail, lambda i: (layer, i) + (0,) * len(tail))
    rope_spec = _const_spec((ts, LANES))
    first = not new_caches
    slab_spec = lambda *tail: pl.BlockSpec(
        (DEPTH if first else 1, sb) + tail, lambda i: (0 if first else layer, i) + (0,) * len(tail))
    n_in = SAMPLE_MIXER_INPUTS
    aliases = {n_in + i: 1 + i for i in range(len(new_caches))}
    stacked = [(DEPTH, n_seq, WINDOW, KV_WIDTH)] * 2 + [(DEPTH, n_seq, HG_HEADS, HG_DK, HG_DV)]
    mix, *new_caches = pl.pallas_call(
        functools.partial(_sample_mixer_kernel, layer, n_new, len(new_caches)),
        grid=(n_seq // sb,),
        in_specs=[
            tok_spec(IN_WIDTH),
            rope_spec, rope_spec, rope_spec,
            _const_spec((ts, ts)), _const_spec((ts, ts)), _const_spec((4 * ts, sb * HG_DV)),
            layer_spec(WINDOW, KV_WIDTH), layer_spec(WINDOW, KV_WIDTH),
            layer_spec(HG_HEADS, HG_DK, HG_DV),
            pl.BlockSpec(memory_space=pltpu.SMEM),
            _const_spec((DEPTH, HG_WIDTH)),
            _layer_spec(layer, (1, HG_DV)),
        ] + [pl.BlockSpec(memory_space=pl.ANY)] * len(new_caches),
        out_specs=[
            tok_spec(MIX_WIDTH),
            slab_spec(WINDOW, KV_WIDTH), slab_spec(WINDOW, KV_WIDTH),
            slab_spec(HG_HEADS, HG_DK, HG_DV),
        ],
        out_shape=[jax.ShapeDtypeStruct((n_tok, MIX_WIDTH), F32)]
        + [jax.ShapeDtypeStruct(shape, F32) for shape in stacked],
        input_output_aliases=aliases,
        scratch_shapes=[pltpu.VMEM((3, SUBLANES + ts, HG_WIDTH), F32)],
        compiler_params=pltpu.CompilerParams(
            dimension_semantics=("arbitrary",), vmem_limit_bytes=VMEM_LIMIT),
        name=f"sample_mixer{layer}",
    )(proj, *(jnp.tile(t, (sb, 1)) for t in rope), tri, rev, sel, cache_k, cache_v, state, sinks, lb_all,
      onorm, *new_caches)

    y = pl.pallas_call(
        functools.partial(_sample_out_kernel, last),
        grid=(1,),
        in_specs=[whole((n_tok, D_MODEL)), whole((n_tok, MIX_WIDTH)), slab((MIX_WIDTH, D_MODEL)),
                  slab((1, D_MODEL)), slab((D_MODEL, D_FF)), slab((D_FF, D_MODEL)), whole((1, D_MODEL))],
        out_specs=pl.BlockSpec((n_tok, D_MODEL), lambda i: (0, 0)),
        out_shape=jax.ShapeDtypeStruct((n_tok, D_MODEL), F32),
        compiler_params=params,
        name=f"sample_out{layer}",
    )(x2, mix, w_o, mnorm, w_up, w_dn, fnorm)
    return y.reshape(n_seq, n_new, D_MODEL), new_caches


def _rope_tables(pos):
    n = pos.shape[0]
    inv_freq = jnp.power(ROPE_THETA, -jnp.arange(ROT_HALF, dtype=F32) * (2.0 / ROT_DIM))
    ang = pos.astype(F32)[:, None] * inv_freq[None, :]
    cos, sin = jnp.cos(ang), jnp.sin(ang)
    rest = HEAD_DIM - ROT_DIM
    zeros_h = jnp.zeros((n, ROT_HALF), F32)
    cos_t = jnp.concatenate([cos, cos, jnp.ones((n, rest), F32)], axis=1)
    sdn_t = jnp.concatenate([zeros_h, sin, jnp.zeros((n, rest), F32)], axis=1)
    sup_t = jnp.concatenate([-sin, zeros_h, jnp.zeros((n, rest), F32)], axis=1)
    reps = LANES // HEAD_DIM
    return tuple(jnp.tile(t, (1, reps)) for t in (cos_t, sdn_t, sup_t))


def _chunk_tri(n, chunk):
    r = jnp.arange(n)
    same = (r[:, None] // chunk) == (r[None, :] // chunk)
    return (same & (r[None, :] <= r[:, None])).astype(BF16)


def kernel(x_prompt, x_sample, cache_k, cache_v, state_hgrn, attn_norm, w_in, att_sinks, hgrn_lower_bounds,
           hgrn_out_norm, w_o, mlp_norm, w_up, w_down, final_norm):
    batch, seq, _ = x_prompt.shape
    n_seq, n_new, _ = x_sample.shape
    assert seq % PROMPT_BLOCK == 0 and PROMPT_BLOCK % HG_CHUNK == 0

    rope_p = _rope_tables(jnp.arange(seq, dtype=jnp.int32))
    rope_s = _rope_tables(PAST_LEN + jnp.arange(n_new, dtype=jnp.int32))
    tri = _chunk_tri(PROMPT_BLOCK, HG_CHUNK)
    lb_all = hgrn_lower_bounds.astype(F32)
    fnorm = final_norm.reshape(1, D_MODEL)

    ck = cache_k.reshape(DEPTH, n_seq, WINDOW, KV_WIDTH)
    cv = cache_v.reshape(DEPTH, n_seq, WINDOW, KV_WIDTH)
    weights = (attn_norm.reshape(DEPTH, 1, D_MODEL), w_in.astype(BF16), hgrn_out_norm.reshape(DEPTH, 1, HG_DV),
               w_o.astype(BF16), mlp_norm.reshape(DEPTH, 1, D_MODEL), w_up.astype(BF16), w_down.astype(BF16),
               fnorm)
    xp, xs = x_prompt, x_sample
    nk_p, nv_p, ns_p, new_caches = [], [], [], []
    for l in range(DEPTH):
        xp, k1, v1, s1 = _prompt_layer(l, xp, rope_p, tri, att_sinks[l], lb_all, *weights)
        xs, new_caches = _sample_layer(l, xs, rope_s, ck, cv, state_hgrn, new_caches, att_sinks[l], lb_all,
                                       *weights)
        nk_p.append(k1), nv_p.append(v1), ns_p.append(s1)
    nk_s, nv_s, ns_s = new_caches

    kv_p = (DEPTH, batch, WINDOW, ATT_KV_HEADS, HEAD_DIM)
    kv_s = (DEPTH, n_seq, WINDOW, ATT_KV_HEADS, HEAD_DIM)
    return (xp, xs,
            jnp.stack(nk_p).reshape(kv_p), jnp.stack(nv_p).reshape(kv_p), jnp.stack(ns_p),
            nk_s.reshape(kv_s), nv_s.reshape(kv_s), ns_s)
```

```python
import functools

import jax
import jax.numpy as jnp
from jax import lax
from jax.experimental import pallas as pl
from jax.experimental.pallas import tpu as pltpu

F32 = jnp.float32
BF16 = jnp.bfloat16

D_MODEL = 1024
DEPTH = 2
PAST_LEN = 16384
ATT_HEADS = 8
ATT_KV_HEADS = 2
HEAD_DIM = 64
GROUP = ATT_HEADS // ATT_KV_HEADS
ATT_WIDTH = ATT_HEADS * HEAD_DIM
KV_WIDTH = ATT_KV_HEADS * HEAD_DIM
WINDOW = 128
ROT_DIM = HEAD_DIM // 4
ROT_HALF = ROT_DIM // 2
ROPE_THETA = 500000.0
HG_HEADS = 4
HG_DK = 128
HG_DV = 128
HG_WIDTH = HG_HEADS * HG_DK
MIX_WIDTH = ATT_WIDTH + HG_WIDTH
IN_WIDTH = ATT_WIDTH + 2 * KV_WIDTH + 4 * HG_WIDTH
D_FF = 4 * D_MODEL
EPS = 1e-6

OFF_Q = 0
OFF_K = ATT_WIDTH
OFF_V = OFF_K + KV_WIDTH
OFF_HQ = OFF_V + KV_WIDTH
OFF_HF = OFF_HQ + HG_WIDTH
OFF_HI = OFF_HF + HG_WIDTH
OFF_HG = OFF_HI + HG_WIDTH

LANES = 128
PROMPT_BLOCK = 256
HG_CHUNK = 128
HG_SUB = 32
HG_GUARD = 80.0
FF_BLOCK = 2048
SUBLANES = 8
SAMPLE_IN_BLOCK = 1408
SAMPLE_FF_BLOCK = 1024
SAMPLE_MIXER_INPUTS = 13
VMEM_LIMIT = 58 * 1024 * 1024


def _dot(a, b):
    return jnp.dot(a, b, preferred_element_type=F32)


def _dot_nt(a, b):
    return lax.dot_general(a, b, (((1,), (1,)), ((), ())), preferred_element_type=F32)


def _rms(x, g_row):
    ms = jnp.mean(x * x, axis=-1, keepdims=True)
    return (x * lax.rsqrt(ms + EPS)) * g_row


def _rope(x, cos, sin_dn, sin_up):
    return x * cos + pltpu.roll(x, ROT_HALF, 1) * sin_dn + pltpu.roll(x, LANES - ROT_HALF, 1) * sin_up


def _lower_bound(lb_all, layer):
    m = jnp.max(lb_all, axis=0, keepdims=True)
    e = jnp.exp(lb_all - m)
    p = e / jnp.sum(e, axis=0, keepdims=True)
    cs = p[0:1]
    for l in range(1, layer + 1):
        cs = cs + p[l:l + 1]
    return jnp.maximum(cs - p[0:1], 0.0)


def _hgrn_gates(z, lb):
    e = jnp.exp(-jnp.abs(z))
    log_sig = jnp.minimum(z, 0.0) - jnp.log1p(e)
    a1 = jnp.log(lb)
    a2 = jnp.log1p(-lb) + log_sig
    log_f = jnp.maximum(a1, a2) + jnp.log1p(jnp.exp(-jnp.abs(a1 - a2)))
    k_in = (1.0 - lb) * (jnp.where(z >= 0.0, e, 1.0) / (1.0 + e))
    return log_f, k_in


def _silu(x):
    return x / (1.0 + jnp.exp(-x))


def _stacked_queries(proj_ref, cos, sdn, sup):
    t = proj_ref.shape[0]
    lo_half = lax.broadcasted_iota(jnp.int32, (t, LANES), 1) < HEAD_DIM
    q_ext = []
    for hp in range(ATT_HEADS // 2):
        q_slab = _rope(proj_ref[:, OFF_Q + hp * LANES:OFF_Q + (hp + 1) * LANES], cos, sdn, sup)
        q_slab = q_slab * (HEAD_DIM ** -0.5)
        q_swap = pltpu.roll(q_slab, HEAD_DIM, 1)
        for sub in range(2):
            kvh = (2 * hp + sub) // GROUP
            src = q_slab if sub == kvh else q_swap
            keep = lo_half if kvh == 0 else ~lo_half
            q_ext.append(jnp.where(keep, src, 0.0).astype(BF16))
    return q_ext


def _softmax_sink(s, allowed, sink):
    s = jnp.where(allowed, s, -jnp.inf)
    m = jnp.maximum(jnp.max(s, axis=-1, keepdims=True), sink)
    p = jnp.exp(s - m)
    inv = 1.0 / (jnp.sum(p, axis=-1, keepdims=True) + jnp.exp(sink - m))
    return p.astype(BF16), inv


def _merge_head_pair(a, b, kvh):
    lo_half = lax.broadcasted_iota(jnp.int32, a.shape, 1) < HEAD_DIM
    if kvh == 0:
        return jnp.where(lo_half, a, pltpu.roll(b, HEAD_DIM, 1))
    return jnp.where(lo_half, pltpu.roll(a, HEAD_DIM, 1), b)


def _cumsum_rows(tri, g):
    g1 = g.astype(BF16)
    r1 = g - g1.astype(F32)
    g2 = r1.astype(BF16)
    g3 = (r1 - g2.astype(F32)).astype(BF16)
    return _dot(tri, g1) + _dot(tri, g2) + _dot(tri, g3)


def _hgrn_intra_operands(q, k, b):
    c = q.shape[0]
    q_parts, k_parts = [], []
    for i in range(c // HG_SUB):
        lo, hi = i * HG_SUB, (i + 1) * HG_SUB
        r = b[lo - 1:lo] if i > 0 else jnp.zeros((1, HG_DK), F32)
        q_parts.append(q[lo:hi] * jnp.exp(b[lo:hi] - r))
        k_parts.append(k[:hi] * jnp.exp(r - b[:hi]))
        if hi < c:
            k_parts.append(jnp.zeros((c - hi, HG_DK), F32))
    return (jnp.concatenate(q_parts, axis=0).astype(BF16),
            jnp.concatenate(k_parts, axis=0).astype(BF16))


def _hgrn_worst_subblock_decay(b_all):
    worst = None
    for lo in range(0, b_all.shape[0], HG_SUB):
        tot = b_all[lo + HG_SUB - 1:lo + HG_SUB]
        if lo % HG_CHUNK:
            tot = tot - b_all[lo - 1:lo]
        worst = tot if worst is None else jnp.minimum(worst, tot)
    return worst


def _hgrn_intra_exact(q, k, v, b, work_ref, out_ref):
    for i, x in enumerate((q, k, v, b)):
        work_ref[i] = x
    s_idx = lax.broadcasted_iota(jnp.int32, (HG_CHUNK, 1), 0)

    def row(t, carry):
        c0 = pl.multiple_of((t // HG_CHUNK) * HG_CHUNK, HG_CHUNK)
        q_t = work_ref[0, pl.ds(t, 1), :]
        b_t = work_ref[3, pl.ds(t, 1), :]
        k_c = work_ref[1, pl.ds(c0, HG_CHUNK), :]
        v_c = work_ref[2, pl.ds(c0, HG_CHUNK), :]
        b_c = work_ref[3, pl.ds(c0, HG_CHUNK), :]
        w = k_c * jnp.exp(jnp.minimum(b_t - b_c, 0.0)) * q_t
        a = jnp.where(s_idx <= t - c0, jnp.sum(w, axis=-1, keepdims=True), 0.0)
        out_ref[pl.ds(t, 1), :] = jnp.sum(a * v_c, axis=0, keepdims=True)
        return carry

    lax.fori_loop(0, q.shape[0], row, 0)
    return out_ref[...]


def _hgrn_intra_scores(a_full):
    c = a_full.shape[0]
    a = jnp.concatenate(
        [a_full[i * HG_SUB:(i + 1) * HG_SUB, i * c:(i + 1) * c] for i in range(c // HG_SUB)], axis=0)
    row = lax.broadcasted_iota(jnp.int32, (c, c), 0)
    col = lax.broadcasted_iota(jnp.int32, (c, c), 1)
    return jnp.where(col <= row, a, 0.0).astype(BF16)


def _out_proj(x, mix_bf16, wo_ref, mnorm_ref):
    x1 = x + _dot(mix_bf16, wo_ref[...])
    return x1, _rms(x1, mnorm_ref[...]).astype(BF16)


def _mlp_cols(acc, h2, w_up_cols, w_dn_rows):
    u = jnp.square(jnp.maximum(_dot(h2, w_up_cols), 0.0)).astype(BF16)
    return acc + _dot(u, w_dn_rows)


def _mlp_block(acc, h2, c, wup_ref, wdn_ref):
    return _mlp_cols(acc, h2, wup_ref[:, c * FF_BLOCK:(c + 1) * FF_BLOCK],
                     wdn_ref[c * FF_BLOCK:(c + 1) * FF_BLOCK, :])


def _prompt_layer_kernel(layer, last, n_tblk,
                         x_ref, cos_ref, sdn_ref, sup_ref, tri_ref, sinks_ref, lb_ref, anorm_ref,
                         win_ref, onorm_ref, wo_ref, mnorm_ref, wup_ref, wdn_ref, fnorm_ref,
                         y_ref, nk_ref, nv_ref, ns_ref,
                         proj_ref, mix_ref, xprev_ref, kprev_ref, vprev_ref, st_ref, work_ref, exact_ref):
    j = pl.program_id(0)
    n_steps = pl.num_programs(0)
    live = j < n_steps - 1
    t_blk = lax.rem(jnp.minimum(j, n_steps - 2), n_tblk)
    tb = PROMPT_BLOCK

    @pl.when(j == 0)
    def _():
        xprev_ref[...] = jnp.zeros_like(xprev_ref)
        mix_ref[...] = jnp.zeros_like(mix_ref)

    @pl.when(t_blk == 0)
    def _():
        kprev_ref[...] = jnp.zeros_like(kprev_ref)
        vprev_ref[...] = jnp.zeros_like(vprev_ref)
        st_ref[...] = jnp.zeros_like(st_ref)

    acc, h2 = _out_proj(xprev_ref[...], mix_ref[...], wo_ref, mnorm_ref)

    x = x_ref[0]
    xprev_ref[...] = x
    h = _rms(x, anorm_ref[...]).astype(BF16)
    proj_ref[...] = _dot(h, win_ref[...])

    cos, sdn, sup = cos_ref[...], sdn_ref[...], sup_ref[...]

    k_rot = _rope(proj_ref[:, OFF_K:OFF_K + KV_WIDTH], cos, sdn, sup)
    v_new = proj_ref[:, OFF_V:OFF_V + KV_WIDTH]
    k_all = jnp.concatenate([kprev_ref[...], k_rot], axis=0).astype(BF16)
    v_all = jnp.concatenate([vprev_ref[...], v_new], axis=0).astype(BF16)

    q_ext = _stacked_queries(proj_ref, cos, sdn, sup)
    row = lax.broadcasted_iota(jnp.int32, (WINDOW, 2 * WINDOW), 0)
    col = lax.broadcasted_iota(jnp.int32, (WINDOW, 2 * WINDOW), 1)
    in_window = (col >= row) & (col <= row + WINDOW)
    first_lo = jnp.where(t_blk == 0, WINDOW, 0)
    att = [[None] * (tb // WINDOW) for _ in range(ATT_HEADS)]
    for n in range(tb // WINDOW):
        rows_n = slice(n * WINDOW, (n + 1) * WINDOW)
        keys_n = slice(n * WINDOW, (n + 2) * WINDOW)
        q_stack = jnp.concatenate([q[rows_n] for q in q_ext], axis=0)
        s_all = _dot_nt(q_stack, k_all[keys_n])
        allowed = in_window & (col >= first_lo) if n == 0 else in_window
        soft = [_softmax_sink(s_all[head * WINDOW:(head + 1) * WINDOW], allowed, sinks_ref[head])
                for head in range(ATT_HEADS)]
        o_all = _dot(jnp.concatenate([p for p, _ in soft], axis=0), v_all[keys_n])
        for head in range(ATT_HEADS):
            att[head][n] = o_all[head * WINDOW:(head + 1) * WINDOW] * soft[head][1]
    for hp in range(ATT_HEADS // 2):
        a, b = (jnp.concatenate(att[2 * hp + sub], axis=0) for sub in range(2))
        mix_ref[:, hp * LANES:(hp + 1) * LANES] = _merge_head_pair(a, b, (2 * hp) // GROUP).astype(BF16)
    acc = _mlp_block(acc, h2, 0, wup_ref, wdn_ref)

    lb = _lower_bound(lb_ref[...], layer)
    log_f, k_in = _hgrn_gates(proj_ref[:, OFF_HF:OFF_HF + HG_WIDTH], lb)
    b_all = _cumsum_rows(tri_ref[...], log_f)
    onorm = onorm_ref[...]
    n_chunks = tb // HG_CHUNK
    units = [(hh, c) for c in range(n_chunks) for hh in range(HG_HEADS)]
    q_in, v_in = [], []
    for hh in range(HG_HEADS):
        q_in.append(_silu(proj_ref[:, OFF_HQ + hh * HG_DK:OFF_HQ + (hh + 1) * HG_DK]))
        v_in.append(proj_ref[:, OFF_HI + hh * HG_DV:OFF_HI + (hh + 1) * HG_DV])

    def piece(x, hh, c, lanes=False):
        x = x[:, hh * HG_DK:(hh + 1) * HG_DK] if lanes else x[hh]
        return x[c * HG_CHUNK:(c + 1) * HG_CHUNK]

    intra_ops = {u: _hgrn_intra_operands(piece(q_in, *u), piece(k_in, *u, lanes=True),
                                         piece(b_all, *u, lanes=True)) for u in units}
    a_full = {u: _dot_nt(*intra_ops[u]) for u in units}
    o_inter, st_new = {}, []
    for hh in range(HG_HEADS):
        st = st_ref[hh]
        for c in range(n_chunks):
            q, k, v, b = (piece(q_in, hh, c), piece(k_in, hh, c, True), piece(v_in, hh, c),
                          piece(b_all, hh, c, True))
            o_inter[(hh, c)] = _dot_nt((q * jnp.exp(b)).astype(BF16), st.astype(BF16))
            b_last = b[HG_CHUNK - 1:HG_CHUNK]
            k_hat = (k * jnp.exp(b_last - b)).astype(BF16)
            st = st * jnp.exp(b_last) + _dot(v.T.astype(BF16), k_hat)
        st_new.append(st)
    a_mask = {u: _hgrn_intra_scores(a_full[u]) for u in units}
    o_intra = {u: _dot(a_mask[u], piece(v_in, *u).astype(BF16)) for u in units}

    def emit_head(hh, o_intra_h):
        o = o_intra_h + jnp.concatenate([o_inter[(hh, c)] for c in range(n_chunks)], axis=0)
        gate = _silu(proj_ref[:, OFF_HG + hh * HG_DV:OFF_HG + (hh + 1) * HG_DV])
        mix_ref[:, ATT_WIDTH + hh * HG_DV:ATT_WIDTH + (hh + 1) * HG_DV] = (
            _rms(o, onorm) * gate).astype(BF16)

    for hh in range(HG_HEADS):
        emit_head(hh, jnp.concatenate([o_intra[(hh, c)] for c in range(n_chunks)], axis=0))
    acc = _mlp_block(acc, h2, 1, wup_ref, wdn_ref)
    y_ref[0] = _rms(acc, fnorm_ref[...]) if last else acc

    @pl.when(jnp.min(_hgrn_worst_subblock_decay(b_all)) < -HG_GUARD)
    def _():
        for hh in range(HG_HEADS):
            sl = slice(hh * HG_DK, (hh + 1) * HG_DK)
            emit_head(hh, _hgrn_intra_exact(q_in[hh], k_in[:, sl], v_in[hh], b_all[:, sl],
                                            work_ref, exact_ref))

    @pl.when(live)
    def _():
        kprev_ref[...] = k_rot[tb - WINDOW:]
        vprev_ref[...] = v_new[tb - WINDOW:]
        nk_ref[0] = k_rot[tb - WINDOW:]
        nv_ref[0] = v_new[tb - WINDOW:]
        for hh in range(HG_HEADS):
            st_ref[hh] = st_new[hh]

    @pl.when(live & (t_blk == n_tblk - 1))
    def _():
        for hh in range(HG_HEADS):
            ns_ref[0, hh] = st_new[hh].T


def _const_spec(shape):
    nd = len(shape)
    return pl.BlockSpec(shape, lambda *_: (0,) * nd, pipeline_mode=pl.Buffered(1))


def _layer_spec(layer, shape):
    nd = len(shape)
    return pl.BlockSpec((None,) + shape, lambda *_: (layer,) + (0,) * nd, pipeline_mode=pl.Buffered(1))


def _prompt_layer(layer, x, rope, tri, sinks, lb_all, anorm, w_in, onorm, w_o, mnorm, w_up, w_dn, fnorm):
    batch, seq, _ = x.shape
    tb = PROMPT_BLOCK
    n_tblk = seq // tb
    n_blocks = batch * n_tblk
    last = layer == DEPTH - 1
    mixer_blk = lambda j: jnp.minimum(j, n_blocks - 1)
    tail_blk = lambda j: jnp.maximum(j - 1, 0)
    rope_spec = pl.BlockSpec((tb, LANES), lambda j: (mixer_blk(j) % n_tblk, 0))
    carry_spec = pl.BlockSpec((1, WINDOW, KV_WIDTH), lambda j: (mixer_blk(j) // n_tblk, 0, 0))
    return pl.pallas_call(
        functools.partial(_prompt_layer_kernel, layer, last, n_tblk),
        grid=(n_blocks + 1,),
        in_specs=[
            pl.BlockSpec((1, tb, D_MODEL), lambda j: (mixer_blk(j) // n_tblk, mixer_blk(j) % n_tblk, 0)),
            rope_spec, rope_spec, rope_spec,
            _const_spec((tb, tb)),
            pl.BlockSpec(memory_space=pltpu.SMEM),
            _const_spec((DEPTH, HG_WIDTH)),
            _layer_spec(layer, (1, D_MODEL)),
            _layer_spec(layer, (D_MODEL, IN_WIDTH)),
            _layer_spec(layer, (1, HG_DV)),
            _layer_spec(layer, (MIX_WIDTH, D_MODEL)),
            _layer_spec(layer, (1, D_MODEL)),
            _layer_spec(layer, (D_MODEL, D_FF)),
            _layer_spec(layer, (D_FF, D_MODEL)),
            _const_spec((1, D_MODEL)),
        ],
        out_specs=[
            pl.BlockSpec((1, tb, D_MODEL), lambda j: (tail_blk(j) // n_tblk, tail_blk(j) % n_tblk, 0)),
            carry_spec, carry_spec,
            pl.BlockSpec((1, HG_HEADS, HG_DK, HG_DV), lambda j: (mixer_blk(j) // n_tblk, 0, 0, 0)),
        ],
        out_shape=[
            jax.ShapeDtypeStruct((batch, seq, D_MODEL), F32),
            jax.ShapeDtypeStruct((batch, WINDOW, KV_WIDTH), F32),
            jax.ShapeDtypeStruct((batch, WINDOW, KV_WIDTH), F32),
            jax.ShapeDtypeStruct((batch, HG_HEADS, HG_DK, HG_DV), F32),
        ],
        scratch_shapes=[
            pltpu.VMEM((tb, IN_WIDTH), F32),
            pltpu.VMEM((tb, MIX_WIDTH), BF16),
            pltpu.VMEM((tb, D_MODEL), F32),
            pltpu.VMEM((WINDOW, KV_WIDTH), F32),
            pltpu.VMEM((WINDOW, KV_WIDTH), F32),
            pltpu.VMEM((HG_HEADS, HG_DV, HG_DK), F32),
            pltpu.VMEM((4, tb, HG_DK), F32),
            pltpu.VMEM((tb, HG_DV), F32),
        ],
        compiler_params=pltpu.CompilerParams(
            dimension_semantics=("arbitrary",), vmem_limit_bytes=VMEM_LIMIT),
        name=f"prompt_layer{layer}",
    )(x, *rope, tri, sinks, lb_all, anorm, w_in, onorm, w_o, mnorm, w_up, w_dn, fnorm)


def _sample_in_kernel(x_ref, anorm_ref, win_ref, proj_ref, h_ref):
    @pl.when(pl.program_id(0) == 0)
    def _():
        h_ref[...] = _rms(x_ref[...], anorm_ref[...]).astype(BF16)

    proj_ref[...] = _dot(h_ref[...], win_ref[...])


def _sample_out_kernel(last, x_ref, mix_ref, wo_ref, mnorm_ref, wup_ref, wdn_ref, fnorm_ref, y_ref,
                       h2_ref, acc_ref):
    c = pl.program_id(0)

    @pl.when(c == 0)
    def _():
        acc_ref[...], h2_ref[...] = _out_proj(x_ref[...], mix_ref[...].astype(BF16), wo_ref, mnorm_ref)

    acc_ref[...] = _mlp_cols(acc_ref[...], h2_ref[...], wup_ref[...], wdn_ref[...])

    @pl.when(c == pl.num_programs(0) - 1)
    def _():
        y_ref[...] = _rms(acc_ref[...], fnorm_ref[...]) if last else acc_ref[...]


def _sample_mixer_kernel(layer, n_new, n_aliased, *refs):
    (p_ref, cos_ref, sdn_ref, sup_ref, tri_ref, rev_ref, sel_ref, ck_ref, cv_ref, s0_ref, sinks_ref, lb_ref,
     onorm_ref) = refs[:SAMPLE_MIXER_INPUTS]
    mix_ref, nk_ref, nv_ref, ns_ref, pad_ref = refs[SAMPLE_MIXER_INPUTS + n_aliased:]
    slot = 0 if n_aliased else layer
    for out_ref in (nk_ref, nv_ref, ns_ref):
        for other in range(out_ref.shape[0]):
            if other != slot:
                out_ref[other] = jnp.zeros(out_ref.shape[1:], F32)
    sb = ck_ref.shape[0]
    ts = p_ref.shape[0]
    tok_shift = n_new.bit_length() - 1
    key_shift = WINDOW.bit_length() - 1
    n_cache = sb * WINDOW
    cos, sdn, sup = cos_ref[...], sdn_ref[...], sup_ref[...]

    k_new = _rope(p_ref[:, OFF_K:OFF_K + KV_WIDTH], cos, sdn, sup)
    v_new = p_ref[:, OFF_V:OFF_V + KV_WIDTH]
    pad_keys = jnp.zeros((LANES - ts, KV_WIDTH), F32)
    k_all = jnp.concatenate([ck_ref[...].reshape(n_cache, KV_WIDTH), k_new, pad_keys], axis=0).astype(BF16)
    v_all = jnp.concatenate([cv_ref[...].reshape(n_cache, KV_WIDTH), v_new, pad_keys], axis=0).astype(BF16)
    n_keys = n_cache + LANES
    row = lax.broadcasted_iota(jnp.int32, (ts, n_keys), 0)
    col = lax.broadcasted_iota(jnp.int32, (ts, n_keys), 1)
    new_idx = col - n_cache
    cached_ok = ((col >> key_shift) == (row >> tok_shift)) & ((col & (WINDOW - 1)) >= (row & (n_new - 1)))
    new_ok = ((new_idx < ts) & ((new_idx >> tok_shift) == (row >> tok_shift))
              & ((new_idx & (n_new - 1)) <= (row & (n_new - 1))))
    allowed = ((col < n_cache) & cached_ok) | ((col >= n_cache) & new_ok)

    q_ext = _stacked_queries(p_ref, cos, sdn, sup)
    s_all = _dot_nt(jnp.concatenate(q_ext, axis=0), k_all)
    soft = [_softmax_sink(s_all[head * ts:(head + 1) * ts], allowed, sinks_ref[head])
            for head in range(ATT_HEADS)]
    o_all = _dot(jnp.concatenate([p for p, _ in soft], axis=0), v_all)
    att = [o_all[head * ts:(head + 1) * ts] * soft[head][1] for head in range(ATT_HEADS)]
    for hp in range(ATT_HEADS // 2):
        mix_ref[:, hp * LANES:(hp + 1) * LANES] = _merge_head_pair(
            att[2 * hp], att[2 * hp + 1], (2 * hp) // GROUP)
    for s in range(sb):
        nk_ref[slot, s, 0:WINDOW - n_new] = ck_ref[s, n_new:WINDOW]
        nv_ref[slot, s, 0:WINDOW - n_new] = cv_ref[s, n_new:WINDOW]
        nk_ref[slot, s, WINDOW - n_new:WINDOW] = k_new[s * n_new:(s + 1) * n_new]
        nv_ref[slot, s, WINDOW - n_new:WINDOW] = v_new[s * n_new:(s + 1) * n_new]

    lb = _lower_bound(lb_ref[...], layer)
    log_f, k_in = _hgrn_gates(p_ref[:, OFF_HF:OFF_HF + HG_WIDTH], lb)
    b = _cumsum_rows(tri_ref[...], log_f)
    later = _cumsum_rows(rev_ref[...], log_f)
    q_in = _silu(p_ref[:, OFF_HQ:OFF_HQ + HG_WIDTH])
    v_in = p_ref[:, OFF_HI:OFF_HI + HG_WIDTH]
    gate = _silu(p_ref[:, OFF_HG:OFF_HG + HG_WIDTH])
    onorm = onorm_ref[...]
    heads = [slice(hh * HG_DK, (hh + 1) * HG_DK) for hh in range(HG_HEADS)]

    pad_ref[...] = jnp.zeros_like(pad_ref)
    for i, x in enumerate((k_in, b, v_in)):
        pad_ref[i, SUBLANES:SUBLANES + ts] = x
    tok = lax.broadcasted_iota(jnp.int32, (ts, 1), 0) & (n_new - 1)
    o_intra = [jnp.zeros((ts, HG_DV), F32) for _ in heads]
    for d in range(n_new):
        k_d, b_d, v_d = (pad_ref[i, SUBLANES - d:SUBLANES - d + ts] for i in range(3))
        w = q_in * k_d * jnp.exp(jnp.minimum(b - b_d, 0.0))
        for hh, sl in enumerate(heads):
            a_col = jnp.sum(w[:, sl], axis=-1, keepdims=True)
            o_intra[hh] = o_intra[hh] + jnp.where(tok >= d, a_col, 0.0) * v_d[:, sl]

    q_dec = (q_in * jnp.exp(b)).astype(BF16)
    seq_of_row = lax.broadcasted_iota(jnp.int32, (ts, 1), 0) >> tok_shift
    for hh, sl in enumerate(heads):
        o = o_intra[hh]
        for s in range(sb):
            o_s = _dot(q_dec[:, sl], s0_ref[s, hh].astype(BF16))
            o = o + jnp.where(seq_of_row == s, o_s, 0.0)
        mix_ref[:, ATT_WIDTH + hh * HG_DV:ATT_WIDTH + (hh + 1) * HG_DV] = _rms(o, onorm) * gate[:, sl]

    k_hat = k_in * jnp.exp(later)
    e_b = jnp.exp(b)
    e1 = e_b.astype(BF16).astype(F32)
    e2 = (e_b - e1).astype(BF16).astype(F32)
    e3 = e_b - e1 - e2
    own = ((lax.broadcasted_iota(jnp.int32, (ts, sb * HG_DV), 1) >> key_shift)
           == (lax.broadcasted_iota(jnp.int32, (ts, sb * HG_DV), 0) >> tok_shift))
    no_v = jnp.zeros((3 * ts, sb * HG_DV), F32)
    for hh, sl in enumerate(heads):
        lhs = jnp.concatenate([k_hat[:, sl], e1[:, sl], e2[:, sl], e3[:, sl]], axis=0).T.astype(BF16)
        v_rep = jnp.concatenate([v_in[:, sl]] * sb, axis=1)
        v_diag = jnp.concatenate([jnp.where(own, v_rep, 0.0), no_v], axis=0).astype(BF16)
        upd = _dot(lhs, v_diag)
        decay = _dot(lhs, sel_ref[...])
        for s in range(sb):
            cols = slice(s * HG_DV, (s + 1) * HG_DV)
            ns_ref[slot, s, hh] = decay[:, cols] * s0_ref[s, hh] + upd[:, cols]


def _sample_layer(layer, x, rope, cache_k, cache_v, state, new_caches, sinks, lb_all, anorm, w_in, onorm, w_o,
                  mnorm, w_up, w_dn, fnorm):
    n_seq, n_new, _ = x.shape
    n_tok = n_seq * n_new
    ts = LANES // 4
    sb = ts // n_new
    assert sb * n_new == ts and n_seq % sb == 0 and n_new <= SUBLANES
    assert n_new & (n_new - 1) == 0 and WINDOW & (WINDOW - 1) == 0 and HG_DV == WINDOW
    last = layer == DEPTH - 1
    x2 = x.reshape(n_tok, D_MODEL)
    params = pltpu.CompilerParams(vmem_limit_bytes=VMEM_LIMIT)
    whole = _const_spec
    slab = functools.partial(_layer_spec, layer)

    proj = pl.pallas_call(
        _sample_in_kernel,
        grid=(IN_WIDTH // SAMPLE_IN_BLOCK,),
        in_specs=[whole((n_tok, D_MODEL)), slab((1, D_MODEL)),
                  pl.BlockSpec((None, D_MODEL, SAMPLE_IN_BLOCK), lambda c: (layer, 0, c))],
        out_specs=pl.BlockSpec((n_tok, SAMPLE_IN_BLOCK), lambda c: (0, c)),
        out_shape=jax.ShapeDtypeStruct((n_tok, IN_WIDTH), F32),
        scratch_shapes=[pltpu.VMEM((n_tok, D_MODEL), BF16)],
        compiler_params=params,
        name=f"sample_in{layer}",
    )(x2, anorm, w_in)

    r = jnp.arange(ts)
    same_seq = (r[:, None] // n_new) == (r[None, :] // n_new)
    tri = (same_seq & (r[None, :] <= r[:, None])).astype(BF16)
    rev = (same_seq & (r[None, :] > r[:, None])).astype(BF16)
    c = jnp.arange(sb * HG_DV)
    is_last = ((r[:, None] % n_new) == n_new - 1) & ((r[:, None] // n_new) == (c[None, :] // HG_DV))
    sel = jnp.concatenate([jnp.zeros_like(is_last)] + [is_last] * 3, axis=0).astype(BF16)

    tok_spec = lambda width: pl.BlockSpec((ts, width), lambda i: (i, 0))
    seq_spec = lambda *tail: pl.BlockSpec((sb,) + tail, lambda i: (i,) + (0,) * len(tail))
    layer_spec = lambda *tail: pl.BlockSpec((None, sb) + tail, lambda i: (layer, i) + (0,) * len(tail))
    rope_spec = _const_spec((ts, LANES))
    first = not new_caches
    slab_spec = lambda *tail: pl.BlockSpec(
        (DEPTH if first else 1, sb) + tail, lambda i: (0 if first else layer, i) + (0,) * len(tail))
    n_in = SAMPLE_MIXER_INPUTS
    aliases = {n_in + i: 1 + i for i in range(len(new_caches))}
    stacked = [(DEPTH, n_seq, WINDOW, KV_WIDTH)] * 2 + [(DEPTH, n_seq, HG_HEADS, HG_DK, HG_DV)]
    mix, *new_caches = pl.pallas_call(
        functools.partial(_sample_mixer_kernel, layer, n_new, len(new_caches)),
        grid=(n_seq // sb,),
        in_specs=[
            tok_spec(IN_WIDTH),
            rope_spec, rope_spec, rope_spec,
            _const_spec((ts, ts)), _const_spec((ts, ts)), _const_spec((4 * ts, sb * HG_DV)),
            layer_spec(WINDOW, KV_WIDTH), layer_spec(WINDOW, KV_WIDTH),
            layer_spec(HG_HEADS, HG_DK, HG_DV),
            pl.BlockSpec(memory_space=pltpu.SMEM),
            _const_spec((DEPTH, HG_WIDTH)),
            _layer_spec(layer, (1, HG_DV)),
        ] + [pl.BlockSpec(memory_space=pl.ANY)] * len(new_caches),
        out_specs=[
            tok_spec(MIX_WIDTH),
            slab_spec(WINDOW, KV_WIDTH), slab_spec(WINDOW, KV_WIDTH),
            slab_spec(HG_HEADS, HG_DK, HG_DV),
        ],
        out_shape=[jax.ShapeDtypeStruct((n_tok, MIX_WIDTH), F32)]
        + [jax.ShapeDtypeStruct(shape, F32) for shape in stacked],
        input_output_aliases=aliases,
        scratch_shapes=[pltpu.VMEM((3, SUBLANES + ts, HG_WIDTH), F32)],
        compiler_params=pltpu.CompilerParams(
            dimension_semantics=("arbitrary",), vmem_limit_bytes=VMEM_LIMIT),
        name=f"sample_mixer{layer}",
    )(proj, *(jnp.tile(t, (sb, 1)) for t in rope), tri, rev, sel, cache_k, cache_v, state, sinks, lb_all,
      onorm, *new_caches)

    y = pl.pallas_call(
        functools.partial(_sample_out_kernel, last),
        grid=(D_FF // SAMPLE_FF_BLOCK,),
        in_specs=[whole((n_tok, D_MODEL)), whole((n_tok, MIX_WIDTH)), slab((MIX_WIDTH, D_MODEL)),
                  slab((1, D_MODEL)),
                  pl.BlockSpec((None, D_MODEL, SAMPLE_FF_BLOCK), lambda c: (layer, 0, c)),
                  pl.BlockSpec((None, SAMPLE_FF_BLOCK, D_MODEL), lambda c: (layer, c, 0)),
                  whole((1, D_MODEL))],
        out_specs=pl.BlockSpec((n_tok, D_MODEL), lambda c: (0, 0)),
        out_shape=jax.ShapeDtypeStruct((n_tok, D_MODEL), F32),
        scratch_shapes=[pltpu.VMEM((n_tok, D_MODEL), BF16), pltpu.VMEM((n_tok, D_MODEL), F32)],
        compiler_params=params,
        name=f"sample_out{layer}",
    )(x2, mix, w_o, mnorm, w_up, w_dn, fnorm)
    return y.reshape(n_seq, n_new, D_MODEL), new_caches


def _rope_tables(pos):
    n = pos.shape[0]
    inv_freq = jnp.power(ROPE_THETA, -jnp.arange(ROT_HALF, dtype=F32) * (2.0 / ROT_DIM))
    ang = pos.astype(F32)[:, None] * inv_freq[None, :]
    cos, sin = jnp.cos(ang), jnp.sin(ang)
    rest = HEAD_DIM - ROT_DIM
    zeros_h = jnp.zeros((n, ROT_HALF), F32)
    cos_t = jnp.concatenate([cos, cos, jnp.ones((n, rest), F32)], axis=1)
    sdn_t = jnp.concatenate([zeros_h, sin, jnp.zeros((n, rest), F32)], axis=1)
    sup_t = jnp.concatenate([-sin, zeros_h, jnp.zeros((n, rest), F32)], axis=1)
    reps = LANES // HEAD_DIM
    return tuple(jnp.tile(t, (1, reps)) for t in (cos_t, sdn_t, sup_t))


def _chunk_tri(n, chunk):
    r = jnp.arange(n)
    same = (r[:, None] // chunk) == (r[None, :] // chunk)
    return (same & (r[None, :] <= r[:, None])).astype(BF16)


def kernel(x_prompt, x_sample, cache_k, cache_v, state_hgrn, attn_norm, w_in, att_sinks, hgrn_lower_bounds,
           hgrn_out_norm, w_o, mlp_norm, w_up, w_down, final_norm):
    batch, seq, _ = x_prompt.shape
    n_seq, n_new, _ = x_sample.shape
    assert seq % PROMPT_BLOCK == 0 and PROMPT_BLOCK % HG_CHUNK == 0

    rope_p = _rope_tables(jnp.arange(seq, dtype=jnp.int32))
    rope_s = _rope_tables(PAST_LEN + jnp.arange(n_new, dtype=jnp.int32))
    tri = _chunk_tri(PROMPT_BLOCK, HG_CHUNK)
    lb_all = hgrn_lower_bounds.astype(F32)
    fnorm = final_norm.reshape(1, D_MODEL)

    ck = cache_k.reshape(DEPTH, n_seq, WINDOW, KV_WIDTH)
    cv = cache_v.reshape(DEPTH, n_seq, WINDOW, KV_WIDTH)
    weights = (attn_norm.reshape(DEPTH, 1, D_MODEL), w_in.astype(BF16), hgrn_out_norm.reshape(DEPTH, 1, HG_DV),
               w_o.astype(BF16), mlp_norm.reshape(DEPTH, 1, D_MODEL), w_up.astype(BF16), w_down.astype(BF16),
               fnorm)
    xp, xs = x_prompt, x_sample
    nk_p, nv_p, ns_p, new_caches = [], [], [], []
    for l in range(DEPTH):
        xp, k1, v1, s1 = _prompt_layer(l, xp, rope_p, tri, att_sinks[l], lb_all, *weights)
        xs, new_caches = _sample_layer(l, xs, rope_s, ck, cv, state_hgrn, new_caches, att_sinks[l], lb_all,
                                       *weights)
        nk_p.append(k1), nv_p.append(v1), ns_p.append(s1)
    nk_s, nv_s, ns_s = new_caches

    kv_p = (DEPTH, batch, WINDOW, ATT_KV_HEADS, HEAD_DIM)
    kv_s = (DEPTH, n_seq, WINDOW, ATT_KV_HEADS, HEAD_DIM)
    return (xp, xs,
            jnp.stack(nk_p).reshape(kv_p), jnp.stack(nv_p).reshape(kv_p), jnp.stack(ns_p),
            nk_s.reshape(kv_s), nv_s.reshape(kv_s), ns_s)
```

```python
import functools

import jax
import jax.numpy as jnp
from jax import lax
from jax.experimental import pallas as pl
from jax.experimental.pallas import tpu as pltpu

F32 = jnp.float32
BF16 = jnp.bfloat16

D_MODEL = 1024
DEPTH = 2
PAST_LEN = 16384
ATT_HEADS = 8
ATT_KV_HEADS = 2
HEAD_DIM = 64
GROUP = ATT_HEADS // ATT_KV_HEADS
ATT_WIDTH = ATT_HEADS * HEAD_DIM
KV_WIDTH = ATT_KV_HEADS * HEAD_DIM
WINDOW = 128
ROT_DIM = HEAD_DIM // 4
ROT_HALF = ROT_DIM // 2
ROPE_THETA = 500000.0
HG_HEADS = 4
HG_DK = 128
HG_DV = 128
HG_WIDTH = HG_HEADS * HG_DK
MIX_WIDTH = ATT_WIDTH + HG_WIDTH
IN_WIDTH = ATT_WIDTH + 2 * KV_WIDTH + 4 * HG_WIDTH
D_FF = 4 * D_MODEL
EPS = 1e-6

OFF_Q = 0
OFF_K = ATT_WIDTH
OFF_V = OFF_K + KV_WIDTH
OFF_HQ = OFF_V + KV_WIDTH
OFF_HF = OFF_HQ + HG_WIDTH
OFF_HI = OFF_HF + HG_WIDTH
OFF_HG = OFF_HI + HG_WIDTH

LANES = 128
PROMPT_BLOCK = 256
HG_CHUNK = 128
HG_SUB = 32
HG_GUARD = 80.0
FF_BLOCK = 2048
SUBLANES = 8
SAMPLE_IN_BLOCK = 1408
SAMPLE_FF_BLOCK = 1024
SAMPLE_MIXER_INPUTS = 13
VMEM_LIMIT = 58 * 1024 * 1024


def _dot(a, b):
    return jnp.dot(a, b, preferred_element_type=F32)


def _dot_nt(a, b):
    return lax.dot_general(a, b, (((1,), (1,)), ((), ())), preferred_element_type=F32)


def _rms(x, g_row):
    ms = jnp.mean(x * x, axis=-1, keepdims=True)
    return (x * lax.rsqrt(ms + EPS)) * g_row


def _rope(x, cos, sin_dn, sin_up):
    return x * cos + pltpu.roll(x, ROT_HALF, 1) * sin_dn + pltpu.roll(x, LANES - ROT_HALF, 1) * sin_up


def _lower_bound(lb_all, layer):
    m = jnp.max(lb_all, axis=0, keepdims=True)
    e = jnp.exp(lb_all - m)
    p = e / jnp.sum(e, axis=0, keepdims=True)
    cs = p[0:1]
    for l in range(1, layer + 1):
        cs = cs + p[l:l + 1]
    return jnp.maximum(cs - p[0:1], 0.0)


def _hgrn_gates(z, lb):
    e = jnp.exp(-jnp.abs(z))
    log_sig = jnp.minimum(z, 0.0) - jnp.log1p(e)
    a1 = jnp.log(lb)
    a2 = jnp.log1p(-lb) + log_sig
    log_f = jnp.maximum(a1, a2) + jnp.log1p(jnp.exp(-jnp.abs(a1 - a2)))
    k_in = (1.0 - lb) * (jnp.where(z >= 0.0, e, 1.0) / (1.0 + e))
    return log_f, k_in


def _silu(x):
    return x / (1.0 + jnp.exp(-x))


def _stacked_queries(proj_ref, cos, sdn, sup):
    t = proj_ref.shape[0]
    lo_half = lax.broadcasted_iota(jnp.int32, (t, LANES), 1) < HEAD_DIM
    q_ext = []
    for hp in range(ATT_HEADS // 2):
        q_slab = _rope(proj_ref[:, OFF_Q + hp * LANES:OFF_Q + (hp + 1) * LANES], cos, sdn, sup)
        q_slab = q_slab * (HEAD_DIM ** -0.5)
        q_swap = pltpu.roll(q_slab, HEAD_DIM, 1)
        for sub in range(2):
            kvh = (2 * hp + sub) // GROUP
            src = q_slab if sub == kvh else q_swap
            keep = lo_half if kvh == 0 else ~lo_half
            q_ext.append(jnp.where(keep, src, 0.0).astype(BF16))
    return q_ext


def _softmax_sink(s, allowed, sink):
    s = jnp.where(allowed, s, -jnp.inf)
    m = jnp.maximum(jnp.max(s, axis=-1, keepdims=True), sink)
    p = jnp.exp(s - m)
    inv = 1.0 / (jnp.sum(p, axis=-1, keepdims=True) + jnp.exp(sink - m))
    return p.astype(BF16), inv


def _merge_head_pair(a, b, kvh):
    lo_half = lax.broadcasted_iota(jnp.int32, a.shape, 1) < HEAD_DIM
    if kvh == 0:
        return jnp.where(lo_half, a, pltpu.roll(b, HEAD_DIM, 1))
    return jnp.where(lo_half, pltpu.roll(a, HEAD_DIM, 1), b)


def _cumsum_rows(tri, g):
    g1 = g.astype(BF16)
    r1 = g - g1.astype(F32)
    g2 = r1.astype(BF16)
    g3 = (r1 - g2.astype(F32)).astype(BF16)
    return _dot(tri, g1) + _dot(tri, g2) + _dot(tri, g3)


def _hgrn_intra_operands(q, k, b):
    c = q.shape[0]
    q_parts, k_parts = [], []
    for i in range(c // HG_SUB):
        lo, hi = i * HG_SUB, (i + 1) * HG_SUB
        r = b[lo - 1:lo] if i > 0 else jnp.zeros((1, HG_DK), F32)
        q_parts.append(q[lo:hi] * jnp.exp(b[lo:hi] - r))
        k_parts.append(k[:hi] * jnp.exp(r - b[:hi]))
        if hi < c:
            k_parts.append(jnp.zeros((c - hi, HG_DK), F32))
    return (jnp.concatenate(q_parts, axis=0).astype(BF16),
            jnp.concatenate(k_parts, axis=0).astype(BF16))


def _hgrn_worst_subblock_decay(b_all):
    worst = None
    for lo in range(0, b_all.shape[0], HG_SUB):
        tot = b_all[lo + HG_SUB - 1:lo + HG_SUB]
        if lo % HG_CHUNK:
            tot = tot - b_all[lo - 1:lo]
        worst = tot if worst is None else jnp.minimum(worst, tot)
    return worst


def _hgrn_intra_exact(q, k, v, b, work_ref, out_ref):
    for i, x in enumerate((q, k, v, b)):
        work_ref[i] = x
    s_idx = lax.broadcasted_iota(jnp.int32, (HG_CHUNK, 1), 0)

    def row(t, carry):
        c0 = pl.multiple_of((t // HG_CHUNK) * HG_CHUNK, HG_CHUNK)
        q_t = work_ref[0, pl.ds(t, 1), :]
        b_t = work_ref[3, pl.ds(t, 1), :]
        k_c = work_ref[1, pl.ds(c0, HG_CHUNK), :]
        v_c = work_ref[2, pl.ds(c0, HG_CHUNK), :]
        b_c = work_ref[3, pl.ds(c0, HG_CHUNK), :]
        w = k_c * jnp.exp(jnp.minimum(b_t - b_c, 0.0)) * q_t
        a = jnp.where(s_idx <= t - c0, jnp.sum(w, axis=-1, keepdims=True), 0.0)
        out_ref[pl.ds(t, 1), :] = jnp.sum(a * v_c, axis=0, keepdims=True)
        return carry

    lax.fori_loop(0, q.shape[0], row, 0)
    return out_ref[...]


def _hgrn_intra_scores(a_full):
    c = a_full.shape[0]
    a = jnp.concatenate(
        [a_full[i * HG_SUB:(i + 1) * HG_SUB, i * c:(i + 1) * c] for i in range(c // HG_SUB)], axis=0)
    row = lax.broadcasted_iota(jnp.int32, (c, c), 0)
    col = lax.broadcasted_iota(jnp.int32, (c, c), 1)
    return jnp.where(col <= row, a, 0.0).astype(BF16)


def _out_proj(x, mix_bf16, wo_ref, mnorm_ref):
    x1 = x + _dot(mix_bf16, wo_ref[...])
    return x1, _rms(x1, mnorm_ref[...]).astype(BF16)


def _mlp_cols(acc, h2, w_up_cols, w_dn_rows):
    u = jnp.square(jnp.maximum(_dot(h2, w_up_cols), 0.0)).astype(BF16)
    return acc + _dot(u, w_dn_rows)


def _mlp_block(acc, h2, c, wup_ref, wdn_ref):
    return _mlp_cols(acc, h2, wup_ref[:, c * FF_BLOCK:(c + 1) * FF_BLOCK],
                     wdn_ref[c * FF_BLOCK:(c + 1) * FF_BLOCK, :])


def _prompt_layer_kernel(layer, last, n_tblk,
                         x_ref, cos_ref, sdn_ref, sup_ref, tri_ref, sinks_ref, lb_ref, anorm_ref,
                         win_ref, onorm_ref, wo_ref, mnorm_ref, wup_ref, wdn_ref, fnorm_ref,
                         y_ref, nk_ref, nv_ref, ns_ref,
                         proj_ref, mix_ref, xprev_ref, kprev_ref, vprev_ref, st_ref, work_ref, exact_ref):
    j = pl.program_id(0)
    n_steps = pl.num_programs(0)
    live = j < n_steps - 1
    t_blk = lax.rem(jnp.minimum(j, n_steps - 2), n_tblk)
    tb = PROMPT_BLOCK

    @pl.when(j == 0)
    def _():
        xprev_ref[...] = jnp.zeros_like(xprev_ref)
        mix_ref[...] = jnp.zeros_like(mix_ref)

    @pl.when(t_blk == 0)
    def _():
        kprev_ref[...] = jnp.zeros_like(kprev_ref)
        vprev_ref[...] = jnp.zeros_like(vprev_ref)
        st_ref[...] = jnp.zeros_like(st_ref)

    acc, h2 = _out_proj(xprev_ref[...], mix_ref[...], wo_ref, mnorm_ref)

    x = x_ref[0]
    xprev_ref[...] = x
    h = _rms(x, anorm_ref[...]).astype(BF16)
    proj_ref[...] = _dot(h, win_ref[...])

    cos, sdn, sup = cos_ref[...], sdn_ref[...], sup_ref[...]

    k_rot = _rope(proj_ref[:, OFF_K:OFF_K + KV_WIDTH], cos, sdn, sup)
    v_new = proj_ref[:, OFF_V:OFF_V + KV_WIDTH]
    k_all = jnp.concatenate([kprev_ref[...], k_rot], axis=0).astype(BF16)
    v_all = jnp.concatenate([vprev_ref[...], v_new], axis=0).astype(BF16)

    q_ext = _stacked_queries(proj_ref, cos, sdn, sup)
    row = lax.broadcasted_iota(jnp.int32, (WINDOW, 2 * WINDOW), 0)
    col = lax.broadcasted_iota(jnp.int32, (WINDOW, 2 * WINDOW), 1)
    in_window = (col >= row) & (col <= row + WINDOW)
    first_lo = jnp.where(t_blk == 0, WINDOW, 0)
    att = [[None] * (tb // WINDOW) for _ in range(ATT_HEADS)]
    for n in range(tb // WINDOW):
        rows_n = slice(n * WINDOW, (n + 1) * WINDOW)
        keys_n = slice(n * WINDOW, (n + 2) * WINDOW)
        q_stack = jnp.concatenate([q[rows_n] for q in q_ext], axis=0)
        s_all = _dot_nt(q_stack, k_all[keys_n])
        allowed = in_window & (col >= first_lo) if n == 0 else in_window
        soft = [_softmax_sink(s_all[head * WINDOW:(head + 1) * WINDOW], allowed, sinks_ref[head])
                for head in range(ATT_HEADS)]
        o_all = _dot(jnp.concatenate([p for p, _ in soft], axis=0), v_all[keys_n])
        for head in range(ATT_HEADS):
            att[head][n] = o_all[head * WINDOW:(head + 1) * WINDOW] * soft[head][1]
    for hp in range(ATT_HEADS // 2):
        a, b = (jnp.concatenate(att[2 * hp + sub], axis=0) for sub in range(2))
        mix_ref[:, hp * LANES:(hp + 1) * LANES] = _merge_head_pair(a, b, (2 * hp) // GROUP).astype(BF16)
    acc = _mlp_block(acc, h2, 0, wup_ref, wdn_ref)

    lb = _lower_bound(lb_ref[...], layer)
    log_f, k_in = _hgrn_gates(proj_ref[:, OFF_HF:OFF_HF + HG_WIDTH], lb)
    b_all = _cumsum_rows(tri_ref[...], log_f)
    onorm = onorm_ref[...]
    n_chunks = tb // HG_CHUNK
    units = [(hh, c) for c in range(n_chunks) for hh in range(HG_HEADS)]
    q_in, v_in = [], []
    for hh in range(HG_HEADS):
        q_in.append(_silu(proj_ref[:, OFF_HQ + hh * HG_DK:OFF_HQ + (hh + 1) * HG_DK]))
        v_in.append(proj_ref[:, OFF_HI + hh * HG_DV:OFF_HI + (hh + 1) * HG_DV])

    def piece(x, hh, c, lanes=False):
        x = x[:, hh * HG_DK:(hh + 1) * HG_DK] if lanes else x[hh]
        return x[c * HG_CHUNK:(c + 1) * HG_CHUNK]

    intra_ops = {u: _hgrn_intra_operands(piece(q_in, *u), piece(k_in, *u, lanes=True),
                                         piece(b_all, *u, lanes=True)) for u in units}
    a_full = {u: _dot_nt(*intra_ops[u]) for u in units}
    o_inter, st_new = {}, []
    for hh in range(HG_HEADS):
        st = st_ref[hh]
        for c in range(n_chunks):
            q, k, v, b = (piece(q_in, hh, c), piece(k_in, hh, c, True), piece(v_in, hh, c),
                          piece(b_all, hh, c, True))
            o_inter[(hh, c)] = _dot_nt((q * jnp.exp(b)).astype(BF16), st.astype(BF16))
            b_last = b[HG_CHUNK - 1:HG_CHUNK]
            k_hat = (k * jnp.exp(b_last - b)).astype(BF16)
            st = st * jnp.exp(b_last) + _dot(v.T.astype(BF16), k_hat)
        st_new.append(st)
    a_mask = {u: _hgrn_intra_scores(a_full[u]) for u in units}
    o_intra = {u: _dot(a_mask[u], piece(v_in, *u).astype(BF16)) for u in units}

    def emit_head(hh, o_intra_h):
        o = o_intra_h + jnp.concatenate([o_inter[(hh, c)] for c in range(n_chunks)], axis=0)
        gate = _silu(proj_ref[:, OFF_HG + hh * HG_DV:OFF_HG + (hh + 1) * HG_DV])
        mix_ref[:, ATT_WIDTH + hh * HG_DV:ATT_WIDTH + (hh + 1) * HG_DV] = (
            _rms(o, onorm) * gate).astype(BF16)

    for hh in range(HG_HEADS):
        emit_head(hh, jnp.concatenate([o_intra[(hh, c)] for c in range(n_chunks)], axis=0))
    acc = _mlp_block(acc, h2, 1, wup_ref, wdn_ref)
    y_ref[0] = _rms(acc, fnorm_ref[...]) if last else acc

    @pl.when(jnp.min(_hgrn_worst_subblock_decay(b_all)) < -HG_GUARD)
    def _():
        for hh in range(HG_HEADS):
            sl = slice(hh * HG_DK, (hh + 1) * HG_DK)
            emit_head(hh, _hgrn_intra_exact(q_in[hh], k_in[:, sl], v_in[hh], b_all[:, sl],
                                            work_ref, exact_ref))

    @pl.when(live)
    def _():
        kprev_ref[...] = k_rot[tb - WINDOW:]
        vprev_ref[...] = v_new[tb - WINDOW:]
        nk_ref[0] = k_rot[tb - WINDOW:]
        nv_ref[0] = v_new[tb - WINDOW:]
        for hh in range(HG_HEADS):
            st_ref[hh] = st_new[hh]

    @pl.when(live & (t_blk == n_tblk - 1))
    def _():
        for hh in range(HG_HEADS):
            ns_ref[0, hh] = st_new[hh].T


def _const_spec(shape):
    nd = len(shape)
    return pl.BlockSpec(shape, lambda *_: (0,) * nd, pipeline_mode=pl.Buffered(1))


def _layer_spec(layer, shape):
    nd = len(shape)
    return pl.BlockSpec((None,) + shape, lambda *_: (layer,) + (0,) * nd, pipeline_mode=pl.Buffered(1))


def _prompt_layer(layer, x, rope, tri, sinks, lb_all, anorm, w_in, onorm, w_o, mnorm, w_up, w_dn, fnorm):
    batch, seq, _ = x.shape
    tb = PROMPT_BLOCK
    n_tblk = seq // tb
    n_blocks = batch * n_tblk
    last = layer == DEPTH - 1
    mixer_blk = lambda j: jnp.minimum(j, n_blocks - 1)
    tail_blk = lambda j: jnp.maximum(j - 1, 0)
    rope_spec = pl.BlockSpec((tb, LANES), lambda j: (mixer_blk(j) % n_tblk, 0))
    carry_spec = pl.BlockSpec((1, WINDOW, KV_WIDTH), lambda j: (mixer_blk(j) // n_tblk, 0, 0))
    return pl.pallas_call(
        functools.partial(_prompt_layer_kernel, layer, last, n_tblk),
        grid=(n_blocks + 1,),
        in_specs=[
            pl.BlockSpec((1, tb, D_MODEL), lambda j: (mixer_blk(j) // n_tblk, mixer_blk(j) % n_tblk, 0)),
            rope_spec, rope_spec, rope_spec,
            _const_spec((tb, tb)),
            pl.BlockSpec(memory_space=pltpu.SMEM),
            _const_spec((DEPTH, HG_WIDTH)),
            _layer_spec(layer, (1, D_MODEL)),
            _const_spec((D_MODEL, IN_WIDTH)),
            _layer_spec(layer, (1, HG_DV)),
            _const_spec((MIX_WIDTH, D_MODEL)),
            _layer_spec(layer, (1, D_MODEL)),
            _const_spec((D_MODEL, D_FF)),
            _const_spec((D_FF, D_MODEL)),
            _const_spec((1, D_MODEL)),
        ],
        out_specs=[
            pl.BlockSpec((1, tb, D_MODEL), lambda j: (tail_blk(j) // n_tblk, tail_blk(j) % n_tblk, 0)),
            carry_spec, carry_spec,
            pl.BlockSpec((1, HG_HEADS, HG_DK, HG_DV), lambda j: (mixer_blk(j) // n_tblk, 0, 0, 0)),
        ],
        out_shape=[
            jax.ShapeDtypeStruct((batch, seq, D_MODEL), F32),
            jax.ShapeDtypeStruct((batch, WINDOW, KV_WIDTH), F32),
            jax.ShapeDtypeStruct((batch, WINDOW, KV_WIDTH), F32),
            jax.ShapeDtypeStruct((batch, HG_HEADS, HG_DK, HG_DV), F32),
        ],
        scratch_shapes=[
            pltpu.VMEM((tb, IN_WIDTH), F32),
            pltpu.VMEM((tb, MIX_WIDTH), BF16),
            pltpu.VMEM((tb, D_MODEL), F32),
            pltpu.VMEM((WINDOW, KV_WIDTH), F32),
            pltpu.VMEM((WINDOW, KV_WIDTH), F32),
            pltpu.VMEM((HG_HEADS, HG_DV, HG_DK), F32),
            pltpu.VMEM((4, tb, HG_DK), F32),
            pltpu.VMEM((tb, HG_DV), F32),
        ],
        compiler_params=pltpu.CompilerParams(
            dimension_semantics=("arbitrary",), vmem_limit_bytes=VMEM_LIMIT),
        name=f"prompt_layer{layer}",
    )(x, *rope, tri, sinks, lb_all, anorm, w_in, onorm, w_o, mnorm, w_up, w_dn, fnorm)


def _sample_in_kernel(x_ref, anorm_ref, win_ref, proj_ref, win_bf_ref, h_ref):
    @pl.when(pl.program_id(0) == 0)
    def _():
        h_ref[...] = _rms(x_ref[...], anorm_ref[...]).astype(BF16)

    win_bf_ref[...] = win_ref[...].astype(BF16)
    proj_ref[...] = _dot(h_ref[...], win_bf_ref[...])


def _sample_out_kernel(last, x_ref, mix_ref, wo_ref, mnorm_ref, wup_ref, wdn_ref, fnorm_ref,
                       y_ref, wo_bf_ref, wup_bf_ref, wdn_bf_ref, h2_ref, acc_ref):
    c = pl.program_id(0)

    @pl.when(c == 0)
    def _():
        wo_bf_ref[...] = wo_ref[...].astype(BF16)
        acc_ref[...], h2_ref[...] = _out_proj(x_ref[...], mix_ref[...].astype(BF16), wo_bf_ref, mnorm_ref)

    wup_bf_ref[...] = wup_ref[...].astype(BF16)
    wdn_bf_ref[...] = wdn_ref[...].astype(BF16)
    acc_ref[...] = _mlp_cols(acc_ref[...], h2_ref[...], wup_bf_ref[...], wdn_bf_ref[...])

    @pl.when(c == pl.num_programs(0) - 1)
    def _():
        y_ref[...] = _rms(acc_ref[...], fnorm_ref[...]) if last else acc_ref[...]


def _sample_mixer_kernel(layer, n_new, n_aliased, *refs):
    (p_ref, cos_ref, sdn_ref, sup_ref, tri_ref, rev_ref, sel_ref, ck_ref, cv_ref, s0_ref, sinks_ref, lb_ref,
     onorm_ref) = refs[:SAMPLE_MIXER_INPUTS]
    mix_ref, nk_ref, nv_ref, ns_ref, pad_ref = refs[SAMPLE_MIXER_INPUTS + n_aliased:]
    slot = 0 if n_aliased else layer
    for out_ref in (nk_ref, nv_ref, ns_ref):
        for other in range(out_ref.shape[0]):
            if other != slot:
                out_ref[other] = jnp.zeros(out_ref.shape[1:], F32)
    sb = ck_ref.shape[0]
    ts = p_ref.shape[0]
    tok_shift = n_new.bit_length() - 1
    key_shift = WINDOW.bit_length() - 1
    n_cache = sb * WINDOW
    cos, sdn, sup = cos_ref[...], sdn_ref[...], sup_ref[...]

    k_new = _rope(p_ref[:, OFF_K:OFF_K + KV_WIDTH], cos, sdn, sup)
    v_new = p_ref[:, OFF_V:OFF_V + KV_WIDTH]
    pad_keys = jnp.zeros((LANES - ts, KV_WIDTH), F32)
    k_all = jnp.concatenate([ck_ref[...].reshape(n_cache, KV_WIDTH), k_new, pad_keys], axis=0).astype(BF16)
    v_all = jnp.concatenate([cv_ref[...].reshape(n_cache, KV_WIDTH), v_new, pad_keys], axis=0).astype(BF16)
    n_keys = n_cache + LANES
    row = lax.broadcasted_iota(jnp.int32, (ts, n_keys), 0)
    col = lax.broadcasted_iota(jnp.int32, (ts, n_keys), 1)
    new_idx = col - n_cache
    cached_ok = ((col >> key_shift) == (row >> tok_shift)) & ((col & (WINDOW - 1)) >= (row & (n_new - 1)))
    new_ok = ((new_idx < ts) & ((new_idx >> tok_shift) == (row >> tok_shift))
              & ((new_idx & (n_new - 1)) <= (row & (n_new - 1))))
    allowed = ((col < n_cache) & cached_ok) | ((col >= n_cache) & new_ok)

    q_ext = _stacked_queries(p_ref, cos, sdn, sup)
    s_all = _dot_nt(jnp.concatenate(q_ext, axis=0), k_all)
    soft = [_softmax_sink(s_all[head * ts:(head + 1) * ts], allowed, sinks_ref[head])
            for head in range(ATT_HEADS)]
    o_all = _dot(jnp.concatenate([p for p, _ in soft], axis=0), v_all)
    att = [o_all[head * ts:(head + 1) * ts] * soft[head][1] for head in range(ATT_HEADS)]
    for hp in range(ATT_HEADS // 2):
        mix_ref[:, hp * LANES:(hp + 1) * LANES] = _merge_head_pair(
            att[2 * hp], att[2 * hp + 1], (2 * hp) // GROUP)
    for s in range(sb):
        nk_ref[slot, s, 0:WINDOW - n_new] = ck_ref[s, n_new:WINDOW]
        nv_ref[slot, s, 0:WINDOW - n_new] = cv_ref[s, n_new:WINDOW]
        nk_ref[slot, s, WINDOW - n_new:WINDOW] = k_new[s * n_new:(s + 1) * n_new]
        nv_ref[slot, s, WINDOW - n_new:WINDOW] = v_new[s * n_new:(s + 1) * n_new]

    lb = _lower_bound(lb_ref[...], layer)
    log_f, k_in = _hgrn_gates(p_ref[:, OFF_HF:OFF_HF + HG_WIDTH], lb)
    b = _cumsum_rows(tri_ref[...], log_f)
    later = _cumsum_rows(rev_ref[...], log_f)
    q_in = _silu(p_ref[:, OFF_HQ:OFF_HQ + HG_WIDTH])
    v_in = p_ref[:, OFF_HI:OFF_HI + HG_WIDTH]
    gate = _silu(p_ref[:, OFF_HG:OFF_HG + HG_WIDTH])
    onorm = onorm_ref[...]
    heads = [slice(hh * HG_DK, (hh + 1) * HG_DK) for hh in range(HG_HEADS)]

    pad_ref[...] = jnp.zeros_like(pad_ref)
    for i, x in enumerate((k_in, b, v_in)):
        pad_ref[i, SUBLANES:SUBLANES + ts] = x
    tok = lax.broadcasted_iota(jnp.int32, (ts, 1), 0) & (n_new - 1)
    o_intra = [jnp.zeros((ts, HG_DV), F32) for _ in heads]
    for d in range(n_new):
        k_d, b_d, v_d = (pad_ref[i, SUBLANES - d:SUBLANES - d + ts] for i in range(3))
        w = q_in * k_d * jnp.exp(jnp.minimum(b - b_d, 0.0))
        for hh, sl in enumerate(heads):
            a_col = jnp.sum(w[:, sl], axis=-1, keepdims=True)
            o_intra[hh] = o_intra[hh] + jnp.where(tok >= d, a_col, 0.0) * v_d[:, sl]

    q_dec = (q_in * jnp.exp(b)).astype(BF16)
    seq_of_row = lax.broadcasted_iota(jnp.int32, (ts, 1), 0) >> tok_shift
    for hh, sl in enumerate(heads):
        o = o_intra[hh]
        for s in range(sb):
            o_s = _dot(q_dec[:, sl], s0_ref[s, hh].astype(BF16))
            o = o + jnp.where(seq_of_row == s, o_s, 0.0)
        mix_ref[:, ATT_WIDTH + hh * HG_DV:ATT_WIDTH + (hh + 1) * HG_DV] = _rms(o, onorm) * gate[:, sl]

    k_hat = k_in * jnp.exp(later)
    e_b = jnp.exp(b)
    e1 = e_b.astype(BF16).astype(F32)
    e2 = (e_b - e1).astype(BF16).astype(F32)
    e3 = e_b - e1 - e2
    own = ((lax.broadcasted_iota(jnp.int32, (ts, sb * HG_DV), 1) >> key_shift)
           == (lax.broadcasted_iota(jnp.int32, (ts, sb * HG_DV), 0) >> tok_shift))
    no_v = jnp.zeros((3 * ts, sb * HG_DV), F32)
    for hh, sl in enumerate(heads):
        lhs = jnp.concatenate([k_hat[:, sl], e1[:, sl], e2[:, sl], e3[:, sl]], axis=0).T.astype(BF16)
        v_rep = jnp.concatenate([v_in[:, sl]] * sb, axis=1)
        v_diag = jnp.concatenate([jnp.where(own, v_rep, 0.0), no_v], axis=0).astype(BF16)
        upd = _dot(lhs, v_diag)
        decay = _dot(lhs, sel_ref[...])
        for s in range(sb):
            cols = slice(s * HG_DV, (s + 1) * HG_DV)
            ns_ref[slot, s, hh] = decay[:, cols] * s0_ref[s, hh] + upd[:, cols]


def _sample_layer(layer, x, rope, cache_k, cache_v, state, new_caches, sinks, lb_all, anorm, w_in, onorm, w_o,
                  mnorm, w_up, w_dn, fnorm):
    n_seq, n_new, _ = x.shape
    n_tok = n_seq * n_new
    ts = LANES // 4
    sb = ts // n_new
    assert sb * n_new == ts and n_seq % sb == 0 and n_new <= SUBLANES
    assert n_new & (n_new - 1) == 0 and WINDOW & (WINDOW - 1) == 0 and HG_DV == WINDOW
    last = layer == DEPTH - 1
    x2 = x.reshape(n_tok, D_MODEL)
    params = pltpu.CompilerParams(vmem_limit_bytes=VMEM_LIMIT)
    whole = _const_spec
    slab = functools.partial(_layer_spec, layer)

    proj, w_in_bf = pl.pallas_call(
        _sample_in_kernel,
        grid=(IN_WIDTH // SAMPLE_IN_BLOCK,),
        in_specs=[whole((n_tok, D_MODEL)), slab((1, D_MODEL)),
                  pl.BlockSpec((None, D_MODEL, SAMPLE_IN_BLOCK), lambda c: (layer, 0, c))],
        out_specs=[pl.BlockSpec((n_tok, SAMPLE_IN_BLOCK), lambda c: (0, c)),
                   pl.BlockSpec((D_MODEL, SAMPLE_IN_BLOCK), lambda c: (0, c))],
        out_shape=[jax.ShapeDtypeStruct((n_tok, IN_WIDTH), F32),
                   jax.ShapeDtypeStruct((D_MODEL, IN_WIDTH), BF16)],
        scratch_shapes=[pltpu.VMEM((n_tok, D_MODEL), BF16)],
        compiler_params=params,
        name=f"sample_in{layer}",
    )(x2, anorm, w_in)

    r = jnp.arange(ts)
    same_seq = (r[:, None] // n_new) == (r[None, :] // n_new)
    tri = (same_seq & (r[None, :] <= r[:, None])).astype(BF16)
    rev = (same_seq & (r[None, :] > r[:, None])).astype(BF16)
    c = jnp.arange(sb * HG_DV)
    is_last = ((r[:, None] % n_new) == n_new - 1) & ((r[:, None] // n_new) == (c[None, :] // HG_DV))
    sel = jnp.concatenate([jnp.zeros_like(is_last)] + [is_last] * 3, axis=0).astype(BF16)

    tok_spec = lambda width: pl.BlockSpec((ts, width), lambda i: (i, 0))
    seq_spec = lambda *tail: pl.BlockSpec((sb,) + tail, lambda i: (i,) + (0,) * len(tail))
    layer_spec = lambda *tail: pl.BlockSpec((None, sb) + tail, lambda i: (layer, i) + (0,) * len(tail))
    rope_spec = _const_spec((ts, LANES))
    first = not new_caches
    slab_spec = lambda *tail: pl.BlockSpec(
        (DEPTH if first else 1, sb) + tail, lambda i: (0 if first else layer, i) + (0,) * len(tail))
    n_in = SAMPLE_MIXER_INPUTS
    aliases = {n_in + i: 1 + i for i in range(len(new_caches))}
    stacked = [(DEPTH, n_seq, WINDOW, KV_WIDTH)] * 2 + [(DEPTH, n_seq, HG_HEADS, HG_DK, HG_DV)]
    mix, *new_caches = pl.pallas_call(
        functools.partial(_sample_mixer_kernel, layer, n_new, len(new_caches)),
        grid=(n_seq // sb,),
        in_specs=[
            tok_spec(IN_WIDTH),
            rope_spec, rope_spec, rope_spec,
            _const_spec((ts, ts)), _const_spec((ts, ts)), _const_spec((4 * ts, sb * HG_DV)),
            layer_spec(WINDOW, KV_WIDTH), layer_spec(WINDOW, KV_WIDTH),
            layer_spec(HG_HEADS, HG_DK, HG_DV),
            pl.BlockSpec(memory_space=pltpu.SMEM),
            _const_spec((DEPTH, HG_WIDTH)),
            _layer_spec(layer, (1, HG_DV)),
        ] + [pl.BlockSpec(memory_space=pl.ANY)] * len(new_caches),
        out_specs=[
            tok_spec(MIX_WIDTH),
            slab_spec(WINDOW, KV_WIDTH), slab_spec(WINDOW, KV_WIDTH),
            slab_spec(HG_HEADS, HG_DK, HG_DV),
        ],
        out_shape=[jax.ShapeDtypeStruct((n_tok, MIX_WIDTH), F32)]
        + [jax.ShapeDtypeStruct(shape, F32) for shape in stacked],
        input_output_aliases=aliases,
        scratch_shapes=[pltpu.VMEM((3, SUBLANES + ts, HG_WIDTH), F32)],
        compiler_params=pltpu.CompilerParams(
            dimension_semantics=("arbitrary",), vmem_limit_bytes=VMEM_LIMIT),
        name=f"sample_mixer{layer}",
    )(proj, *(jnp.tile(t, (sb, 1)) for t in rope), tri, rev, sel, cache_k, cache_v, state, sinks, lb_all,
      onorm, *new_caches)

    y, w_o_bf, w_up_bf, w_dn_bf = pl.pallas_call(
        functools.partial(_sample_out_kernel, last),
        grid=(D_FF // SAMPLE_FF_BLOCK,),
        in_specs=[whole((n_tok, D_MODEL)), whole((n_tok, MIX_WIDTH)), slab((MIX_WIDTH, D_MODEL)),
                  slab((1, D_MODEL)),
                  pl.BlockSpec((None, D_MODEL, SAMPLE_FF_BLOCK), lambda c: (layer, 0, c)),
                  pl.BlockSpec((None, SAMPLE_FF_BLOCK, D_MODEL), lambda c: (layer, c, 0)),
                  whole((1, D_MODEL))],
        out_specs=[pl.BlockSpec((n_tok, D_MODEL), lambda c: (0, 0)),
                   pl.BlockSpec((MIX_WIDTH, D_MODEL), lambda c: (0, 0)),
                   pl.BlockSpec((D_MODEL, SAMPLE_FF_BLOCK), lambda c: (0, c)),
                   pl.BlockSpec((SAMPLE_FF_BLOCK, D_MODEL), lambda c: (c, 0))],
        out_shape=[jax.ShapeDtypeStruct((n_tok, D_MODEL), F32),
                   jax.ShapeDtypeStruct((MIX_WIDTH, D_MODEL), BF16),
                   jax.ShapeDtypeStruct((D_MODEL, D_FF), BF16),
                   jax.ShapeDtypeStruct((D_FF, D_MODEL), BF16)],
        scratch_shapes=[pltpu.VMEM((n_tok, D_MODEL), BF16), pltpu.VMEM((n_tok, D_MODEL), F32)],
        compiler_params=params,
        name=f"sample_out{layer}",
    )(x2, mix, w_o, mnorm, w_up, w_dn, fnorm)
    return y.reshape(n_seq, n_new, D_MODEL), new_caches, (w_in_bf, w_o_bf, w_up_bf, w_dn_bf)


def _rope_tables(pos):
    n = pos.shape[0]
    inv_freq = jnp.power(ROPE_THETA, -jnp.arange(ROT_HALF, dtype=F32) * (2.0 / ROT_DIM))
    ang = pos.astype(F32)[:, None] * inv_freq[None, :]
    cos, sin = jnp.cos(ang), jnp.sin(ang)
    rest = HEAD_DIM - ROT_DIM
    zeros_h = jnp.zeros((n, ROT_HALF), F32)
    cos_t = jnp.concatenate([cos, cos, jnp.ones((n, rest), F32)], axis=1)
    sdn_t = jnp.concatenate([zeros_h, sin, jnp.zeros((n, rest), F32)], axis=1)
    sup_t = jnp.concatenate([-sin, zeros_h, jnp.zeros((n, rest), F32)], axis=1)
    reps = LANES // HEAD_DIM
    return tuple(jnp.tile(t, (1, reps)) for t in (cos_t, sdn_t, sup_t))


def _chunk_tri(n, chunk):
    r = jnp.arange(n)
    same = (r[:, None] // chunk) == (r[None, :] // chunk)
    return (same & (r[None, :] <= r[:, None])).astype(BF16)


def kernel(x_prompt, x_sample, cache_k, cache_v, state_hgrn, attn_norm, w_in, att_sinks, hgrn_lower_bounds,
           hgrn_out_norm, w_o, mlp_norm, w_up, w_down, final_norm):
    batch, seq, _ = x_prompt.shape
    n_seq, n_new, _ = x_sample.shape
    assert seq % PROMPT_BLOCK == 0 and PROMPT_BLOCK % HG_CHUNK == 0

    rope_p = _rope_tables(jnp.arange(seq, dtype=jnp.int32))
    rope_s = _rope_tables(PAST_LEN + jnp.arange(n_new, dtype=jnp.int32))
    tri = _chunk_tri(PROMPT_BLOCK, HG_CHUNK)
    lb_all = hgrn_lower_bounds.astype(F32)
    fnorm = final_norm.reshape(1, D_MODEL)

    ck = cache_k.reshape(DEPTH, n_seq, WINDOW, KV_WIDTH)
    cv = cache_v.reshape(DEPTH, n_seq, WINDOW, KV_WIDTH)
    anorm, onorm, mnorm = (attn_norm.reshape(DEPTH, 1, D_MODEL), hgrn_out_norm.reshape(DEPTH, 1, HG_DV),
                           mlp_norm.reshape(DEPTH, 1, D_MODEL))
    xp, xs = x_prompt, x_sample
    nk_p, nv_p, ns_p, new_caches = [], [], [], []
    for l in range(DEPTH):
        xs, new_caches, (w_in_bf, w_o_bf, w_up_bf, w_dn_bf) = _sample_layer(
            l, xs, rope_s, ck, cv, state_hgrn, new_caches, att_sinks[l], lb_all, anorm, w_in, onorm, w_o, mnorm,
            w_up, w_down, fnorm)
        xp, k1, v1, s1 = _prompt_layer(l, xp, rope_p, tri, att_sinks[l], lb_all, anorm, w_in_bf, onorm, w_o_bf,
                                       mnorm, w_up_bf, w_dn_bf, fnorm)
        nk_p.append(k1), nv_p.append(v1), ns_p.append(s1)
    nk_s, nv_s, ns_s = new_caches

    kv_p = (DEPTH, batch, WINDOW, ATT_KV_HEADS, HEAD_DIM)
    kv_s = (DEPTH, n_seq, WINDOW, ATT_KV_HEADS, HEAD_DIM)
    return (xp, xs,
            jnp.stack(nk_p).reshape(kv_p), jnp.stack(nv_p).reshape(kv_p), jnp.stack(ns_p),
            nk_s.reshape(kv_s), nv_s.reshape(kv_s), ns_s)
```

```python
import functools

import jax
import jax.numpy as jnp
from jax import lax
from jax.experimental import pallas as pl
from jax.experimental.pallas import tpu as pltpu

F32 = jnp.float32
BF16 = jnp.bfloat16

D_MODEL = 1024
DEPTH = 2
PAST_LEN = 16384
ATT_HEADS = 8
ATT_KV_HEADS = 2
HEAD_DIM = 64
GROUP = ATT_HEADS // ATT_KV_HEADS
ATT_WIDTH = ATT_HEADS * HEAD_DIM
KV_WIDTH = ATT_KV_HEADS * HEAD_DIM
WINDOW = 128
ROT_DIM = HEAD_DIM // 4
ROT_HALF = ROT_DIM // 2
ROPE_THETA = 500000.0
HG_HEADS = 4
HG_DK = 128
HG_DV = 128
HG_WIDTH = HG_HEADS * HG_DK
MIX_WIDTH = ATT_WIDTH + HG_WIDTH
IN_WIDTH = ATT_WIDTH + 2 * KV_WIDTH + 4 * HG_WIDTH
D_FF = 4 * D_MODEL
EPS = 1e-6

OFF_Q = 0
OFF_K = ATT_WIDTH
OFF_V = OFF_K + KV_WIDTH
OFF_HQ = OFF_V + KV_WIDTH
OFF_HF = OFF_HQ + HG_WIDTH
OFF_HI = OFF_HF + HG_WIDTH
OFF_HG = OFF_HI + HG_WIDTH

LANES = 128
PROMPT_BLOCK = 256
HG_CHUNK = 128
HG_SUB = 32
HG_GUARD = 80.0
FF_BLOCK = 2048
SUBLANES = 8
SAMPLE_IN_BLOCK = 1408
SAMPLE_FF_BLOCK = 1024
SAMPLE_MIXER_INPUTS = 13
VMEM_LIMIT = 58 * 1024 * 1024


def _dot(a, b):
    return jnp.dot(a, b, preferred_element_type=F32)


def _dot_nt(a, b):
    return lax.dot_general(a, b, (((1,), (1,)), ((), ())), preferred_element_type=F32)


def _rms(x, g_row):
    ms = jnp.mean(x * x, axis=-1, keepdims=True)
    return (x * lax.rsqrt(ms + EPS)) * g_row


def _rope(x, cos, sin_dn, sin_up):
    return x * cos + pltpu.roll(x, ROT_HALF, 1) * sin_dn + pltpu.roll(x, LANES - ROT_HALF, 1) * sin_up


def _lower_bound(lb_all, layer):
    m = jnp.max(lb_all, axis=0, keepdims=True)
    e = jnp.exp(lb_all - m)
    p = e / jnp.sum(e, axis=0, keepdims=True)
    cs = p[0:1]
    for l in range(1, layer + 1):
        cs = cs + p[l:l + 1]
    return jnp.maximum(cs - p[0:1], 0.0)


def _hgrn_gates(z, lb):
    e = jnp.exp(-jnp.abs(z))
    log_sig = jnp.minimum(z, 0.0) - jnp.log1p(e)
    a1 = jnp.log(lb)
    a2 = jnp.log1p(-lb) + log_sig
    log_f = jnp.maximum(a1, a2) + jnp.log1p(jnp.exp(-jnp.abs(a1 - a2)))
    k_in = (1.0 - lb) * (jnp.where(z >= 0.0, e, 1.0) / (1.0 + e))
    return log_f, k_in


def _silu(x):
    return x / (1.0 + jnp.exp(-x))


def _stacked_queries(proj_ref, cos, sdn, sup):
    t = proj_ref.shape[0]
    lo_half = lax.broadcasted_iota(jnp.int32, (t, LANES), 1) < HEAD_DIM
    q_ext = []
    for hp in range(ATT_HEADS // 2):
        q_slab = _rope(proj_ref[:, OFF_Q + hp * LANES:OFF_Q + (hp + 1) * LANES], cos, sdn, sup)
        q_slab = q_slab * (HEAD_DIM ** -0.5)
        q_swap = pltpu.roll(q_slab, HEAD_DIM, 1)
        for sub in range(2):
            kvh = (2 * hp + sub) // GROUP
            src = q_slab if sub == kvh else q_swap
            keep = lo_half if kvh == 0 else ~lo_half
            q_ext.append(jnp.where(keep, src, 0.0).astype(BF16))
    return q_ext


def _softmax_sink(s, allowed, sink):
    s = jnp.where(allowed, s, -jnp.inf)
    m = jnp.maximum(jnp.max(s, axis=-1, keepdims=True), sink)
    p = jnp.exp(s - m)
    inv = 1.0 / (jnp.sum(p, axis=-1, keepdims=True) + jnp.exp(sink - m))
    return p.astype(BF16), inv


def _merge_head_pair(a, b, kvh):
    lo_half = lax.broadcasted_iota(jnp.int32, a.shape, 1) < HEAD_DIM
    if kvh == 0:
        return jnp.where(lo_half, a, pltpu.roll(b, HEAD_DIM, 1))
    return jnp.where(lo_half, pltpu.roll(a, HEAD_DIM, 1), b)


def _cumsum_rows(tri, g):
    g1 = g.astype(BF16)
    r1 = g - g1.astype(F32)
    g2 = r1.astype(BF16)
    g3 = (r1 - g2.astype(F32)).astype(BF16)
    return _dot(tri, g1) + _dot(tri, g2) + _dot(tri, g3)


def _hgrn_intra_operands(q, k, b):
    c = q.shape[0]
    q_parts, k_parts = [], []
    for i in range(c // HG_SUB):
        lo, hi = i * HG_SUB, (i + 1) * HG_SUB
        r = b[lo - 1:lo] if i > 0 else jnp.zeros((1, HG_DK), F32)
        q_parts.append(q[lo:hi] * jnp.exp(b[lo:hi] - r))
        k_parts.append(k[:hi] * jnp.exp(r - b[:hi]))
        if hi < c:
            k_parts.append(jnp.zeros((c - hi, HG_DK), F32))
    return (jnp.concatenate(q_parts, axis=0).astype(BF16),
            jnp.concatenate(k_parts, axis=0).astype(BF16))


def _hgrn_worst_subblock_decay(b_all):
    worst = None
    for lo in range(0, b_all.shape[0], HG_SUB):
        tot = b_all[lo + HG_SUB - 1:lo + HG_SUB]
        if lo % HG_CHUNK:
            tot = tot - b_all[lo - 1:lo]
        worst = tot if worst is None else jnp.minimum(worst, tot)
    return worst


def _hgrn_intra_exact(q, k, v, b, work_ref, out_ref):
    for i, x in enumerate((q, k, v, b)):
        work_ref[i] = x
    s_idx = lax.broadcasted_iota(jnp.int32, (HG_CHUNK, 1), 0)

    def row(t, carry):
        c0 = pl.multiple_of((t // HG_CHUNK) * HG_CHUNK, HG_CHUNK)
        q_t = work_ref[0, pl.ds(t, 1), :]
        b_t = work_ref[3, pl.ds(t, 1), :]
        k_c = work_ref[1, pl.ds(c0, HG_CHUNK), :]
        v_c = work_ref[2, pl.ds(c0, HG_CHUNK), :]
        b_c = work_ref[3, pl.ds(c0, HG_CHUNK), :]
        w = k_c * jnp.exp(jnp.minimum(b_t - b_c, 0.0)) * q_t
        a = jnp.where(s_idx <= t - c0, jnp.sum(w, axis=-1, keepdims=True), 0.0)
        out_ref[pl.ds(t, 1), :] = jnp.sum(a * v_c, axis=0, keepdims=True)
        return carry

    lax.fori_loop(0, q.shape[0], row, 0)
    return out_ref[...]


def _hgrn_intra_scores(a_full):
    c = a_full.shape[0]
    a = jnp.concatenate(
        [a_full[i * HG_SUB:(i + 1) * HG_SUB, i * c:(i + 1) * c] for i in range(c // HG_SUB)], axis=0)
    row = lax.broadcasted_iota(jnp.int32, (c, c), 0)
    col = lax.broadcasted_iota(jnp.int32, (c, c), 1)
    return jnp.where(col <= row, a, 0.0).astype(BF16)


def _out_proj(x, mix_bf16, wo_ref, mnorm_ref):
    x1 = x + _dot(mix_bf16, wo_ref[...])
    return x1, _rms(x1, mnorm_ref[...]).astype(BF16)


def _mlp_cols(acc, h2, w_up_cols, w_dn_rows):
    u = jnp.square(jnp.maximum(_dot(h2, w_up_cols), 0.0)).astype(BF16)
    return acc + _dot(u, w_dn_rows)


def _mlp_block(acc, h2, c, wup_ref, wdn_ref):
    return _mlp_cols(acc, h2, wup_ref[:, c * FF_BLOCK:(c + 1) * FF_BLOCK],
                     wdn_ref[c * FF_BLOCK:(c + 1) * FF_BLOCK, :])


def _prompt_step(with_mixer, with_tail, layer, last, n_tblk, t_blk,
                 x_ref, cos_ref, sdn_ref, sup_ref, tri_ref, sinks_ref, lb_ref, anorm_ref,
                 win_ref, onorm_ref, wo_ref, mnorm_ref, wup_ref, wdn_ref, fnorm_ref,
                 y_ref, nk_ref, nv_ref, ns_ref,
                 proj_ref, mix_ref, xprev_ref, kprev_ref, vprev_ref, st_ref, work_ref, exact_ref):
    tb = PROMPT_BLOCK
    if with_mixer:
        @pl.when(t_blk == 0)
        def _():
            kprev_ref[...] = jnp.zeros_like(kprev_ref)
            vprev_ref[...] = jnp.zeros_like(vprev_ref)
            st_ref[...] = jnp.zeros_like(st_ref)

    if with_tail:
        acc, h2 = _out_proj(xprev_ref[...], mix_ref[...], wo_ref, mnorm_ref)
    if not with_mixer:
        for c in range(D_FF // FF_BLOCK):
            acc = _mlp_block(acc, h2, c, wup_ref, wdn_ref)
        y_ref[0] = _rms(acc, fnorm_ref[...]) if last else acc
        return

    x = x_ref[0]
    xprev_ref[...] = x
    h = _rms(x, anorm_ref[...]).astype(BF16)
    proj_ref[...] = _dot(h, win_ref[...])

    cos, sdn, sup = cos_ref[...], sdn_ref[...], sup_ref[...]

    k_rot = _rope(proj_ref[:, OFF_K:OFF_K + KV_WIDTH], cos, sdn, sup)
    v_new = proj_ref[:, OFF_V:OFF_V + KV_WIDTH]
    k_all = jnp.concatenate([kprev_ref[...], k_rot], axis=0).astype(BF16)
    v_all = jnp.concatenate([vprev_ref[...], v_new], axis=0).astype(BF16)

    q_ext = _stacked_queries(proj_ref, cos, sdn, sup)
    row = lax.broadcasted_iota(jnp.int32, (WINDOW, 2 * WINDOW), 0)
    col = lax.broadcasted_iota(jnp.int32, (WINDOW, 2 * WINDOW), 1)
    in_window = (col >= row) & (col <= row + WINDOW)
    first_lo = jnp.where(t_blk == 0, WINDOW, 0)
    att = [[None] * (tb // WINDOW) for _ in range(ATT_HEADS)]
    for n in range(tb // WINDOW):
        rows_n = slice(n * WINDOW, (n + 1) * WINDOW)
        keys_n = slice(n * WINDOW, (n + 2) * WINDOW)
        q_stack = jnp.concatenate([q[rows_n] for q in q_ext], axis=0)
        s_all = _dot_nt(q_stack, k_all[keys_n])
        allowed = in_window & (col >= first_lo) if n == 0 else in_window
        soft = [_softmax_sink(s_all[head * WINDOW:(head + 1) * WINDOW], allowed, sinks_ref[head])
                for head in range(ATT_HEADS)]
        o_all = _dot(jnp.concatenate([p for p, _ in soft], axis=0), v_all[keys_n])
        for head in range(ATT_HEADS):
            att[head][n] = o_all[head * WINDOW:(head + 1) * WINDOW] * soft[head][1]
    for hp in range(ATT_HEADS // 2):
        a, b = (jnp.concatenate(att[2 * hp + sub], axis=0) for sub in range(2))
        mix_ref[:, hp * LANES:(hp + 1) * LANES] = _merge_head_pair(a, b, (2 * hp) // GROUP).astype(BF16)
    if with_tail:
        acc = _mlp_block(acc, h2, 0, wup_ref, wdn_ref)

    lb = _lower_bound(lb_ref[...], layer)
    log_f, k_in = _hgrn_gates(proj_ref[:, OFF_HF:OFF_HF + HG_WIDTH], lb)
    b_all = _cumsum_rows(tri_ref[...], log_f)
    onorm = onorm_ref[...]
    n_chunks = tb // HG_CHUNK
    units = [(hh, c) for c in range(n_chunks) for hh in range(HG_HEADS)]
    q_in, v_in = [], []
    for hh in range(HG_HEADS):
        q_in.append(_silu(proj_ref[:, OFF_HQ + hh * HG_DK:OFF_HQ + (hh + 1) * HG_DK]))
        v_in.append(proj_ref[:, OFF_HI + hh * HG_DV:OFF_HI + (hh + 1) * HG_DV])

    def piece(x, hh, c, lanes=False):
        x = x[:, hh * HG_DK:(hh + 1) * HG_DK] if lanes else x[hh]
        return x[c * HG_CHUNK:(c + 1) * HG_CHUNK]

    intra_ops = {u: _hgrn_intra_operands(piece(q_in, *u), piece(k_in, *u, lanes=True),
                                         piece(b_all, *u, lanes=True)) for u in units}
    a_full = {u: _dot_nt(*intra_ops[u]) for u in units}
    o_inter, st_new = {}, []
    for hh in range(HG_HEADS):
        st = st_ref[hh]
        for c in range(n_chunks):
            q, k, v, b = (piece(q_in, hh, c), piece(k_in, hh, c, True), piece(v_in, hh, c),
                          piece(b_all, hh, c, True))
            o_inter[(hh, c)] = _dot_nt((q * jnp.exp(b)).astype(BF16), st.astype(BF16))
            b_last = b[HG_CHUNK - 1:HG_CHUNK]
            k_hat = (k * jnp.exp(b_last - b)).astype(BF16)
            st = st * jnp.exp(b_last) + _dot(v.T.astype(BF16), k_hat)
        st_new.append(st)
    a_mask = {u: _hgrn_intra_scores(a_full[u]) for u in units}
    o_intra = {u: _dot(a_mask[u], piece(v_in, *u).astype(BF16)) for u in units}

    def emit_head(hh, o_intra_h):
        o = o_intra_h + jnp.concatenate([o_inter[(hh, c)] for c in range(n_chunks)], axis=0)
        gate = _silu(proj_ref[:, OFF_HG + hh * HG_DV:OFF_HG + (hh + 1) * HG_DV])
        mix_ref[:, ATT_WIDTH + hh * HG_DV:ATT_WIDTH + (hh + 1) * HG_DV] = (
            _rms(o, onorm) * gate).astype(BF16)

    for hh in range(HG_HEADS):
        emit_head(hh, jnp.concatenate([o_intra[(hh, c)] for c in range(n_chunks)], axis=0))
    if with_tail:
        acc = _mlp_block(acc, h2, 1, wup_ref, wdn_ref)
        y_ref[0] = _rms(acc, fnorm_ref[...]) if last else acc

    @pl.when(jnp.min(_hgrn_worst_subblock_decay(b_all)) < -HG_GUARD)
    def _():
        for hh in range(HG_HEADS):
            sl = slice(hh * HG_DK, (hh + 1) * HG_DK)
            emit_head(hh, _hgrn_intra_exact(q_in[hh], k_in[:, sl], v_in[hh], b_all[:, sl],
                                            work_ref, exact_ref))

    kprev_ref[...] = k_rot[tb - WINDOW:]
    vprev_ref[...] = v_new[tb - WINDOW:]
    nk_ref[0] = k_rot[tb - WINDOW:]
    nv_ref[0] = v_new[tb - WINDOW:]
    for hh in range(HG_HEADS):
        st_ref[hh] = st_new[hh]

    @pl.when(t_blk == n_tblk - 1)
    def _():
        for hh in range(HG_HEADS):
            ns_ref[0, hh] = st_new[hh].T


def _prompt_layer_kernel(layer, last, n_tblk, *refs):
    j = pl.program_id(0)
    n_blocks = pl.num_programs(0) - 1
    t_blk = lax.rem(j, n_tblk)

    def run(with_mixer, with_tail):
        _prompt_step(with_mixer, with_tail, layer, last, n_tblk, t_blk, *refs)

    pl.when(j == 0)(functools.partial(run, True, False))
    pl.when((j > 0) & (j < n_blocks))(functools.partial(run, True, True))
    pl.when(j == n_blocks)(functools.partial(run, False, True))


def _const_spec(shape):
    nd = len(shape)
    return pl.BlockSpec(shape, lambda *_: (0,) * nd, pipeline_mode=pl.Buffered(1))


def _layer_spec(layer, shape):
    nd = len(shape)
    return pl.BlockSpec((None,) + shape, lambda *_: (layer,) + (0,) * nd, pipeline_mode=pl.Buffered(1))


def _prompt_layer(layer, x, rope, tri, sinks, lb_all, anorm, w_in, onorm, w_o, mnorm, w_up, w_dn, fnorm):
    batch, seq, _ = x.shape
    tb = PROMPT_BLOCK
    n_tblk = seq // tb
    n_blocks = batch * n_tblk
    last = layer == DEPTH - 1
    mixer_blk = lambda j: jnp.minimum(j, n_blocks - 1)
    tail_blk = lambda j: jnp.maximum(j - 1, 0)
    rope_spec = pl.BlockSpec((tb, LANES), lambda j: (mixer_blk(j) % n_tblk, 0))
    carry_spec = pl.BlockSpec((1, WINDOW, KV_WIDTH), lambda j: (mixer_blk(j) // n_tblk, 0, 0))
    return pl.pallas_call(
        functools.partial(_prompt_layer_kernel, layer, last, n_tblk),
        grid=(n_blocks + 1,),
        in_specs=[
            pl.BlockSpec((1, tb, D_MODEL), lambda j: (mixer_blk(j) // n_tblk, mixer_blk(j) % n_tblk, 0)),
            rope_spec, rope_spec, rope_spec,
            _const_spec((tb, tb)),
            pl.BlockSpec(memory_space=pltpu.SMEM),
            _const_spec((DEPTH, HG_WIDTH)),
            _layer_spec(layer, (1, D_MODEL)),
            _const_spec((D_MODEL, IN_WIDTH)),
            _layer_spec(layer, (1, HG_DV)),
            _const_spec((MIX_WIDTH, D_MODEL)),
            _layer_spec(layer, (1, D_MODEL)),
            _const_spec((D_MODEL, D_FF)),
            _const_spec((D_FF, D_MODEL)),
            _const_spec((1, D_MODEL)),
        ],
        out_specs=[
            pl.BlockSpec((1, tb, D_MODEL), lambda j: (tail_blk(j) // n_tblk, tail_blk(j) % n_tblk, 0)),
            carry_spec, carry_spec,
            pl.BlockSpec((1, HG_HEADS, HG_DK, HG_DV), lambda j: (mixer_blk(j) // n_tblk, 0, 0, 0)),
        ],
        out_shape=[
            jax.ShapeDtypeStruct((batch, seq, D_MODEL), F32),
            jax.ShapeDtypeStruct((batch, WINDOW, KV_WIDTH), F32),
            jax.ShapeDtypeStruct((batch, WINDOW, KV_WIDTH), F32),
            jax.ShapeDtypeStruct((batch, HG_HEADS, HG_DK, HG_DV), F32),
        ],
        scratch_shapes=[
            pltpu.VMEM((tb, IN_WIDTH), F32),
            pltpu.VMEM((tb, MIX_WIDTH), BF16),
            pltpu.VMEM((tb, D_MODEL), F32),
            pltpu.VMEM((WINDOW, KV_WIDTH), F32),
            pltpu.VMEM((WINDOW, KV_WIDTH), F32),
            pltpu.VMEM((HG_HEADS, HG_DV, HG_DK), F32),
            pltpu.VMEM((4, tb, HG_DK), F32),
            pltpu.VMEM((tb, HG_DV), F32),
        ],
        compiler_params=pltpu.CompilerParams(
            dimension_semantics=("arbitrary",), vmem_limit_bytes=VMEM_LIMIT),
        name=f"prompt_layer{layer}",
    )(x, *rope, tri, sinks, lb_all, anorm, w_in, onorm, w_o, mnorm, w_up, w_dn, fnorm)


def _sample_in_kernel(x_ref, anorm_ref, win_ref, proj_ref, win_bf_ref, h_ref):
    @pl.when(pl.program_id(0) == 0)
    def _():
        h_ref[...] = _rms(x_ref[...], anorm_ref[...]).astype(BF16)

    win_bf_ref[...] = win_ref[...].astype(BF16)
    proj_ref[...] = _dot(h_ref[...], win_bf_ref[...])


def _sample_out_kernel(last, x_ref, mix_ref, wo_ref, mnorm_ref, wup_ref, wdn_ref, fnorm_ref,
                       y_ref, wo_bf_ref, wup_bf_ref, wdn_bf_ref, h2_ref, acc_ref):
    c = pl.program_id(0)

    @pl.when(c == 0)
    def _():
        wo_bf_ref[...] = wo_ref[...].astype(BF16)
        acc_ref[...], h2_ref[...] = _out_proj(x_ref[...], mix_ref[...].astype(BF16), wo_bf_ref, mnorm_ref)

    wup_bf_ref[...] = wup_ref[...].astype(BF16)
    wdn_bf_ref[...] = wdn_ref[...].astype(BF16)
    acc_ref[...] = _mlp_cols(acc_ref[...], h2_ref[...], wup_bf_ref[...], wdn_bf_ref[...])

    @pl.when(c == pl.num_programs(0) - 1)
    def _():
        y_ref[...] = _rms(acc_ref[...], fnorm_ref[...]) if last else acc_ref[...]


def _sample_mixer_kernel(layer, n_new, n_aliased, *refs):
    (p_ref, cos_ref, sdn_ref, sup_ref, tri_ref, rev_ref, sel_ref, ck_ref, cv_ref, s0_ref, sinks_ref, lb_ref,
     onorm_ref) = refs[:SAMPLE_MIXER_INPUTS]
    mix_ref, nk_ref, nv_ref, ns_ref, pad_ref = refs[SAMPLE_MIXER_INPUTS + n_aliased:]
    slot = 0 if n_aliased else layer
    for out_ref in (nk_ref, nv_ref, ns_ref):
        for other in range(out_ref.shape[0]):
            if other != slot:
                out_ref[other] = jnp.zeros(out_ref.shape[1:], F32)
    sb = ck_ref.shape[0]
    ts = p_ref.shape[0]
    tok_shift = n_new.bit_length() - 1
    key_shift = WINDOW.bit_length() - 1
    n_cache = sb * WINDOW
    cos, sdn, sup = cos_ref[...], sdn_ref[...], sup_ref[...]

    k_new = _rope(p_ref[:, OFF_K:OFF_K + KV_WIDTH], cos, sdn, sup)
    v_new = p_ref[:, OFF_V:OFF_V + KV_WIDTH]
    pad_keys = jnp.zeros((LANES - ts, KV_WIDTH), F32)
    k_all = jnp.concatenate([ck_ref[...].reshape(n_cache, KV_WIDTH), k_new, pad_keys], axis=0).astype(BF16)
    v_all = jnp.concatenate([cv_ref[...].reshape(n_cache, KV_WIDTH), v_new, pad_keys], axis=0).astype(BF16)
    n_keys = n_cache + LANES
    row = lax.broadcasted_iota(jnp.int32, (ts, n_keys), 0)
    col = lax.broadcasted_iota(jnp.int32, (ts, n_keys), 1)
    new_idx = col - n_cache
    cached_ok = ((col >> key_shift) == (row >> tok_shift)) & ((col & (WINDOW - 1)) >= (row & (n_new - 1)))
    new_ok = ((new_idx < ts) & ((new_idx >> tok_shift) == (row >> tok_shift))
              & ((new_idx & (n_new - 1)) <= (row & (n_new - 1))))
    allowed = ((col < n_cache) & cached_ok) | ((col >= n_cache) & new_ok)

    q_ext = _stacked_queries(p_ref, cos, sdn, sup)
    s_all = _dot_nt(jnp.concatenate(q_ext, axis=0), k_all)
    soft = [_softmax_sink(s_all[head * ts:(head + 1) * ts], allowed, sinks_ref[head])
            for head in range(ATT_HEADS)]
    o_all = _dot(jnp.concatenate([p for p, _ in soft], axis=0), v_all)
    att = [o_all[head * ts:(head + 1) * ts] * soft[head][1] for head in range(ATT_HEADS)]
    for hp in range(ATT_HEADS // 2):
        mix_ref[:, hp * LANES:(hp + 1) * LANES] = _merge_head_pair(
            att[2 * hp], att[2 * hp + 1], (2 * hp) // GROUP)
    for s in range(sb):
        nk_ref[slot, s, 0:WINDOW - n_new] = ck_ref[s, n_new:WINDOW]
        nv_ref[slot, s, 0:WINDOW - n_new] = cv_ref[s, n_new:WINDOW]
        nk_ref[slot, s, WINDOW - n_new:WINDOW] = k_new[s * n_new:(s + 1) * n_new]
        nv_ref[slot, s, WINDOW - n_new:WINDOW] = v_new[s * n_new:(s + 1) * n_new]

    lb = _lower_bound(lb_ref[...], layer)
    log_f, k_in = _hgrn_gates(p_ref[:, OFF_HF:OFF_HF + HG_WIDTH], lb)
    b = _cumsum_rows(tri_ref[...], log_f)
    later = _cumsum_rows(rev_ref[...], log_f)
    q_in = _silu(p_ref[:, OFF_HQ:OFF_HQ + HG_WIDTH])
    v_in = p_ref[:, OFF_HI:OFF_HI + HG_WIDTH]
    gate = _silu(p_ref[:, OFF_HG:OFF_HG + HG_WIDTH])
    onorm = onorm_ref[...]
    heads = [slice(hh * HG_DK, (hh + 1) * HG_DK) for hh in range(HG_HEADS)]

    pad_ref[...] = jnp.zeros_like(pad_ref)
    for i, x in enumerate((k_in, b, v_in)):
        pad_ref[i, SUBLANES:SUBLANES + ts] = x
    tok = lax.broadcasted_iota(jnp.int32, (ts, 1), 0) & (n_new - 1)
    o_intra = [jnp.zeros((ts, HG_DV), F32) for _ in heads]
    for d in range(n_new):
        k_d, b_d, v_d = (pad_ref[i, SUBLANES - d:SUBLANES - d + ts] for i in range(3))
        w = q_in * k_d * jnp.exp(jnp.minimum(b - b_d, 0.0))
        for hh, sl in enumerate(heads):
            a_col = jnp.sum(w[:, sl], axis=-1, keepdims=True)
            o_intra[hh] = o_intra[hh] + jnp.where(tok >= d, a_col, 0.0) * v_d[:, sl]

    q_dec = (q_in * jnp.exp(b)).astype(BF16)
    seq_of_row = lax.broadcasted_iota(jnp.int32, (ts, 1), 0) >> tok_shift
    for hh, sl in enumerate(heads):
        o = o_intra[hh]
        for s in range(sb):
            o_s = _dot(q_dec[:, sl], s0_ref[s, hh].astype(BF16))
            o = o + jnp.where(seq_of_row == s, o_s, 0.0)
        mix_ref[:, ATT_WIDTH + hh * HG_DV:ATT_WIDTH + (hh + 1) * HG_DV] = _rms(o, onorm) * gate[:, sl]

    k_hat = k_in * jnp.exp(later)
    e_b = jnp.exp(b)
    e1 = e_b.astype(BF16).astype(F32)
    e2 = (e_b - e1).astype(BF16).astype(F32)
    e3 = e_b - e1 - e2
    own = ((lax.broadcasted_iota(jnp.int32, (ts, sb * HG_DV), 1) >> key_shift)
           == (lax.broadcasted_iota(jnp.int32, (ts, sb * HG_DV), 0) >> tok_shift))
    no_v = jnp.zeros((3 * ts, sb * HG_DV), F32)
    for hh, sl in enumerate(heads):
        lhs = jnp.concatenate([k_hat[:, sl], e1[:, sl], e2[:, sl], e3[:, sl]], axis=0).T.astype(BF16)
        v_rep = jnp.concatenate([v_in[:, sl]] * sb, axis=1)
        v_diag = jnp.concatenate([jnp.where(own, v_rep, 0.0), no_v], axis=0).astype(BF16)
        upd = _dot(lhs, v_diag)
        decay = _dot(lhs, sel_ref[...])
        for s in range(sb):
            cols = slice(s * HG_DV, (s + 1) * HG_DV)
            ns_ref[slot, s, hh] = decay[:, cols] * s0_ref[s, hh] + upd[:, cols]


def _sample_layer(layer, x, rope, cache_k, cache_v, state, new_caches, sinks, lb_all, anorm, w_in, onorm, w_o,
                  mnorm, w_up, w_dn, fnorm):
    n_seq, n_new, _ = x.shape
    n_tok = n_seq * n_new
    ts = LANES // 4
    sb = ts // n_new
    assert sb * n_new == ts and n_seq % sb == 0 and n_new <= SUBLANES
    assert n_new & (n_new - 1) == 0 and WINDOW & (WINDOW - 1) == 0 and HG_DV == WINDOW
    last = layer == DEPTH - 1
    x2 = x.reshape(n_tok, D_MODEL)
    params = pltpu.CompilerParams(vmem_limit_bytes=VMEM_LIMIT)
    whole = _const_spec
    slab = functools.partial(_layer_spec, layer)

    proj, w_in_bf = pl.pallas_call(
        _sample_in_kernel,
        grid=(IN_WIDTH // SAMPLE_IN_BLOCK,),
        in_specs=[whole((n_tok, D_MODEL)), slab((1, D_MODEL)),
                  pl.BlockSpec((None, D_MODEL, SAMPLE_IN_BLOCK), lambda c: (layer, 0, c))],
        out_specs=[pl.BlockSpec((n_tok, SAMPLE_IN_BLOCK), lambda c: (0, c)),
                   pl.BlockSpec((D_MODEL, SAMPLE_IN_BLOCK), lambda c: (0, c))],
        out_shape=[jax.ShapeDtypeStruct((n_tok, IN_WIDTH), F32),
                   jax.ShapeDtypeStruct((D_MODEL, IN_WIDTH), BF16)],
        scratch_shapes=[pltpu.VMEM((n_tok, D_MODEL), BF16)],
        compiler_params=params,
        name=f"sample_in{layer}",
    )(x2, anorm, w_in)

    r = jnp.arange(ts)
    same_seq = (r[:, None] // n_new) == (r[None, :] // n_new)
    tri = (same_seq & (r[None, :] <= r[:, None])).astype(BF16)
    rev = (same_seq & (r[None, :] > r[:, None])).astype(BF16)
    c = jnp.arange(sb * HG_DV)
    is_last = ((r[:, None] % n_new) == n_new - 1) & ((r[:, None] // n_new) == (c[None, :] // HG_DV))
    sel = jnp.concatenate([jnp.zeros_like(is_last)] + [is_last] * 3, axis=0).astype(BF16)

    tok_spec = lambda width: pl.BlockSpec((ts, width), lambda i: (i, 0))
    seq_spec = lambda *tail: pl.BlockSpec((sb,) + tail, lambda i: (i,) + (0,) * len(tail))
    layer_spec = lambda *tail: pl.BlockSpec((None, sb) + tail, lambda i: (layer, i) + (0,) * len(tail))
    rope_spec = _const_spec((ts, LANES))
    first = not new_caches
    slab_spec = lambda *tail: pl.BlockSpec(
        (DEPTH if first else 1, sb) + tail, lambda i: (0 if first else layer, i) + (0,) * len(tail))
    n_in = SAMPLE_MIXER_INPUTS
    aliases = {n_in + i: 1 + i for i in range(len(new_caches))}
    stacked = [(DEPTH, n_seq, WINDOW, KV_WIDTH)] * 2 + [(DEPTH, n_seq, HG_HEADS, HG_DK, HG_DV)]
    mix, *new_caches = pl.pallas_call(
        functools.partial(_sample_mixer_kernel, layer, n_new, len(new_caches)),
        grid=(n_seq // sb,),
        in_specs=[
            tok_spec(IN_WIDTH),
            rope_spec, rope_spec, rope_spec,
            _const_spec((ts, ts)), _const_spec((ts, ts)), _const_spec((4 * ts, sb * HG_DV)),
            layer_spec(WINDOW, KV_WIDTH), layer_spec(WINDOW, KV_WIDTH),
            layer_spec(HG_HEADS, HG_DK, HG_DV),
            pl.BlockSpec(memory_space=pltpu.SMEM),
            _const_spec((DEPTH, HG_WIDTH)),
            _layer_spec(layer, (1, HG_DV)),
        ] + [pl.BlockSpec(memory_space=pl.ANY)] * len(new_caches),
        out_specs=[
            tok_spec(MIX_WIDTH),
            slab_spec(WINDOW, KV_WIDTH), slab_spec(WINDOW, KV_WIDTH),
            slab_spec(HG_HEADS, HG_DK, HG_DV),
        ],
        out_shape=[jax.ShapeDtypeStruct((n_tok, MIX_WIDTH), F32)]
        + [jax.ShapeDtypeStruct(shape, F32) for shape in stacked],
        input_output_aliases=aliases,
        scratch_shapes=[pltpu.VMEM((3, SUBLANES + ts, HG_WIDTH), F32)],
        compiler_params=pltpu.CompilerParams(
            dimension_semantics=("arbitrary",), vmem_limit_bytes=VMEM_LIMIT),
        name=f"sample_mixer{layer}",
    )(proj, *(jnp.tile(t, (sb, 1)) for t in rope), tri, rev, sel, cache_k, cache_v, state, sinks, lb_all,
      onorm, *new_caches)

    y, w_o_bf, w_up_bf, w_dn_bf = pl.pallas_call(
        functools.partial(_sample_out_kernel, last),
        grid=(D_FF // SAMPLE_FF_BLOCK,),
        in_specs=[whole((n_tok, D_MODEL)), whole((n_tok, MIX_WIDTH)), slab((MIX_WIDTH, D_MODEL)),
                  slab((1, D_MODEL)),
                  pl.BlockSpec((None, D_MODEL, SAMPLE_FF_BLOCK), lambda c: (layer, 0, c)),
                  pl.BlockSpec((None, SAMPLE_FF_BLOCK, D_MODEL), lambda c: (layer, c, 0)),
                  whole((1, D_MODEL))],
        out_specs=[pl.BlockSpec((n_tok, D_MODEL), lambda c: (0, 0)),
                   pl.BlockSpec((MIX_WIDTH, D_MODEL), lambda c: (0, 0)),
                   pl.BlockSpec((D_MODEL, SAMPLE_FF_BLOCK), lambda c: (0, c)),
                   pl.BlockSpec((SAMPLE_FF_BLOCK, D_MODEL), lambda c: (c, 0))],
        out_shape=[jax.ShapeDtypeStruct((n_tok, D_MODEL), F32),
                   jax.ShapeDtypeStruct((MIX_WIDTH, D_MODEL), BF16),
                   jax.ShapeDtypeStruct((D_MODEL, D_FF), BF16),
                   jax.ShapeDtypeStruct((D_FF, D_MODEL), BF16)],
        scratch_shapes=[pltpu.VMEM((n_tok, D_MODEL), BF16), pltpu.VMEM((n_tok, D_MODEL), F32)],
        compiler_params=params,
        name=f"sample_out{layer}",
    )(x2, mix, w_o, mnorm, w_up, w_dn, fnorm)
    return y.reshape(n_seq, n_new, D_MODEL), new_caches, (w_in_bf, w_o_bf, w_up_bf, w_dn_bf)


def _rope_tables(pos):
    n = pos.shape[0]
    inv_freq = jnp.power(ROPE_THETA, -jnp.arange(ROT_HALF, dtype=F32) * (2.0 / ROT_DIM))
    ang = pos.astype(F32)[:, None] * inv_freq[None, :]
    cos, sin = jnp.cos(ang), jnp.sin(ang)
    rest = HEAD_DIM - ROT_DIM
    zeros_h = jnp.zeros((n, ROT_HALF), F32)
    cos_t = jnp.concatenate([cos, cos, jnp.ones((n, rest), F32)], axis=1)
    sdn_t = jnp.concatenate([zeros_h, sin, jnp.zeros((n, rest), F32)], axis=1)
    sup_t = jnp.concatenate([-sin, zeros_h, jnp.zeros((n, rest), F32)], axis=1)
    reps = LANES // HEAD_DIM
    return tuple(jnp.tile(t, (1, reps)) for t in (cos_t, sdn_t, sup_t))


def _chunk_tri(n, chunk):
    r = jnp.arange(n)
    same = (r[:, None] // chunk) == (r[None, :] // chunk)
    return (same & (r[None, :] <= r[:, None])).astype(BF16)


def kernel(x_prompt, x_sample, cache_k, cache_v, state_hgrn, attn_norm, w_in, att_sinks, hgrn_lower_bounds,
           hgrn_out_norm, w_o, mlp_norm, w_up, w_down, final_norm):
    batch, seq, _ = x_prompt.shape
    n_seq, n_new, _ = x_sample.shape
    assert seq % PROMPT_BLOCK == 0 and PROMPT_BLOCK % HG_CHUNK == 0

    rope_p = _rope_tables(jnp.arange(seq, dtype=jnp.int32))
    rope_s = _rope_tables(PAST_LEN + jnp.arange(n_new, dtype=jnp.int32))
    tri = _chunk_tri(PROMPT_BLOCK, HG_CHUNK)
    lb_all = hgrn_lower_bounds.astype(F32)
    fnorm = final_norm.reshape(1, D_MODEL)

    ck = cache_k.reshape(DEPTH, n_seq, WINDOW, KV_WIDTH)
    cv = cache_v.reshape(DEPTH, n_seq, WINDOW, KV_WIDTH)
    anorm, onorm, mnorm = (attn_norm.reshape(DEPTH, 1, D_MODEL), hgrn_out_norm.reshape(DEPTH, 1, HG_DV),
                           mlp_norm.reshape(DEPTH, 1, D_MODEL))
    xp, xs = x_prompt, x_sample
    nk_p, nv_p, ns_p, new_caches = [], [], [], []
    for l in range(DEPTH):
        xs, new_caches, (w_in_bf, w_o_bf, w_up_bf, w_dn_bf) = _sample_layer(
            l, xs, rope_s, ck, cv, state_hgrn, new_caches, att_sinks[l], lb_all, anorm, w_in, onorm, w_o, mnorm,
            w_up, w_down, fnorm)
        xp, k1, v1, s1 = _prompt_layer(l, xp, rope_p, tri, att_sinks[l], lb_all, anorm, w_in_bf, onorm, w_o_bf,
                                       mnorm, w_up_bf, w_dn_bf, fnorm)
        nk_p.append(k1), nv_p.append(v1), ns_p.append(s1)
    nk_s, nv_s, ns_s = new_caches

    kv_p = (DEPTH, batch, WINDOW, ATT_KV_HEADS, HEAD_DIM)
    kv_s = (DEPTH, n_seq, WINDOW, ATT_KV_HEADS, HEAD_DIM)
    return (xp, xs,
            jnp.stack(nk_p).reshape(kv_p), jnp.stack(nv_p).reshape(kv_p), jnp.stack(ns_p),
            nk_s.reshape(kv_s), nv_s.reshape(kv_s), ns_s)
```

```python
import functools

import jax
import jax.numpy as jnp
from jax import lax
from jax.experimental import pallas as pl
from jax.experimental.pallas import tpu as pltpu

F32 = jnp.float32
BF16 = jnp.bfloat16

D_MODEL = 1024
DEPTH = 2
PAST_LEN = 16384
ATT_HEADS = 8
ATT_KV_HEADS = 2
HEAD_DIM = 64
GROUP = ATT_HEADS // ATT_KV_HEADS
ATT_WIDTH = ATT_HEADS * HEAD_DIM
KV_WIDTH = ATT_KV_HEADS * HEAD_DIM
WINDOW = 128
ROT_DIM = HEAD_DIM // 4
ROT_HALF = ROT_DIM // 2
ROPE_THETA = 500000.0
HG_HEADS = 4
HG_DK = 128
HG_DV = 128
HG_WIDTH = HG_HEADS * HG_DK
MIX_WIDTH = ATT_WIDTH + HG_WIDTH
IN_WIDTH = ATT_WIDTH + 2 * KV_WIDTH + 4 * HG_WIDTH
D_FF = 4 * D_MODEL
EPS = 1e-6

OFF_Q = 0
OFF_K = ATT_WIDTH
OFF_V = OFF_K + KV_WIDTH
OFF_HQ = OFF_V + KV_WIDTH
OFF_HF = OFF_HQ + HG_WIDTH
OFF_HI = OFF_HF + HG_WIDTH
OFF_HG = OFF_HI + HG_WIDTH

LANES = 128
PROMPT_BLOCK = 256
HG_CHUNK = 128
HG_SUB = 32
HG_GUARD = 80.0
FF_BLOCK = 2048
SUBLANES = 8
SAMPLE_IN_BLOCK = 1408
SAMPLE_FF_BLOCK = 1024
SAMPLE_MIXER_INPUTS = 13
VMEM_LIMIT = 58 * 1024 * 1024


def _dot(a, b):
    return jnp.dot(a, b, preferred_element_type=F32)


def _dot_nt(a, b):
    return lax.dot_general(a, b, (((1,), (1,)), ((), ())), preferred_element_type=F32)


def _rms(x, g_row):
    ms = jnp.mean(x * x, axis=-1, keepdims=True)
    return (x * lax.rsqrt(ms + EPS)) * g_row


def _rope(x, cos, sin_dn, sin_up):
    return x * cos + pltpu.roll(x, ROT_HALF, 1) * sin_dn + pltpu.roll(x, LANES - ROT_HALF, 1) * sin_up


def _lower_bound(lb_all, layer):
    m = jnp.max(lb_all, axis=0, keepdims=True)
    e = jnp.exp(lb_all - m)
    p = e / jnp.sum(e, axis=0, keepdims=True)
    cs = p[0:1]
    for l in range(1, layer + 1):
        cs = cs + p[l:l + 1]
    return jnp.maximum(cs - p[0:1], 0.0)


def _hgrn_gates(z, lb):
    e = jnp.exp(-jnp.abs(z))
    log_sig = jnp.minimum(z, 0.0) - jnp.log1p(e)
    a1 = jnp.log(lb)
    a2 = jnp.log1p(-lb) + log_sig
    log_f = jnp.maximum(a1, a2) + jnp.log1p(jnp.exp(-jnp.abs(a1 - a2)))
    k_in = (1.0 - lb) * (jnp.where(z >= 0.0, e, 1.0) / (1.0 + e))
    return log_f, k_in


def _silu(x):
    return x / (1.0 + jnp.exp(-x))


def _stacked_queries(proj_ref, cos, sdn, sup):
    t = proj_ref.shape[0]
    lo_half = lax.broadcasted_iota(jnp.int32, (t, LANES), 1) < HEAD_DIM
    q_ext = []
    for hp in range(ATT_HEADS // 2):
        q_slab = _rope(proj_ref[:, OFF_Q + hp * LANES:OFF_Q + (hp + 1) * LANES], cos, sdn, sup)
        q_slab = q_slab * (HEAD_DIM ** -0.5)
        q_swap = pltpu.roll(q_slab, HEAD_DIM, 1)
        for sub in range(2):
            kvh = (2 * hp + sub) // GROUP
            src = q_slab if sub == kvh else q_swap
            keep = lo_half if kvh == 0 else ~lo_half
            q_ext.append(jnp.where(keep, src, 0.0).astype(BF16))
    return q_ext


def _softmax_sink(s, allowed, sink):
    s = jnp.where(allowed, s, -jnp.inf)
    m = jnp.maximum(jnp.max(s, axis=-1, keepdims=True), sink)
    p = jnp.exp(s - m)
    inv = 1.0 / (jnp.sum(p, axis=-1, keepdims=True) + jnp.exp(sink - m))
    return p.astype(BF16), inv


def _merge_head_pair(a, b, kvh):
    lo_half = lax.broadcasted_iota(jnp.int32, a.shape, 1) < HEAD_DIM
    if kvh == 0:
        return jnp.where(lo_half, a, pltpu.roll(b, HEAD_DIM, 1))
    return jnp.where(lo_half, pltpu.roll(a, HEAD_DIM, 1), b)


def _cumsum_rows(tri, g):
    g1 = g.astype(BF16)
    r1 = g - g1.astype(F32)
    g2 = r1.astype(BF16)
    g3 = (r1 - g2.astype(F32)).astype(BF16)
    return _dot(tri, g1) + _dot(tri, g2) + _dot(tri, g3)


def _hgrn_intra_operands(q, k, b):
    c = q.shape[0]
    pairs = []
    for i in range(c // HG_SUB):
        lo, hi = i * HG_SUB, (i + 1) * HG_SUB
        r = b[lo - 1:lo] if i > 0 else jnp.zeros((1, HG_DK), F32)
        pairs.append(((q[lo:hi] * jnp.exp(b[lo:hi] - r)).astype(BF16),
                      (k[:hi] * jnp.exp(r - b[:hi])).astype(BF16)))
    return pairs


def _hgrn_worst_subblock_decay(b_all):
    worst = None
    for lo in range(0, b_all.shape[0], HG_SUB):
        tot = b_all[lo + HG_SUB - 1:lo + HG_SUB]
        if lo % HG_CHUNK:
            tot = tot - b_all[lo - 1:lo]
        worst = tot if worst is None else jnp.minimum(worst, tot)
    return worst


def _hgrn_intra_exact(q, k, v, b, work_ref, out_ref):
    for i, x in enumerate((q, k, v, b)):
        work_ref[i] = x
    s_idx = lax.broadcasted_iota(jnp.int32, (HG_CHUNK, 1), 0)

    def row(t, carry):
        c0 = pl.multiple_of((t // HG_CHUNK) * HG_CHUNK, HG_CHUNK)
        q_t = work_ref[0, pl.ds(t, 1), :]
        b_t = work_ref[3, pl.ds(t, 1), :]
        k_c = work_ref[1, pl.ds(c0, HG_CHUNK), :]
        v_c = work_ref[2, pl.ds(c0, HG_CHUNK), :]
        b_c = work_ref[3, pl.ds(c0, HG_CHUNK), :]
        w = k_c * jnp.exp(jnp.minimum(b_t - b_c, 0.0)) * q_t
        a = jnp.where(s_idx <= t - c0, jnp.sum(w, axis=-1, keepdims=True), 0.0)
        out_ref[pl.ds(t, 1), :] = jnp.sum(a * v_c, axis=0, keepdims=True)
        return carry

    lax.fori_loop(0, q.shape[0], row, 0)
    return out_ref[...]


def _hgrn_intra_scores(a_parts):
    c = HG_SUB * len(a_parts)
    rows = [a if a.shape[1] == c else jnp.concatenate([a, jnp.zeros((HG_SUB, c - a.shape[1]), F32)], axis=1)
            for a in a_parts]
    a = jnp.concatenate(rows, axis=0)
    row = lax.broadcasted_iota(jnp.int32, (c, c), 0)
    col = lax.broadcasted_iota(jnp.int32, (c, c), 1)
    return jnp.where(col <= row, a, 0.0).astype(BF16)


def _out_proj(x, mix_bf16, wo_ref, mnorm_ref):
    x1 = x + _dot(mix_bf16, wo_ref[...])
    return x1, _rms(x1, mnorm_ref[...]).astype(BF16)


def _mlp_cols(acc, h2, w_up_cols, w_dn_rows):
    u = jnp.square(jnp.maximum(_dot(h2, w_up_cols), 0.0)).astype(BF16)
    return acc + _dot(u, w_dn_rows)


def _mlp_block(acc, h2, c, wup_ref, wdn_ref):
    return _mlp_cols(acc, h2, wup_ref[:, c * FF_BLOCK:(c + 1) * FF_BLOCK],
                     wdn_ref[c * FF_BLOCK:(c + 1) * FF_BLOCK, :])


def _prompt_step(with_mixer, with_tail, layer, last, n_tblk, t_blk,
                 x_ref, cos_ref, sdn_ref, sup_ref, tri_ref, sinks_ref, lb_ref, anorm_ref,
                 win_ref, onorm_ref, wo_ref, mnorm_ref, wup_ref, wdn_ref, fnorm_ref,
                 y_ref, nk_ref, nv_ref, ns_ref,
                 proj_ref, mix_ref, xprev_ref, kprev_ref, vprev_ref, st_ref, work_ref, exact_ref):
    tb = PROMPT_BLOCK
    if with_mixer:
        @pl.when(t_blk == 0)
        def _():
            kprev_ref[...] = jnp.zeros_like(kprev_ref)
            vprev_ref[...] = jnp.zeros_like(vprev_ref)
            st_ref[...] = jnp.zeros_like(st_ref)

    if with_tail:
        acc, h2 = _out_proj(xprev_ref[...], mix_ref[...], wo_ref, mnorm_ref)
    if not with_mixer:
        for c in range(D_FF // FF_BLOCK):
            acc = _mlp_block(acc, h2, c, wup_ref, wdn_ref)
        y_ref[0] = _rms(acc, fnorm_ref[...]) if last else acc
        return

    x = x_ref[0]
    xprev_ref[...] = x
    h = _rms(x, anorm_ref[...]).astype(BF16)
    proj_ref[...] = _dot(h, win_ref[...])

    cos, sdn, sup = cos_ref[...], sdn_ref[...], sup_ref[...]

    k_rot = _rope(proj_ref[:, OFF_K:OFF_K + KV_WIDTH], cos, sdn, sup)
    v_new = proj_ref[:, OFF_V:OFF_V + KV_WIDTH]
    k_all = jnp.concatenate([kprev_ref[...], k_rot], axis=0).astype(BF16)
    v_all = jnp.concatenate([vprev_ref[...], v_new], axis=0).astype(BF16)

    q_ext = _stacked_queries(proj_ref, cos, sdn, sup)
    row = lax.broadcasted_iota(jnp.int32, (WINDOW, 2 * WINDOW), 0)
    col = lax.broadcasted_iota(jnp.int32, (WINDOW, 2 * WINDOW), 1)
    in_window = (col >= row) & (col <= row + WINDOW)
    first_lo = jnp.where(t_blk == 0, WINDOW, 0)
    att = [[None] * (tb // WINDOW) for _ in range(ATT_HEADS)]
    for n in range(tb // WINDOW):
        rows_n = slice(n * WINDOW, (n + 1) * WINDOW)
        keys_n = slice(n * WINDOW, (n + 2) * WINDOW)
        q_stack = jnp.concatenate([q[rows_n] for q in q_ext], axis=0)
        s_all = _dot_nt(q_stack, k_all[keys_n])
        allowed = in_window & (col >= first_lo) if n == 0 else in_window
        soft = [_softmax_sink(s_all[head * WINDOW:(head + 1) * WINDOW], allowed, sinks_ref[head])
                for head in range(ATT_HEADS)]
        o_all = _dot(jnp.concatenate([p for p, _ in soft], axis=0), v_all[keys_n])
        for head in range(ATT_HEADS):
            att[head][n] = o_all[head * WINDOW:(head + 1) * WINDOW] * soft[head][1]
    for hp in range(ATT_HEADS // 2):
        a, b = (jnp.concatenate(att[2 * hp + sub], axis=0) for sub in range(2))
        mix_ref[:, hp * LANES:(hp + 1) * LANES] = _merge_head_pair(a, b, (2 * hp) // GROUP).astype(BF16)
    if with_tail:
        acc = _mlp_block(acc, h2, 0, wup_ref, wdn_ref)

    lb = _lower_bound(lb_ref[...], layer)
    log_f, k_in = _hgrn_gates(proj_ref[:, OFF_HF:OFF_HF + HG_WIDTH], lb)
    b_all = _cumsum_rows(tri_ref[...], log_f)
    onorm = onorm_ref[...]
    n_chunks = tb // HG_CHUNK
    units = [(hh, c) for c in range(n_chunks) for hh in range(HG_HEADS)]
    q_in, v_in = [], []
    for hh in range(HG_HEADS):
        q_in.append(_silu(proj_ref[:, OFF_HQ + hh * HG_DK:OFF_HQ + (hh + 1) * HG_DK]))
        v_in.append(proj_ref[:, OFF_HI + hh * HG_DV:OFF_HI + (hh + 1) * HG_DV])

    def piece(x, hh, c, lanes=False):
        x = x[:, hh * HG_DK:(hh + 1) * HG_DK] if lanes else x[hh]
        return x[c * HG_CHUNK:(c + 1) * HG_CHUNK]

    intra_ops = {u: _hgrn_intra_operands(piece(q_in, *u), piece(k_in, *u, lanes=True),
                                         piece(b_all, *u, lanes=True)) for u in units}
    a_full = {u: [_dot_nt(qh, kh) for qh, kh in intra_ops[u]] for u in units}
    o_inter, st_new = {}, []
    for hh in range(HG_HEADS):
        st = st_ref[hh]
        for c in range(n_chunks):
            q, k, v, b = (piece(q_in, hh, c), piece(k_in, hh, c, True), piece(v_in, hh, c),
                          piece(b_all, hh, c, True))
            o_inter[(hh, c)] = _dot_nt((q * jnp.exp(b)).astype(BF16), st.astype(BF16))
            b_last = b[HG_CHUNK - 1:HG_CHUNK]
            k_hat = (k * jnp.exp(b_last - b)).astype(BF16)
            st = st * jnp.exp(b_last) + _dot(v.T.astype(BF16), k_hat)
        st_new.append(st)
    a_mask = {u: _hgrn_intra_scores(a_full[u]) for u in units}
    o_intra = {u: _dot(a_mask[u], piece(v_in, *u).astype(BF16)) for u in units}

    def emit_head(hh, o_intra_h):
        o = o_intra_h + jnp.concatenate([o_inter[(hh, c)] for c in range(n_chunks)], axis=0)
        gate = _silu(proj_ref[:, OFF_HG + hh * HG_DV:OFF_HG + (hh + 1) * HG_DV])
        mix_ref[:, ATT_WIDTH + hh * HG_DV:ATT_WIDTH + (hh + 1) * HG_DV] = (
            _rms(o, onorm) * gate).astype(BF16)

    for hh in range(HG_HEADS):
        emit_head(hh, jnp.concatenate([o_intra[(hh, c)] for c in range(n_chunks)], axis=0))
    if with_tail:
        acc = _mlp_block(acc, h2, 1, wup_ref, wdn_ref)
        y_ref[0] = _rms(acc, fnorm_ref[...]) if last else acc

    @pl.when(jnp.min(_hgrn_worst_subblock_decay(b_all)) < -HG_GUARD)
    def _():
        for hh in range(HG_HEADS):
            sl = slice(hh * HG_DK, (hh + 1) * HG_DK)
            emit_head(hh, _hgrn_intra_exact(q_in[hh], k_in[:, sl], v_in[hh], b_all[:, sl],
                                            work_ref, exact_ref))

    kprev_ref[...] = k_rot[tb - WINDOW:]
    vprev_ref[...] = v_new[tb - WINDOW:]
    nk_ref[0] = k_rot[tb - WINDOW:]
    nv_ref[0] = v_new[tb - WINDOW:]
    for hh in range(HG_HEADS):
        st_ref[hh] = st_new[hh]

    @pl.when(t_blk == n_tblk - 1)
    def _():
        for hh in range(HG_HEADS):
            ns_ref[0, hh] = st_new[hh].T


def _prompt_layer_kernel(layer, last, n_tblk, *refs):
    j = pl.program_id(0)
    n_blocks = pl.num_programs(0) - 1
    t_blk = lax.rem(j, n_tblk)

    def run(with_mixer, with_tail):
        _prompt_step(with_mixer, with_tail, layer, last, n_tblk, t_blk, *refs)

    pl.when(j == 0)(functools.partial(run, True, False))
    pl.when((j > 0) & (j < n_blocks))(functools.partial(run, True, True))
    pl.when(j == n_blocks)(functools.partial(run, False, True))


def _const_spec(shape):
    nd = len(shape)
    return pl.BlockSpec(shape, lambda *_: (0,) * nd, pipeline_mode=pl.Buffered(1))


def _layer_spec(layer, shape):
    nd = len(shape)
    return pl.BlockSpec((None,) + shape, lambda *_: (layer,) + (0,) * nd, pipeline_mode=pl.Buffered(1))


def _prompt_layer(layer, x, rope, tri, sinks, lb_all, anorm, w_in, onorm, w_o, mnorm, w_up, w_dn, fnorm):
    batch, seq, _ = x.shape
    tb = PROMPT_BLOCK
    n_tblk = seq // tb
    n_blocks = batch * n_tblk
    last = layer == DEPTH - 1
    mixer_blk = lambda j: jnp.minimum(j, n_blocks - 1)
    tail_blk = lambda j: jnp.maximum(j - 1, 0)
    rope_spec = pl.BlockSpec((tb, LANES), lambda j: (mixer_blk(j) % n_tblk, 0))
    carry_spec = pl.BlockSpec((1, WINDOW, KV_WIDTH), lambda j: (mixer_blk(j) // n_tblk, 0, 0))
    return pl.pallas_call(
        functools.partial(_prompt_layer_kernel, layer, last, n_tblk),
        grid=(n_blocks + 1,),
        in_specs=[
            pl.BlockSpec((1, tb, D_MODEL), lambda j: (mixer_blk(j) // n_tblk, mixer_blk(j) % n_tblk, 0)),
            rope_spec, rope_spec, rope_spec,
            _const_spec((tb, tb)),
            pl.BlockSpec(memory_space=pltpu.SMEM),
            _const_spec((DEPTH, HG_WIDTH)),
            _layer_spec(layer, (1, D_MODEL)),
            _const_spec((D_MODEL, IN_WIDTH)),
            _layer_spec(layer, (1, HG_DV)),
            _const_spec((MIX_WIDTH, D_MODEL)),
            _layer_spec(layer, (1, D_MODEL)),
            _const_spec((D_MODEL, D_FF)),
            _const_spec((D_FF, D_MODEL)),
            _const_spec((1, D_MODEL)),
        ],
        out_specs=[
            pl.BlockSpec((1, tb, D_MODEL), lambda j: (tail_blk(j) // n_tblk, tail_blk(j) % n_tblk, 0)),
            carry_spec, carry_spec,
            pl.BlockSpec((1, HG_HEADS, HG_DK, HG_DV), lambda j: (mixer_blk(j) // n_tblk, 0, 0, 0)),
        ],
        out_shape=[
            jax.ShapeDtypeStruct((batch, seq, D_MODEL), F32),
            jax.ShapeDtypeStruct((batch, WINDOW, KV_WIDTH), F32),
            jax.ShapeDtypeStruct((batch, WINDOW, KV_WIDTH), F32),
            jax.ShapeDtypeStruct((batch, HG_HEADS, HG_DK, HG_DV), F32),
        ],
        scratch_shapes=[
            pltpu.VMEM((tb, IN_WIDTH), F32),
            pltpu.VMEM((tb, MIX_WIDTH), BF16),
            pltpu.VMEM((tb, D_MODEL), F32),
            pltpu.VMEM((WINDOW, KV_WIDTH), F32),
            pltpu.VMEM((WINDOW, KV_WIDTH), F32),
            pltpu.VMEM((HG_HEADS, HG_DV, HG_DK), F32),
            pltpu.VMEM((4, tb, HG_DK), F32),
            pltpu.VMEM((tb, HG_DV), F32),
        ],
        compiler_params=pltpu.CompilerParams(
            dimension_semantics=("arbitrary",), vmem_limit_bytes=VMEM_LIMIT),
        name=f"prompt_layer{layer}",
    )(x, *rope, tri, sinks, lb_all, anorm, w_in, onorm, w_o, mnorm, w_up, w_dn, fnorm)


def _sample_in_kernel(x_ref, anorm_ref, win_ref, proj_ref, win_bf_ref, h_ref):
    @pl.when(pl.program_id(0) == 0)
    def _():
        h_ref[...] = _rms(x_ref[...], anorm_ref[...]).astype(BF16)

    win_bf_ref[...] = win_ref[...].astype(BF16)
    proj_ref[...] = _dot(h_ref[...], win_bf_ref[...])


def _sample_out_kernel(last, x_ref, mix_ref, wo_ref, mnorm_ref, wup_ref, wdn_ref, fnorm_ref,
                       y_ref, wo_bf_ref, wup_bf_ref, wdn_bf_ref, h2_ref, acc_ref):
    c = pl.program_id(0)

    @pl.when(c == 0)
    def _():
        wo_bf_ref[...] = wo_ref[...].astype(BF16)
        acc_ref[...], h2_ref[...] = _out_proj(x_ref[...], mix_ref[...].astype(BF16), wo_bf_ref, mnorm_ref)

    wup_bf_ref[...] = wup_ref[...].astype(BF16)
    wdn_bf_ref[...] = wdn_ref[...].astype(BF16)
    acc_ref[...] = _mlp_cols(acc_ref[...], h2_ref[...], wup_bf_ref[...], wdn_bf_ref[...])

    @pl.when(c == pl.num_programs(0) - 1)
    def _():
        y_ref[...] = _rms(acc_ref[...], fnorm_ref[...]) if last else acc_ref[...]


def _sample_mixer_kernel(layer, n_new, n_aliased, *refs):
    (p_ref, cos_ref, sdn_ref, sup_ref, tri_ref, rev_ref, sel_ref, ck_ref, cv_ref, s0_ref, sinks_ref, lb_ref,
     onorm_ref) = refs[:SAMPLE_MIXER_INPUTS]
    mix_ref, nk_ref, nv_ref, ns_ref, pad_ref = refs[SAMPLE_MIXER_INPUTS + n_aliased:]
    slot = 0 if n_aliased else layer
    for out_ref in (nk_ref, nv_ref, ns_ref):
        for other in range(out_ref.shape[0]):
            if other != slot:
                out_ref[other] = jnp.zeros(out_ref.shape[1:], F32)
    sb = ck_ref.shape[0]
    ts = p_ref.shape[0]
    tok_shift = n_new.bit_length() - 1
    key_shift = WINDOW.bit_length() - 1
    n_cache = sb * WINDOW
    cos, sdn, sup = cos_ref[...], sdn_ref[...], sup_ref[...]

    k_new = _rope(p_ref[:, OFF_K:OFF_K + KV_WIDTH], cos, sdn, sup)
    v_new = p_ref[:, OFF_V:OFF_V + KV_WIDTH]
    pad_keys = jnp.zeros((LANES - ts, KV_WIDTH), F32)
    k_all = jnp.concatenate([ck_ref[...].reshape(n_cache, KV_WIDTH), k_new, pad_keys], axis=0).astype(BF16)
    v_all = jnp.concatenate([cv_ref[...].reshape(n_cache, KV_WIDTH), v_new, pad_keys], axis=0).astype(BF16)
    n_keys = n_cache + LANES
    row = lax.broadcasted_iota(jnp.int32, (ts, n_keys), 0)
    col = lax.broadcasted_iota(jnp.int32, (ts, n_keys), 1)
    new_idx = col - n_cache
    cached_ok = ((col >> key_shift) == (row >> tok_shift)) & ((col & (WINDOW - 1)) >= (row & (n_new - 1)))
    new_ok = ((new_idx < ts) & ((new_idx >> tok_shift) == (row >> tok_shift))
              & ((new_idx & (n_new - 1)) <= (row & (n_new - 1))))
    allowed = ((col < n_cache) & cached_ok) | ((col >= n_cache) & new_ok)

    q_ext = _stacked_queries(p_ref, cos, sdn, sup)
    s_all = _dot_nt(jnp.concatenate(q_ext, axis=0), k_all)
    soft = [_softmax_sink(s_all[head * ts:(head + 1) * ts], allowed, sinks_ref[head])
            for head in range(ATT_HEADS)]
    o_all = _dot(jnp.concatenate([p for p, _ in soft], axis=0), v_all)
    att = [o_all[head * ts:(head + 1) * ts] * soft[head][1] for head in range(ATT_HEADS)]
    for hp in range(ATT_HEADS // 2):
        mix_ref[:, hp * LANES:(hp + 1) * LANES] = _merge_head_pair(
            att[2 * hp], att[2 * hp + 1], (2 * hp) // GROUP)
    for s in range(sb):
        nk_ref[slot, s, 0:WINDOW - n_new] = ck_ref[s, n_new:WINDOW]
        nv_ref[slot, s, 0:WINDOW - n_new] = cv_ref[s, n_new:WINDOW]
        nk_ref[slot, s, WINDOW - n_new:WINDOW] = k_new[s * n_new:(s + 1) * n_new]
        nv_ref[slot, s, WINDOW - n_new:WINDOW] = v_new[s * n_new:(s + 1) * n_new]

    lb = _lower_bound(lb_ref[...], layer)
    log_f, k_in = _hgrn_gates(p_ref[:, OFF_HF:OFF_HF + HG_WIDTH], lb)
    b = _cumsum_rows(tri_ref[...], log_f)
    later = _cumsum_rows(rev_ref[...], log_f)
    q_in = _silu(p_ref[:, OFF_HQ:OFF_HQ + HG_WIDTH])
    v_in = p_ref[:, OFF_HI:OFF_HI + HG_WIDTH]
    gate = _silu(p_ref[:, OFF_HG:OFF_HG + HG_WIDTH])
    onorm = onorm_ref[...]
    heads = [slice(hh * HG_DK, (hh + 1) * HG_DK) for hh in range(HG_HEADS)]

    pad_ref[...] = jnp.zeros_like(pad_ref)
    for i, x in enumerate((k_in, b, v_in)):
        pad_ref[i, SUBLANES:SUBLANES + ts] = x
    tok = lax.broadcasted_iota(jnp.int32, (ts, 1), 0) & (n_new - 1)
    o_intra = [jnp.zeros((ts, HG_DV), F32) for _ in heads]
    for d in range(n_new):
        k_d, b_d, v_d = (pad_ref[i, SUBLANES - d:SUBLANES - d + ts] for i in range(3))
        w = q_in * k_d * jnp.exp(jnp.minimum(b - b_d, 0.0))
        for hh, sl in enumerate(heads):
            a_col = jnp.sum(w[:, sl], axis=-1, keepdims=True)
            o_intra[hh] = o_intra[hh] + jnp.where(tok >= d, a_col, 0.0) * v_d[:, sl]

    q_dec = (q_in * jnp.exp(b)).astype(BF16)
    seq_of_row = lax.broadcasted_iota(jnp.int32, (ts, 1), 0) >> tok_shift
    for hh, sl in enumerate(heads):
        o = o_intra[hh]
        for s in range(sb):
            o_s = _dot(q_dec[:, sl], s0_ref[s, hh].astype(BF16))
            o = o + jnp.where(seq_of_row == s, o_s, 0.0)
        mix_ref[:, ATT_WIDTH + hh * HG_DV:ATT_WIDTH + (hh + 1) * HG_DV] = _rms(o, onorm) * gate[:, sl]

    k_hat = k_in * jnp.exp(later)
    e_b = jnp.exp(b)
    e1 = e_b.astype(BF16).astype(F32)
    e2 = (e_b - e1).astype(BF16).astype(F32)
    e3 = e_b - e1 - e2
    own = ((lax.broadcasted_iota(jnp.int32, (ts, sb * HG_DV), 1) >> key_shift)
           == (lax.broadcasted_iota(jnp.int32, (ts, sb * HG_DV), 0) >> tok_shift))
    no_v = jnp.zeros((3 * ts, sb * HG_DV), F32)
    for hh, sl in enumerate(heads):
        lhs = jnp.concatenate([k_hat[:, sl], e1[:, sl], e2[:, sl], e3[:, sl]], axis=0).T.astype(BF16)
        v_rep = jnp.concatenate([v_in[:, sl]] * sb, axis=1)
        v_diag = jnp.concatenate([jnp.where(own, v_rep, 0.0), no_v], axis=0).astype(BF16)
        upd = _dot(lhs, v_diag)
        decay = _dot(lhs, sel_ref[...])
        for s in range(sb):
            cols = slice(s * HG_DV, (s + 1) * HG_DV)
            ns_ref[slot, s, hh] = decay[:, cols] * s0_ref[s, hh] + upd[:, cols]


def _sample_layer(layer, x, rope, cache_k, cache_v, state, new_caches, sinks, lb_all, anorm, w_in, onorm, w_o,
                  mnorm, w_up, w_dn, fnorm):
    n_seq, n_new, _ = x.shape
    n_tok = n_seq * n_new
    ts = LANES // 4
    sb = ts // n_new
    assert sb * n_new == ts and n_seq % sb == 0 and n_new <= SUBLANES
    assert n_new & (n_new - 1) == 0 and WINDOW & (WINDOW - 1) == 0 and HG_DV == WINDOW
    last = layer == DEPTH - 1
    x2 = x.reshape(n_tok, D_MODEL)
    params = pltpu.CompilerParams(vmem_limit_bytes=VMEM_LIMIT)
    whole = _const_spec
    slab = functools.partial(_layer_spec, layer)

    proj, w_in_bf = pl.pallas_call(
        _sample_in_kernel,
        grid=(IN_WIDTH // SAMPLE_IN_BLOCK,),
        in_specs=[whole((n_tok, D_MODEL)), slab((1, D_MODEL)),
                  pl.BlockSpec((None, D_MODEL, SAMPLE_IN_BLOCK), lambda c: (layer, 0, c))],
        out_specs=[pl.BlockSpec((n_tok, SAMPLE_IN_BLOCK), lambda c: (0, c)),
                   pl.BlockSpec((D_MODEL, SAMPLE_IN_BLOCK), lambda c: (0, c))],
        out_shape=[jax.ShapeDtypeStruct((n_tok, IN_WIDTH), F32),
                   jax.ShapeDtypeStruct((D_MODEL, IN_WIDTH), BF16)],
        scratch_shapes=[pltpu.VMEM((n_tok, D_MODEL), BF16)],
        compiler_params=params,
        name=f"sample_in{layer}",
    )(x2, anorm, w_in)

    r = jnp.arange(ts)
    same_seq = (r[:, None] // n_new) == (r[None, :] // n_new)
    tri = (same_seq & (r[None, :] <= r[:, None])).astype(BF16)
    rev = (same_seq & (r[None, :] > r[:, None])).astype(BF16)
    c = jnp.arange(sb * HG_DV)
    is_last = ((r[:, None] % n_new) == n_new - 1) & ((r[:, None] // n_new) == (c[None, :] // HG_DV))
    sel = jnp.concatenate([jnp.zeros_like(is_last)] + [is_last] * 3, axis=0).astype(BF16)

    tok_spec = lambda width: pl.BlockSpec((ts, width), lambda i: (i, 0))
    seq_spec = lambda *tail: pl.BlockSpec((sb,) + tail, lambda i: (i,) + (0,) * len(tail))
    layer_spec = lambda *tail: pl.BlockSpec((None, sb) + tail, lambda i: (layer, i) + (0,) * len(tail))
    rope_spec = _const_spec((ts, LANES))
    first = not new_caches
    slab_spec = lambda *tail: pl.BlockSpec(
        (DEPTH if first else 1, sb) + tail, lambda i: (0 if first else layer, i) + (0,) * len(tail))
    n_in = SAMPLE_MIXER_INPUTS
    aliases = {n_in + i: 1 + i for i in range(len(new_caches))}
    stacked = [(DEPTH, n_seq, WINDOW, KV_WIDTH)] * 2 + [(DEPTH, n_seq, HG_HEADS, HG_DK, HG_DV)]
    mix, *new_caches = pl.pallas_call(
        functools.partial(_sample_mixer_kernel, layer, n_new, len(new_caches)),
        grid=(n_seq // sb,),
        in_specs=[
            tok_spec(IN_WIDTH),
            rope_spec, rope_spec, rope_spec,
            _const_spec((ts, ts)), _const_spec((ts, ts)), _const_spec((4 * ts, sb * HG_DV)),
            layer_spec(WINDOW, KV_WIDTH), layer_spec(WINDOW, KV_WIDTH),
            layer_spec(HG_HEADS, HG_DK, HG_DV),
            pl.BlockSpec(memory_space=pltpu.SMEM),
            _const_spec((DEPTH, HG_WIDTH)),
            _layer_spec(layer, (1, HG_DV)),
        ] + [pl.BlockSpec(memory_space=pl.ANY)] * len(new_caches),
        out_specs=[
            tok_spec(MIX_WIDTH),
            slab_spec(WINDOW, KV_WIDTH), slab_spec(WINDOW, KV_WIDTH),
            slab_spec(HG_HEADS, HG_DK, HG_DV),
        ],
        out_shape=[jax.ShapeDtypeStruct((n_tok, MIX_WIDTH), F32)]
        + [jax.ShapeDtypeStruct(shape, F32) for shape in stacked],
        input_output_aliases=aliases,
        scratch_shapes=[pltpu.VMEM((3, SUBLANES + ts, HG_WIDTH), F32)],
        compiler_params=pltpu.CompilerParams(
            dimension_semantics=("arbitrary",), vmem_limit_bytes=VMEM_LIMIT),
        name=f"sample_mixer{layer}",
    )(proj, *(jnp.tile(t, (sb, 1)) for t in rope), tri, rev, sel, cache_k, cache_v, state, sinks, lb_all,
      onorm, *new_caches)

    y, w_o_bf, w_up_bf, w_dn_bf = pl.pallas_call(
        functools.partial(_sample_out_kernel, last),
        grid=(D_FF // SAMPLE_FF_BLOCK,),
        in_specs=[whole((n_tok, D_MODEL)), whole((n_tok, MIX_WIDTH)), slab((MIX_WIDTH, D_MODEL)),
                  slab((1, D_MODEL)),
                  pl.BlockSpec((None, D_MODEL, SAMPLE_FF_BLOCK), lambda c: (layer, 0, c)),
                  pl.BlockSpec((None, SAMPLE_FF_BLOCK, D_MODEL), lambda c: (layer, c, 0)),
                  whole((1, D_MODEL))],
        out_specs=[pl.BlockSpec((n_tok, D_MODEL), lambda c: (0, 0)),
                   pl.BlockSpec((MIX_WIDTH, D_MODEL), lambda c: (0, 0)),
                   pl.BlockSpec((D_MODEL, SAMPLE_FF_BLOCK), lambda c: (0, c)),
                   pl.BlockSpec((SAMPLE_FF_BLOCK, D_MODEL), lambda c: (c, 0))],
        out_shape=[jax.ShapeDtypeStruct((n_tok, D_MODEL), F32),
                   jax.ShapeDtypeStruct((MIX_WIDTH, D_MODEL), BF16),
                   jax.ShapeDtypeStruct((D_MODEL, D_FF), BF16),
                   jax.ShapeDtypeStruct((D_FF, D_MODEL), BF16)],
        scratch_shapes=[pltpu.VMEM((n_tok, D_MODEL), BF16), pltpu.VMEM((n_tok, D_MODEL), F32)],
        compiler_params=params,
        name=f"sample_out{layer}",
    )(x2, mix, w_o, mnorm, w_up, w_dn, fnorm)
    return y.reshape(n_seq, n_new, D_MODEL), new_caches, (w_in_bf, w_o_bf, w_up_bf, w_dn_bf)


def _rope_tables(pos):
    n = pos.shape[0]
    inv_freq = jnp.power(ROPE_THETA, -jnp.arange(ROT_HALF, dtype=F32) * (2.0 / ROT_DIM))
    ang = pos.astype(F32)[:, None] * inv_freq[None, :]
    cos, sin = jnp.cos(ang), jnp.sin(ang)
    rest = HEAD_DIM - ROT_DIM
    zeros_h = jnp.zeros((n, ROT_HALF), F32)
    cos_t = jnp.concatenate([cos, cos, jnp.ones((n, rest), F32)], axis=1)
    sdn_t = jnp.concatenate([zeros_h, sin, jnp.zeros((n, rest), F32)], axis=1)
    sup_t = jnp.concatenate([-sin, zeros_h, jnp.zeros((n, rest), F32)], axis=1)
    reps = LANES // HEAD_DIM
    return tuple(jnp.tile(t, (1, reps)) for t in (cos_t, sdn_t, sup_t))


def _chunk_tri(n, chunk):
    r = jnp.arange(n)
    same = (r[:, None] // chunk) == (r[None, :] // chunk)
    return (same & (r[None, :] <= r[:, None])).astype(BF16)


def kernel(x_prompt, x_sample, cache_k, cache_v, state_hgrn, attn_norm, w_in, att_sinks, hgrn_lower_bounds,
           hgrn_out_norm, w_o, mlp_norm, w_up, w_down, final_norm):
    batch, seq, _ = x_prompt.shape
    n_seq, n_new, _ = x_sample.shape
    assert seq % PROMPT_BLOCK == 0 and PROMPT_BLOCK % HG_CHUNK == 0

    rope_p = _rope_tables(jnp.arange(seq, dtype=jnp.int32))
    rope_s = _rope_tables(PAST_LEN + jnp.arange(n_new, dtype=jnp.int32))
    tri = _chunk_tri(PROMPT_BLOCK, HG_CHUNK)
    lb_all = hgrn_lower_bounds.astype(F32)
    fnorm = final_norm.reshape(1, D_MODEL)

    ck = cache_k.reshape(DEPTH, n_seq, WINDOW, KV_WIDTH)
    cv = cache_v.reshape(DEPTH, n_seq, WINDOW, KV_WIDTH)
    anorm, onorm, mnorm = (attn_norm.reshape(DEPTH, 1, D_MODEL), hgrn_out_norm.reshape(DEPTH, 1, HG_DV),
                           mlp_norm.reshape(DEPTH, 1, D_MODEL))
    xp, xs = x_prompt, x_sample
    nk_p, nv_p, ns_p, new_caches = [], [], [], []
    for l in range(DEPTH):
        xs, new_caches, (w_in_bf, w_o_bf, w_up_bf, w_dn_bf) = _sample_layer(
            l, xs, rope_s, ck, cv, state_hgrn, new_caches, att_sinks[l], lb_all, anorm, w_in, onorm, w_o, mnorm,
            w_up, w_down, fnorm)
        xp, k1, v1, s1 = _prompt_layer(l, xp, rope_p, tri, att_sinks[l], lb_all, anorm, w_in_bf, onorm, w_o_bf,
                                       mnorm, w_up_bf, w_dn_bf, fnorm)
        nk_p.append(k1), nv_p.append(v1), ns_p.append(s1)
    nk_s, nv_s, ns_s = new_caches

    kv_p = (DEPTH, batch, WINDOW, ATT_KV_HEADS, HEAD_DIM)
    kv_s = (DEPTH, n_seq, WINDOW, ATT_KV_HEADS, HEAD_DIM)
    return (xp, xs,
            jnp.stack(nk_p).reshape(kv_p), jnp.stack(nv_p).reshape(kv_p), jnp.stack(ns_p),
            nk_s.reshape(kv_s), nv_s.reshape(kv_s), ns_s)
```

```python
import functools

import jax
import jax.numpy as jnp
from jax import lax
from jax.experimental import pallas as pl
from jax.experimental.pallas import tpu as pltpu

F32 = jnp.float32
BF16 = jnp.bfloat16

D_MODEL = 1024
DEPTH = 2
PAST_LEN = 16384
ATT_HEADS = 8
ATT_KV_HEADS = 2
HEAD_DIM = 64
GROUP = ATT_HEADS // ATT_KV_HEADS
ATT_WIDTH = ATT_HEADS * HEAD_DIM
KV_WIDTH = ATT_KV_HEADS * HEAD_DIM
WINDOW = 128
ROT_DIM = HEAD_DIM // 4
ROT_HALF = ROT_DIM // 2
ROPE_THETA = 500000.0
HG_HEADS = 4
HG_DK = 128
HG_DV = 128
HG_WIDTH = HG_HEADS * HG_DK
MIX_WIDTH = ATT_WIDTH + HG_WIDTH
IN_WIDTH = ATT_WIDTH + 2 * KV_WIDTH + 4 * HG_WIDTH
D_FF = 4 * D_MODEL
EPS = 1e-6

OFF_Q = 0
OFF_K = ATT_WIDTH
OFF_V = OFF_K + KV_WIDTH
OFF_HQ = OFF_V + KV_WIDTH
OFF_HF = OFF_HQ + HG_WIDTH
OFF_HI = OFF_HF + HG_WIDTH
OFF_HG = OFF_HI + HG_WIDTH

LANES = 128
PROMPT_BLOCK = 256
HG_CHUNK = 128
HG_SUB = 32
HG_GUARD = 80.0
FF_BLOCK = 2048
SUBLANES = 8
SAMPLE_IN_BLOCK = 1408
SAMPLE_FF_BLOCK = 1024
SAMPLE_MIXER_INPUTS = 13
VMEM_LIMIT = 58 * 1024 * 1024


def _dot(a, b):
    return jnp.dot(a, b, preferred_element_type=F32)


def _dot_nt(a, b):
    return lax.dot_general(a, b, (((1,), (1,)), ((), ())), preferred_element_type=F32)


def _rms(x, g_row):
    ms = jnp.mean(x * x, axis=-1, keepdims=True)
    return (x * lax.rsqrt(ms + EPS)) * g_row


def _rope(x, cos, sin_dn, sin_up):
    return x * cos + pltpu.roll(x, ROT_HALF, 1) * sin_dn + pltpu.roll(x, LANES - ROT_HALF, 1) * sin_up


def _lower_bound(lb_all, layer):
    m = jnp.max(lb_all, axis=0, keepdims=True)
    e = jnp.exp(lb_all - m)
    p = e / jnp.sum(e, axis=0, keepdims=True)
    cs = p[0:1]
    for l in range(1, layer + 1):
        cs = cs + p[l:l + 1]
    return jnp.maximum(cs - p[0:1], 0.0)


def _hgrn_gates(z, lb):
    e = jnp.exp(-jnp.abs(z))
    log_sig = jnp.minimum(z, 0.0) - jnp.log1p(e)
    a1 = jnp.log(lb)
    a2 = jnp.log1p(-lb) + log_sig
    log_f = jnp.maximum(a1, a2) + jnp.log1p(jnp.exp(-jnp.abs(a1 - a2)))
    k_in = (1.0 - lb) * (jnp.where(z >= 0.0, e, 1.0) / (1.0 + e))
    return log_f, k_in


def _silu(x):
    return x / (1.0 + jnp.exp(-x))


def _stacked_queries(proj_ref, cos, sdn, sup):
    t = proj_ref.shape[0]
    lo_half = lax.broadcasted_iota(jnp.int32, (t, LANES), 1) < HEAD_DIM
    q_ext = []
    for hp in range(ATT_HEADS // 2):
        q_slab = _rope(proj_ref[:, OFF_Q + hp * LANES:OFF_Q + (hp + 1) * LANES], cos, sdn, sup)
        q_slab = q_slab * (HEAD_DIM ** -0.5)
        q_swap = pltpu.roll(q_slab, HEAD_DIM, 1)
        for sub in range(2):
            kvh = (2 * hp + sub) // GROUP
            src = q_slab if sub == kvh else q_swap
            keep = lo_half if kvh == 0 else ~lo_half
            q_ext.append(jnp.where(keep, src, 0.0).astype(BF16))
    return q_ext


def _softmax_sink(s, allowed, sink):
    s = jnp.where(allowed, s, -jnp.inf)
    m = jnp.maximum(jnp.max(s, axis=-1, keepdims=True), sink)
    p = jnp.exp(s - m)
    inv = 1.0 / (jnp.sum(p, axis=-1, keepdims=True) + jnp.exp(sink - m))
    return p.astype(BF16), inv


def _merge_head_pair(a, b, kvh):
    lo_half = lax.broadcasted_iota(jnp.int32, a.shape, 1) < HEAD_DIM
    if kvh == 0:
        return jnp.where(lo_half, a, pltpu.roll(b, HEAD_DIM, 1))
    return jnp.where(lo_half, pltpu.roll(a, HEAD_DIM, 1), b)


def _cumsum_rows(tri, g):
    g1 = g.astype(BF16)
    r1 = g - g1.astype(F32)
    g2 = r1.astype(BF16)
    g3 = (r1 - g2.astype(F32)).astype(BF16)
    return _dot(tri, g1) + _dot(tri, g2) + _dot(tri, g3)


def _chunk_cumsum(g, tri_groups, planes_ref):
    row = lax.broadcasted_iota(jnp.int32, g.shape, 0) & (SUBLANES - 1)
    for shift in (1, 2, 4):
        g = g + jnp.where(row >= shift, pltpu.roll(g, shift, 0), 0.0)
    n_groups = g.shape[0] // SUBLANES
    n_planes = g.shape[1] // LANES
    for i in range(n_planes):
        planes_ref[i] = g[:, i * LANES:(i + 1) * LANES]
    totals = jnp.concatenate([planes_ref[i, pl.ds(SUBLANES - 1, n_groups, stride=SUBLANES), :]
                              for i in range(n_planes)], axis=1)
    offsets = _cumsum_rows(tri_groups, totals)
    offsets = jnp.broadcast_to(offsets[:, None, :], (n_groups, SUBLANES, g.shape[1]))
    return g + offsets.reshape(g.shape)


def _hgrn_intra_operands(q, k, b):
    c = q.shape[0]
    pairs = []
    for i in range(c // HG_SUB):
        lo, hi = i * HG_SUB, (i + 1) * HG_SUB
        r = b[lo - 1:lo] if i > 0 else jnp.zeros((1, HG_DK), F32)
        pairs.append(((q[lo:hi] * jnp.exp(b[lo:hi] - r)).astype(BF16),
                      (k[:hi] * jnp.exp(r - b[:hi])).astype(BF16)))
    return pairs


def _hgrn_worst_subblock_decay(b_all):
    worst = None
    for lo in range(0, b_all.shape[0], HG_SUB):
        tot = b_all[lo + HG_SUB - 1:lo + HG_SUB]
        if lo % HG_CHUNK:
            tot = tot - b_all[lo - 1:lo]
        worst = tot if worst is None else jnp.minimum(worst, tot)
    return worst


def _hgrn_intra_exact(q, k, v, b, work_ref, out_ref):
    for i, x in enumerate((q, k, v, b)):
        work_ref[i] = x
    s_idx = lax.broadcasted_iota(jnp.int32, (HG_CHUNK, 1), 0)

    def row(t, carry):
        c0 = pl.multiple_of((t // HG_CHUNK) * HG_CHUNK, HG_CHUNK)
        q_t = work_ref[0, pl.ds(t, 1), :]
        b_t = work_ref[3, pl.ds(t, 1), :]
        k_c = work_ref[1, pl.ds(c0, HG_CHUNK), :]
        v_c = work_ref[2, pl.ds(c0, HG_CHUNK), :]
        b_c = work_ref[3, pl.ds(c0, HG_CHUNK), :]
        w = k_c * jnp.exp(jnp.minimum(b_t - b_c, 0.0)) * q_t
        a = jnp.where(s_idx <= t - c0, jnp.sum(w, axis=-1, keepdims=True), 0.0)
        out_ref[pl.ds(t, 1), :] = jnp.sum(a * v_c, axis=0, keepdims=True)
        return carry

    lax.fori_loop(0, q.shape[0], row, 0)
    return out_ref[...]


def _hgrn_intra_scores(a_parts):
    c = HG_SUB * len(a_parts)
    rows = [a if a.shape[1] == c else jnp.concatenate([a, jnp.zeros((HG_SUB, c - a.shape[1]), F32)], axis=1)
            for a in a_parts]
    a = jnp.concatenate(rows, axis=0)
    row = lax.broadcasted_iota(jnp.int32, (c, c), 0)
    col = lax.broadcasted_iota(jnp.int32, (c, c), 1)
    return jnp.where(col <= row, a, 0.0).astype(BF16)


def _out_proj(x, mix_bf16, wo_ref, mnorm_ref):
    x1 = x + _dot(mix_bf16, wo_ref[...])
    return x1, _rms(x1, mnorm_ref[...]).astype(BF16)


def _mlp_cols(acc, h2, w_up_cols, w_dn_rows):
    u = jnp.square(jnp.maximum(_dot(h2, w_up_cols), 0.0)).astype(BF16)
    return acc + _dot(u, w_dn_rows)


def _mlp_block(acc, h2, c, wup_ref, wdn_ref):
    return _mlp_cols(acc, h2, wup_ref[:, c * FF_BLOCK:(c + 1) * FF_BLOCK],
                     wdn_ref[c * FF_BLOCK:(c + 1) * FF_BLOCK, :])


def _prompt_step(with_mixer, with_tail, layer, last, n_tblk, t_blk,
                 x_ref, cos_ref, sdn_ref, sup_ref, tri_ref, sinks_ref, lb_ref, anorm_ref,
                 win_ref, onorm_ref, wo_ref, mnorm_ref, wup_ref, wdn_ref, fnorm_ref,
                 y_ref, nk_ref, nv_ref, ns_ref,
                 proj_ref, mix_ref, xprev_ref, kprev_ref, vprev_ref, st_ref, work_ref, exact_ref):
    tb = PROMPT_BLOCK
    if with_mixer:
        @pl.when(t_blk == 0)
        def _():
            kprev_ref[...] = jnp.zeros_like(kprev_ref)
            vprev_ref[...] = jnp.zeros_like(vprev_ref)
            st_ref[...] = jnp.zeros_like(st_ref)

    if with_tail:
        acc, h2 = _out_proj(xprev_ref[...], mix_ref[...], wo_ref, mnorm_ref)
    if not with_mixer:
        for c in range(D_FF // FF_BLOCK):
            acc = _mlp_block(acc, h2, c, wup_ref, wdn_ref)
        y_ref[0] = _rms(acc, fnorm_ref[...]) if last else acc
        return

    x = x_ref[0]
    xprev_ref[...] = x
    h = _rms(x, anorm_ref[...]).astype(BF16)
    proj_ref[...] = _dot(h, win_ref[...])

    cos, sdn, sup = cos_ref[...], sdn_ref[...], sup_ref[...]

    k_rot = _rope(proj_ref[:, OFF_K:OFF_K + KV_WIDTH], cos, sdn, sup)
    v_new = proj_ref[:, OFF_V:OFF_V + KV_WIDTH]
    k_all = jnp.concatenate([kprev_ref[...], k_rot], axis=0).astype(BF16)
    v_all = jnp.concatenate([vprev_ref[...], v_new], axis=0).astype(BF16)

    q_ext = _stacked_queries(proj_ref, cos, sdn, sup)
    row = lax.broadcasted_iota(jnp.int32, (WINDOW, 2 * WINDOW), 0)
    col = lax.broadcasted_iota(jnp.int32, (WINDOW, 2 * WINDOW), 1)
    in_window = (col >= row) & (col <= row + WINDOW)
    first_lo = jnp.where(t_blk == 0, WINDOW, 0)
    att = [[None] * (tb // WINDOW) for _ in range(ATT_HEADS)]
    for n in range(tb // WINDOW):
        rows_n = slice(n * WINDOW, (n + 1) * WINDOW)
        keys_n = slice(n * WINDOW, (n + 2) * WINDOW)
        q_stack = jnp.concatenate([q[rows_n] for q in q_ext], axis=0)
        s_all = _dot_nt(q_stack, k_all[keys_n])
        allowed = in_window & (col >= first_lo) if n == 0 else in_window
        soft = [_softmax_sink(s_all[head * WINDOW:(head + 1) * WINDOW], allowed, sinks_ref[head])
                for head in range(ATT_HEADS)]
        o_all = _dot(jnp.concatenate([p for p, _ in soft], axis=0), v_all[keys_n])
        for head in range(ATT_HEADS):
            att[head][n] = o_all[head * WINDOW:(head + 1) * WINDOW] * soft[head][1]
    for hp in range(ATT_HEADS // 2):
        a, b = (jnp.concatenate(att[2 * hp + sub], axis=0) for sub in range(2))
        mix_ref[:, hp * LANES:(hp + 1) * LANES] = _merge_head_pair(a, b, (2 * hp) // GROUP).astype(BF16)
    if with_tail:
        acc = _mlp_block(acc, h2, 0, wup_ref, wdn_ref)

    lb = _lower_bound(lb_ref[...], layer)
    log_f, k_in = _hgrn_gates(proj_ref[:, OFF_HF:OFF_HF + HG_WIDTH], lb)
    b_all = _chunk_cumsum(log_f, tri_ref[...], work_ref)
    onorm = onorm_ref[...]
    n_chunks = tb // HG_CHUNK
    units = [(hh, c) for c in range(n_chunks) for hh in range(HG_HEADS)]
    q_in, v_in = [], []
    for hh in range(HG_HEADS):
        q_in.append(_silu(proj_ref[:, OFF_HQ + hh * HG_DK:OFF_HQ + (hh + 1) * HG_DK]))
        v_in.append(proj_ref[:, OFF_HI + hh * HG_DV:OFF_HI + (hh + 1) * HG_DV])

    def piece(x, hh, c, lanes=False):
        x = x[:, hh * HG_DK:(hh + 1) * HG_DK] if lanes else x[hh]
        return x[c * HG_CHUNK:(c + 1) * HG_CHUNK]

    intra_ops = {u: _hgrn_intra_operands(piece(q_in, *u), piece(k_in, *u, lanes=True),
                                         piece(b_all, *u, lanes=True)) for u in units}
    a_full = {u: [_dot_nt(qh, kh) for qh, kh in intra_ops[u]] for u in units}
    o_inter, st_new = {}, []
    for hh in range(HG_HEADS):
        st = st_ref[hh]
        for c in range(n_chunks):
            q, k, v, b = (piece(q_in, hh, c), piece(k_in, hh, c, True), piece(v_in, hh, c),
                          piece(b_all, hh, c, True))
            o_inter[(hh, c)] = _dot_nt((q * jnp.exp(b)).astype(BF16), st.astype(BF16))
            b_last = b[HG_CHUNK - 1:HG_CHUNK]
            k_hat = (k * jnp.exp(b_last - b)).astype(BF16)
            st = st * jnp.exp(b_last) + _dot(v.T.astype(BF16), k_hat)
        st_new.append(st)
    a_mask = {u: _hgrn_intra_scores(a_full[u]) for u in units}
    o_intra = {u: _dot(a_mask[u], piece(v_in, *u).astype(BF16)) for u in units}

    def emit_head(hh, o_intra_h):
        o = o_intra_h + jnp.concatenate([o_inter[(hh, c)] for c in range(n_chunks)], axis=0)
        gate = _silu(proj_ref[:, OFF_HG + hh * HG_DV:OFF_HG + (hh + 1) * HG_DV])
        mix_ref[:, ATT_WIDTH + hh * HG_DV:ATT_WIDTH + (hh + 1) * HG_DV] = (
            _rms(o, onorm) * gate).astype(BF16)

    for hh in range(HG_HEADS):
        emit_head(hh, jnp.concatenate([o_intra[(hh, c)] for c in range(n_chunks)], axis=0))
    if with_tail:
        acc = _mlp_block(acc, h2, 1, wup_ref, wdn_ref)
        y_ref[0] = _rms(acc, fnorm_ref[...]) if last else acc

    @pl.when(jnp.min(_hgrn_worst_subblock_decay(b_all)) < -HG_GUARD)
    def _():
        for hh in range(HG_HEADS):
            sl = slice(hh * HG_DK, (hh + 1) * HG_DK)
            emit_head(hh, _hgrn_intra_exact(q_in[hh], k_in[:, sl], v_in[hh], b_all[:, sl],
                                            work_ref, exact_ref))

    kprev_ref[...] = k_rot[tb - WINDOW:]
    vprev_ref[...] = v_new[tb - WINDOW:]
    nk_ref[0] = k_rot[tb - WINDOW:]
    nv_ref[0] = v_new[tb - WINDOW:]
    for hh in range(HG_HEADS):
        st_ref[hh] = st_new[hh]

    @pl.when(t_blk == n_tblk - 1)
    def _():
        for hh in range(HG_HEADS):
            ns_ref[0, hh] = st_new[hh].T


def _prompt_layer_kernel(layer, last, n_tblk, *refs):
    j = pl.program_id(0)
    n_blocks = pl.num_programs(0) - 1
    t_blk = lax.rem(j, n_tblk)

    def run(with_mixer, with_tail):
        _prompt_step(with_mixer, with_tail, layer, last, n_tblk, t_blk, *refs)

    pl.when(j == 0)(functools.partial(run, True, False))
    pl.when((j > 0) & (j < n_blocks))(functools.partial(run, True, True))
    pl.when(j == n_blocks)(functools.partial(run, False, True))


def _const_spec(shape):
    nd = len(shape)
    return pl.BlockSpec(shape, lambda *_: (0,) * nd, pipeline_mode=pl.Buffered(1))


def _layer_spec(layer, shape):
    nd = len(shape)
    return pl.BlockSpec((None,) + shape, lambda *_: (layer,) + (0,) * nd, pipeline_mode=pl.Buffered(1))


def _prompt_layer(layer, x, rope, tri, sinks, lb_all, anorm, w_in, onorm, w_o, mnorm, w_up, w_dn, fnorm):
    batch, seq, _ = x.shape
    tb = PROMPT_BLOCK
    n_tblk = seq // tb
    n_blocks = batch * n_tblk
    last = layer == DEPTH - 1
    mixer_blk = lambda j: jnp.minimum(j, n_blocks - 1)
    tail_blk = lambda j: jnp.maximum(j - 1, 0)
    rope_spec = pl.BlockSpec((tb, LANES), lambda j: (mixer_blk(j) % n_tblk, 0))
    carry_spec = pl.BlockSpec((1, WINDOW, KV_WIDTH), lambda j: (mixer_blk(j) // n_tblk, 0, 0))
    return pl.pallas_call(
        functools.partial(_prompt_layer_kernel, layer, last, n_tblk),
        grid=(n_blocks + 1,),
        in_specs=[
            pl.BlockSpec((1, tb, D_MODEL), lambda j: (mixer_blk(j) // n_tblk, mixer_blk(j) % n_tblk, 0)),
            rope_spec, rope_spec, rope_spec,
            _const_spec((tb // SUBLANES, tb // SUBLANES)),
            pl.BlockSpec(memory_space=pltpu.SMEM),
            _const_spec((DEPTH, HG_WIDTH)),
            _layer_spec(layer, (1, D_MODEL)),
            _const_spec((D_MODEL, IN_WIDTH)),
            _layer_spec(layer, (1, HG_DV)),
            _const_spec((MIX_WIDTH, D_MODEL)),
            _layer_spec(layer, (1, D_MODEL)),
            _const_spec((D_MODEL, D_FF)),
            _const_spec((D_FF, D_MODEL)),
            _const_spec((1, D_MODEL)),
        ],
        out_specs=[
            pl.BlockSpec((1, tb, D_MODEL), lambda j: (tail_blk(j) // n_tblk, tail_blk(j) % n_tblk, 0)),
            carry_spec, carry_spec,
            pl.BlockSpec((1, HG_HEADS, HG_DK, HG_DV), lambda j: (mixer_blk(j) // n_tblk, 0, 0, 0)),
        ],
        out_shape=[
            jax.ShapeDtypeStruct((batch, seq, D_MODEL), F32),
            jax.ShapeDtypeStruct((batch, WINDOW, KV_WIDTH), F32),
            jax.ShapeDtypeStruct((batch, WINDOW, KV_WIDTH), F32),
            jax.ShapeDtypeStruct((batch, HG_HEADS, HG_DK, HG_DV), F32),
        ],
        scratch_shapes=[
            pltpu.VMEM((tb, IN_WIDTH), F32),
            pltpu.VMEM((tb, MIX_WIDTH), BF16),
            pltpu.VMEM((tb, D_MODEL), F32),
            pltpu.VMEM((WINDOW, KV_WIDTH), F32),
            pltpu.VMEM((WINDOW, KV_WIDTH), F32),
            pltpu.VMEM((HG_HEADS, HG_DV, HG_DK), F32),
            pltpu.VMEM((4, tb, HG_DK), F32),
            pltpu.VMEM((tb, HG_DV), F32),
        ],
        compiler_params=pltpu.CompilerParams(
            dimension_semantics=("arbitrary",), vmem_limit_bytes=VMEM_LIMIT),
        name=f"prompt_layer{layer}",
    )(x, *rope, tri, sinks, lb_all, anorm, w_in, onorm, w_o, mnorm, w_up, w_dn, fnorm)


def _sample_in_kernel(x_ref, anorm_ref, win_ref, proj_ref, win_bf_ref, h_ref):
    @pl.when(pl.program_id(0) == 0)
    def _():
        h_ref[...] = _rms(x_ref[...], anorm_ref[...]).astype(BF16)

    win_bf_ref[...] = win_ref[...].astype(BF16)
    proj_ref[...] = _dot(h_ref[...], win_bf_ref[...])


def _sample_out_kernel(last, x_ref, mix_ref, wo_ref, mnorm_ref, wup_ref, wdn_ref, fnorm_ref,
                       y_ref, wo_bf_ref, wup_bf_ref, wdn_bf_ref, h2_ref, acc_ref):
    c = pl.program_id(0)

    @pl.when(c == 0)
    def _():
        wo_bf_ref[...] = wo_ref[...].astype(BF16)
        acc_ref[...], h2_ref[...] = _out_proj(x_ref[...], mix_ref[...].astype(BF16), wo_bf_ref, mnorm_ref)

    wup_bf_ref[...] = wup_ref[...].astype(BF16)
    wdn_bf_ref[...] = wdn_ref[...].astype(BF16)
    acc_ref[...] = _mlp_cols(acc_ref[...], h2_ref[...], wup_bf_ref[...], wdn_bf_ref[...])

    @pl.when(c == pl.num_programs(0) - 1)
    def _():
        y_ref[...] = _rms(acc_ref[...], fnorm_ref[...]) if last else acc_ref[...]


def _sample_mixer_kernel(layer, n_new, n_aliased, *refs):
    (p_ref, cos_ref, sdn_ref, sup_ref, tri_ref, rev_ref, sel_ref, ck_ref, cv_ref, s0_ref, sinks_ref, lb_ref,
     onorm_ref) = refs[:SAMPLE_MIXER_INPUTS]
    mix_ref, nk_ref, nv_ref, ns_ref, pad_ref = refs[SAMPLE_MIXER_INPUTS + n_aliased:]
    slot = 0 if n_aliased else layer
    for out_ref in (nk_ref, nv_ref, ns_ref):
        for other in range(out_ref.shape[0]):
            if other != slot:
                out_ref[other] = jnp.zeros(out_ref.shape[1:], F32)
    sb = ck_ref.shape[0]
    ts = p_ref.shape[0]
    tok_shift = n_new.bit_length() - 1
    key_shift = WINDOW.bit_length() - 1
    n_cache = sb * WINDOW
    cos, sdn, sup = cos_ref[...], sdn_ref[...], sup_ref[...]

    k_new = _rope(p_ref[:, OFF_K:OFF_K + KV_WIDTH], cos, sdn, sup)
    v_new = p_ref[:, OFF_V:OFF_V + KV_WIDTH]
    pad_keys = jnp.zeros((LANES - ts, KV_WIDTH), F32)
    k_all = jnp.concatenate([ck_ref[...].reshape(n_cache, KV_WIDTH), k_new, pad_keys], axis=0).astype(BF16)
    v_all = jnp.concatenate([cv_ref[...].reshape(n_cache, KV_WIDTH), v_new, pad_keys], axis=0).astype(BF16)
    n_keys = n_cache + LANES
    row = lax.broadcasted_iota(jnp.int32, (ts, n_keys), 0)
    col = lax.broadcasted_iota(jnp.int32, (ts, n_keys), 1)
    new_idx = col - n_cache
    cached_ok = ((col >> key_shift) == (row >> tok_shift)) & ((col & (WINDOW - 1)) >= (row & (n_new - 1)))
    new_ok = ((new_idx < ts) & ((new_idx >> tok_shift) == (row >> tok_shift))
              & ((new_idx & (n_new - 1)) <= (row & (n_new - 1))))
    allowed = ((col < n_cache) & cached_ok) | ((col >= n_cache) & new_ok)

    q_ext = _stacked_queries(p_ref, cos, sdn, sup)
    s_all = _dot_nt(jnp.concatenate(q_ext, axis=0), k_all)
    soft = [_softmax_sink(s_all[head * ts:(head + 1) * ts], allowed, sinks_ref[head])
            for head in range(ATT_HEADS)]
    o_all = _dot(jnp.concatenate([p for p, _ in soft], axis=0), v_all)
    att = [o_all[head * ts:(head + 1) * ts] * soft[head][1] for head in range(ATT_HEADS)]
    for hp in range(ATT_HEADS // 2):
        mix_ref[:, hp * LANES:(hp + 1) * LANES] = _merge_head_pair(
            att[2 * hp], att[2 * hp + 1], (2 * hp) // GROUP)
    for s in range(sb):
        nk_ref[slot, s, 0:WINDOW - n_new] = ck_ref[s, n_new:WINDOW]
        nv_ref[slot, s, 0:WINDOW - n_new] = cv_ref[s, n_new:WINDOW]
        nk_ref[slot, s, WINDOW - n_new:WINDOW] = k_new[s * n_new:(s + 1) * n_new]
        nv_ref[slot, s, WINDOW - n_new:WINDOW] = v_new[s * n_new:(s + 1) * n_new]

    lb = _lower_bound(lb_ref[...], layer)
    log_f, k_in = _hgrn_gates(p_ref[:, OFF_HF:OFF_HF + HG_WIDTH], lb)
    b = _cumsum_rows(tri_ref[...], log_f)
    later = _cumsum_rows(rev_ref[...], log_f)
    q_in = _silu(p_ref[:, OFF_HQ:OFF_HQ + HG_WIDTH])
    v_in = p_ref[:, OFF_HI:OFF_HI + HG_WIDTH]
    gate = _silu(p_ref[:, OFF_HG:OFF_HG + HG_WIDTH])
    onorm = onorm_ref[...]
    heads = [slice(hh * HG_DK, (hh + 1) * HG_DK) for hh in range(HG_HEADS)]

    pad_ref[...] = jnp.zeros_like(pad_ref)
    for i, x in enumerate((k_in, b, v_in)):
        pad_ref[i, SUBLANES:SUBLANES + ts] = x
    tok = lax.broadcasted_iota(jnp.int32, (ts, 1), 0) & (n_new - 1)
    o_intra = [jnp.zeros((ts, HG_DV), F32) for _ in heads]
    for d in range(n_new):
        k_d, b_d, v_d = (pad_ref[i, SUBLANES - d:SUBLANES - d + ts] for i in range(3))
        w = q_in * k_d * jnp.exp(jnp.minimum(b - b_d, 0.0))
        for hh, sl in enumerate(heads):
            a_col = jnp.sum(w[:, sl], axis=-1, keepdims=True)
            o_intra[hh] = o_intra[hh] + jnp.where(tok >= d, a_col, 0.0) * v_d[:, sl]

    q_dec = (q_in * jnp.exp(b)).astype(BF16)
    seq_of_row = lax.broadcasted_iota(jnp.int32, (ts, 1), 0) >> tok_shift
    for hh, sl in enumerate(heads):
        o = o_intra[hh]
        for s in range(sb):
            o_s = _dot(q_dec[:, sl], s0_ref[s, hh].astype(BF16))
            o = o + jnp.where(seq_of_row == s, o_s, 0.0)
        mix_ref[:, ATT_WIDTH + hh * HG_DV:ATT_WIDTH + (hh + 1) * HG_DV] = _rms(o, onorm) * gate[:, sl]

    k_hat = k_in * jnp.exp(later)
    e_b = jnp.exp(b)
    e1 = e_b.astype(BF16).astype(F32)
    e2 = (e_b - e1).astype(BF16).astype(F32)
    e3 = e_b - e1 - e2
    own = ((lax.broadcasted_iota(jnp.int32, (ts, sb * HG_DV), 1) >> key_shift)
           == (lax.broadcasted_iota(jnp.int32, (ts, sb * HG_DV), 0) >> tok_shift))
    no_v = jnp.zeros((3 * ts, sb * HG_DV), F32)
    for hh, sl in enumerate(heads):
        lhs = jnp.concatenate([k_hat[:, sl], e1[:, sl], e2[:, sl], e3[:, sl]], axis=0).T.astype(BF16)
        v_rep = jnp.concatenate([v_in[:, sl]] * sb, axis=1)
        v_diag = jnp.concatenate([jnp.where(own, v_rep, 0.0), no_v], axis=0).astype(BF16)
        upd = _dot(lhs, v_diag)
        decay = _dot(lhs, sel_ref[...])
        for s in range(sb):
            cols = slice(s * HG_DV, (s + 1) * HG_DV)
            ns_ref[slot, s, hh] = decay[:, cols] * s0_ref[s, hh] + upd[:, cols]


def _sample_layer(layer, x, rope, cache_k, cache_v, state, new_caches, sinks, lb_all, anorm, w_in, onorm, w_o,
                  mnorm, w_up, w_dn, fnorm):
    n_seq, n_new, _ = x.shape
    n_tok = n_seq * n_new
    ts = LANES // 4
    sb = ts // n_new
    assert sb * n_new == ts and n_seq % sb == 0 and n_new <= SUBLANES
    assert n_new & (n_new - 1) == 0 and WINDOW & (WINDOW - 1) == 0 and HG_DV == WINDOW
    last = layer == DEPTH - 1
    x2 = x.reshape(n_tok, D_MODEL)
    params = pltpu.CompilerParams(vmem_limit_bytes=VMEM_LIMIT)
    whole = _const_spec
    slab = functools.partial(_layer_spec, layer)

    proj, w_in_bf = pl.pallas_call(
        _sample_in_kernel,
        grid=(IN_WIDTH // SAMPLE_IN_BLOCK,),
        in_specs=[whole((n_tok, D_MODEL)), slab((1, D_MODEL)),
                  pl.BlockSpec((None, D_MODEL, SAMPLE_IN_BLOCK), lambda c: (layer, 0, c))],
        out_specs=[pl.BlockSpec((n_tok, SAMPLE_IN_BLOCK), lambda c: (0, c)),
                   pl.BlockSpec((D_MODEL, SAMPLE_IN_BLOCK), lambda c: (0, c))],
        out_shape=[jax.ShapeDtypeStruct((n_tok, IN_WIDTH), F32),
                   jax.ShapeDtypeStruct((D_MODEL, IN_WIDTH), BF16)],
        scratch_shapes=[pltpu.VMEM((n_tok, D_MODEL), BF16)],
        compiler_params=params,
        name=f"sample_in{layer}",
    )(x2, anorm, w_in)

    r = jnp.arange(ts)
    same_seq = (r[:, None] // n_new) == (r[None, :] // n_new)
    tri = (same_seq & (r[None, :] <= r[:, None])).astype(BF16)
    rev = (same_seq & (r[None, :] > r[:, None])).astype(BF16)
    c = jnp.arange(sb * HG_DV)
    is_last = ((r[:, None] % n_new) == n_new - 1) & ((r[:, None] // n_new) == (c[None, :] // HG_DV))
    sel = jnp.concatenate([jnp.zeros_like(is_last)] + [is_last] * 3, axis=0).astype(BF16)

    tok_spec = lambda width: pl.BlockSpec((ts, width), lambda i: (i, 0))
    seq_spec = lambda *tail: pl.BlockSpec((sb,) + tail, lambda i: (i,) + (0,) * len(tail))
    layer_spec = lambda *tail: pl.BlockSpec((None, sb) + tail, lambda i: (layer, i) + (0,) * len(tail))
    rope_spec = _const_spec((ts, LANES))
    first = not new_caches
    slab_spec = lambda *tail: pl.BlockSpec(
        (DEPTH if first else 1, sb) + tail, lambda i: (0 if first else layer, i) + (0,) * len(tail))
    n_in = SAMPLE_MIXER_INPUTS
    aliases = {n_in + i: 1 + i for i in range(len(new_caches))}
    stacked = [(DEPTH, n_seq, WINDOW, KV_WIDTH)] * 2 + [(DEPTH, n_seq, HG_HEADS, HG_DK, HG_DV)]
    mix, *new_caches = pl.pallas_call(
        functools.partial(_sample_mixer_kernel, layer, n_new, len(new_caches)),
        grid=(n_seq // sb,),
        in_specs=[
            tok_spec(IN_WIDTH),
            rope_spec, rope_spec, rope_spec,
            _const_spec((ts, ts)), _const_spec((ts, ts)), _const_spec((4 * ts, sb * HG_DV)),
            layer_spec(WINDOW, KV_WIDTH), layer_spec(WINDOW, KV_WIDTH),
            layer_spec(HG_HEADS, HG_DK, HG_DV),
            pl.BlockSpec(memory_space=pltpu.SMEM),
            _const_spec((DEPTH, HG_WIDTH)),
            _layer_spec(layer, (1, HG_DV)),
        ] + [pl.BlockSpec(memory_space=pl.ANY)] * len(new_caches),
        out_specs=[
            tok_spec(MIX_WIDTH),
            slab_spec(WINDOW, KV_WIDTH), slab_spec(WINDOW, KV_WIDTH),
            slab_spec(HG_HEADS, HG_DK, HG_DV),
        ],
        out_shape=[jax.ShapeDtypeStruct((n_tok, MIX_WIDTH), F32)]
        + [jax.ShapeDtypeStruct(shape, F32) for shape in stacked],
        input_output_aliases=aliases,
        scratch_shapes=[pltpu.VMEM((3, SUBLANES + ts, HG_WIDTH), F32)],
        compiler_params=pltpu.CompilerParams(
            dimension_semantics=("arbitrary",), vmem_limit_bytes=VMEM_LIMIT),
        name=f"sample_mixer{layer}",
    )(proj, *(jnp.tile(t, (sb, 1)) for t in rope), tri, rev, sel, cache_k, cache_v, state, sinks, lb_all,
      onorm, *new_caches)

    y, w_o_bf, w_up_bf, w_dn_bf = pl.pallas_call(
        functools.partial(_sample_out_kernel, last),
        grid=(D_FF // SAMPLE_FF_BLOCK,),
        in_specs=[whole((n_tok, D_MODEL)), whole((n_tok, MIX_WIDTH)), slab((MIX_WIDTH, D_MODEL)),
                  slab((1, D_MODEL)),
                  pl.BlockSpec((None, D_MODEL, SAMPLE_FF_BLOCK), lambda c: (layer, 0, c)),
                  pl.BlockSpec((None, SAMPLE_FF_BLOCK, D_MODEL), lambda c: (layer, c, 0)),
                  whole((1, D_MODEL))],
        out_specs=[pl.BlockSpec((n_tok, D_MODEL), lambda c: (0, 0)),
                   pl.BlockSpec((MIX_WIDTH, D_MODEL), lambda c: (0, 0)),
                   pl.BlockSpec((D_MODEL, SAMPLE_FF_BLOCK), lambda c: (0, c)),
                   pl.BlockSpec((SAMPLE_FF_BLOCK, D_MODEL), lambda c: (c, 0))],
        out_shape=[jax.ShapeDtypeStruct((n_tok, D_MODEL), F32),
                   jax.ShapeDtypeStruct((MIX_WIDTH, D_MODEL), BF16),
                   jax.ShapeDtypeStruct((D_MODEL, D_FF), BF16),
                   jax.ShapeDtypeStruct((D_FF, D_MODEL), BF16)],
        scratch_shapes=[pltpu.VMEM((n_tok, D_MODEL), BF16), pltpu.VMEM((n_tok, D_MODEL), F32)],
        compiler_params=params,
        name=f"sample_out{layer}",
    )(x2, mix, w_o, mnorm, w_up, w_dn, fnorm)
    return y.reshape(n_seq, n_new, D_MODEL), new_caches, (w_in_bf, w_o_bf, w_up_bf, w_dn_bf)


def _rope_tables(pos):
    n = pos.shape[0]
    inv_freq = jnp.power(ROPE_THETA, -jnp.arange(ROT_HALF, dtype=F32) * (2.0 / ROT_DIM))
    ang = pos.astype(F32)[:, None] * inv_freq[None, :]
    cos, sin = jnp.cos(ang), jnp.sin(ang)
    rest = HEAD_DIM - ROT_DIM
    zeros_h = jnp.zeros((n, ROT_HALF), F32)
    cos_t = jnp.concatenate([cos, cos, jnp.ones((n, rest), F32)], axis=1)
    sdn_t = jnp.concatenate([zeros_h, sin, jnp.zeros((n, rest), F32)], axis=1)
    sup_t = jnp.concatenate([-sin, zeros_h, jnp.zeros((n, rest), F32)], axis=1)
    reps = LANES // HEAD_DIM
    return tuple(jnp.tile(t, (1, reps)) for t in (cos_t, sdn_t, sup_t))


def _group_tri(n, per_chunk):
    r = jnp.arange(n)
    same = (r[:, None] // per_chunk) == (r[None, :] // per_chunk)
    return (same & (r[None, :] < r[:, None])).astype(BF16)


def kernel(x_prompt, x_sample, cache_k, cache_v, state_hgrn, attn_norm, w_in, att_sinks, hgrn_lower_bounds,
           hgrn_out_norm, w_o, mlp_norm, w_up, w_down, final_norm):
    batch, seq, _ = x_prompt.shape
    n_seq, n_new, _ = x_sample.shape
    assert seq % PROMPT_BLOCK == 0 and PROMPT_BLOCK % HG_CHUNK == 0

    rope_p = _rope_tables(jnp.arange(seq, dtype=jnp.int32))
    rope_s = _rope_tables(PAST_LEN + jnp.arange(n_new, dtype=jnp.int32))
    tri = _group_tri(PROMPT_BLOCK // SUBLANES, HG_CHUNK // SUBLANES)
    lb_all = hgrn_lower_bounds.astype(F32)
    fnorm = final_norm.reshape(1, D_MODEL)

    ck = cache_k.reshape(DEPTH, n_seq, WINDOW, KV_WIDTH)
    cv = cache_v.reshape(DEPTH, n_seq, WINDOW, KV_WIDTH)
    anorm, onorm, mnorm = (attn_norm.reshape(DEPTH, 1, D_MODEL), hgrn_out_norm.reshape(DEPTH, 1, HG_DV),
                           mlp_norm.reshape(DEPTH, 1, D_MODEL))
    xp, xs = x_prompt, x_sample
    nk_p, nv_p, ns_p, new_caches = [], [], [], []
    for l in range(DEPTH):
        xs, new_caches, (w_in_bf, w_o_bf, w_up_bf, w_dn_bf) = _sample_layer(
            l, xs, rope_s, ck, cv, state_hgrn, new_caches, att_sinks[l], lb_all, anorm, w_in, onorm, w_o, mnorm,
            w_up, w_down, fnorm)
        xp, k1, v1, s1 = _prompt_layer(l, xp, rope_p, tri, att_sinks[l], lb_all, anorm, w_in_bf, onorm, w_o_bf,
                                       mnorm, w_up_bf, w_dn_bf, fnorm)
        nk_p.append(k1), nv_p.append(v1), ns_p.append(s1)
    nk_s, nv_s, ns_s = new_caches

    kv_p = (DEPTH, batch, WINDOW, ATT_KV_HEADS, HEAD_DIM)
    kv_s = (DEPTH, n_seq, WINDOW, ATT_KV_HEADS, HEAD_DIM)
    return (xp, xs,
            jnp.stack(nk_p).reshape(kv_p), jnp.stack(nv_p).reshape(kv_p), jnp.stack(ns_p),
            nk_s.reshape(kv_s), nv_s.reshape(kv_s), ns_s)
```

```python
import functools

import jax
import jax.numpy as jnp
from jax import lax
from jax.experimental import pallas as pl
from jax.experimental.pallas import tpu as pltpu

F32 = jnp.float32
BF16 = jnp.bfloat16

D_MODEL = 1024
DEPTH = 2
PAST_LEN = 16384
ATT_HEADS = 8
ATT_KV_HEADS = 2
HEAD_DIM = 64
GROUP = ATT_HEADS // ATT_KV_HEADS
ATT_WIDTH = ATT_HEADS * HEAD_DIM
KV_WIDTH = ATT_KV_HEADS * HEAD_DIM
WINDOW = 128
ROT_DIM = HEAD_DIM // 4
ROT_HALF = ROT_DIM // 2
ROPE_THETA = 500000.0
HG_HEADS = 4
HG_DK = 128
HG_DV = 128
HG_WIDTH = HG_HEADS * HG_DK
MIX_WIDTH = ATT_WIDTH + HG_WIDTH
IN_WIDTH = ATT_WIDTH + 2 * KV_WIDTH + 4 * HG_WIDTH
D_FF = 4 * D_MODEL
EPS = 1e-6

OFF_Q = 0
OFF_K = ATT_WIDTH
OFF_V = OFF_K + KV_WIDTH
OFF_HQ = OFF_V + KV_WIDTH
OFF_HF = OFF_HQ + HG_WIDTH
OFF_HI = OFF_HF + HG_WIDTH
OFF_HG = OFF_HI + HG_WIDTH

LANES = 128
PROMPT_BLOCK = 256
HG_CHUNK = 128
HG_SUB = 32
HG_GUARD = 80.0
FF_BLOCK = 2048
SUBLANES = 8
SAMPLE_IN_BLOCK = 1408
SAMPLE_FF_BLOCK = 1024
SAMPLE_MIXER_INPUTS = 13
VMEM_LIMIT = 58 * 1024 * 1024


def _dot(a, b):
    return jnp.dot(a, b, preferred_element_type=F32)


def _dot_nt(a, b):
    return lax.dot_general(a, b, (((1,), (1,)), ((), ())), preferred_element_type=F32)


def _rms(x, g_row):
    ms = jnp.mean(x * x, axis=-1, keepdims=True)
    return (x * lax.rsqrt(ms + EPS)) * g_row


def _rope(x, cos, sin_dn, sin_up):
    return x * cos + pltpu.roll(x, ROT_HALF, 1) * sin_dn + pltpu.roll(x, LANES - ROT_HALF, 1) * sin_up


def _lower_bound(lb_all, layer):
    m = jnp.max(lb_all, axis=0, keepdims=True)
    e = jnp.exp(lb_all - m)
    p = e / jnp.sum(e, axis=0, keepdims=True)
    cs = p[0:1]
    for l in range(1, layer + 1):
        cs = cs + p[l:l + 1]
    return jnp.maximum(cs - p[0:1], 0.0)


def _hgrn_gates(z, lb):
    e = jnp.exp(-jnp.abs(z))
    log_sig = jnp.minimum(z, 0.0) - jnp.log1p(e)
    a1 = jnp.log(lb)
    a2 = jnp.log1p(-lb) + log_sig
    log_f = jnp.maximum(a1, a2) + jnp.log1p(jnp.exp(-jnp.abs(a1 - a2)))
    k_in = (1.0 - lb) * (jnp.where(z >= 0.0, e, 1.0) / (1.0 + e))
    return log_f, k_in


def _silu(x):
    return x * (0.5 + 0.5 * jnp.tanh(0.5 * x))


def _stacked_queries(proj_ref, cos, sdn, sup):
    t = proj_ref.shape[0]
    lo_half = lax.broadcasted_iota(jnp.int32, (t, LANES), 1) < HEAD_DIM
    q_ext = []
    for hp in range(ATT_HEADS // 2):
        q_slab = _rope(proj_ref[:, OFF_Q + hp * LANES:OFF_Q + (hp + 1) * LANES], cos, sdn, sup)
        q_slab = q_slab * (HEAD_DIM ** -0.5)
        q_swap = pltpu.roll(q_slab, HEAD_DIM, 1)
        for sub in range(2):
            kvh = (2 * hp + sub) // GROUP
            src = q_slab if sub == kvh else q_swap
            keep = lo_half if kvh == 0 else ~lo_half
            q_ext.append(jnp.where(keep, src, 0.0).astype(BF16))
    return q_ext


def _softmax_sink(s, allowed, sink):
    s = jnp.where(allowed, s, -jnp.inf)
    m = jnp.maximum(jnp.max(s, axis=-1, keepdims=True), sink)
    p = jnp.exp(s - m)
    inv = 1.0 / (jnp.sum(p, axis=-1, keepdims=True) + jnp.exp(sink - m))
    return p.astype(BF16), inv


def _merge_head_pair(a, b, kvh):
    lo_half = lax.broadcasted_iota(jnp.int32, a.shape, 1) < HEAD_DIM
    if kvh == 0:
        return jnp.where(lo_half, a, pltpu.roll(b, HEAD_DIM, 1))
    return jnp.where(lo_half, pltpu.roll(a, HEAD_DIM, 1), b)


def _cumsum_rows(tri, g):
    g1 = g.astype(BF16)
    r1 = g - g1.astype(F32)
    g2 = r1.astype(BF16)
    g3 = (r1 - g2.astype(F32)).astype(BF16)
    return _dot(tri, g1) + _dot(tri, g2) + _dot(tri, g3)


def _hgrn_intra_operands(q, k, b):
    c = q.shape[0]
    pairs = []
    for i in range(c // HG_SUB):
        lo, hi = i * HG_SUB, (i + 1) * HG_SUB
        r = b[lo - 1:lo] if i > 0 else jnp.zeros((1, HG_DK), F32)
        pairs.append(((q[lo:hi] * jnp.exp(b[lo:hi] - r)).astype(BF16),
                      (k[:hi] * jnp.exp(r - b[:hi])).astype(BF16)))
    return pairs


def _hgrn_worst_subblock_decay(b_all):
    worst = None
    for lo in range(0, b_all.shape[0], HG_SUB):
        tot = b_all[lo + HG_SUB - 1:lo + HG_SUB]
        if lo % HG_CHUNK:
            tot = tot - b_all[lo - 1:lo]
        worst = tot if worst is None else jnp.minimum(worst, tot)
    return worst


def _hgrn_intra_exact(q, k, v, b, work_ref, out_ref):
    for i, x in enumerate((q, k, v, b)):
        work_ref[i] = x
    s_idx = lax.broadcasted_iota(jnp.int32, (HG_CHUNK, 1), 0)

    def row(t, carry):
        c0 = pl.multiple_of((t // HG_CHUNK) * HG_CHUNK, HG_CHUNK)
        q_t = work_ref[0, pl.ds(t, 1), :]
        b_t = work_ref[3, pl.ds(t, 1), :]
        k_c = work_ref[1, pl.ds(c0, HG_CHUNK), :]
        v_c = work_ref[2, pl.ds(c0, HG_CHUNK), :]
        b_c = work_ref[3, pl.ds(c0, HG_CHUNK), :]
        w = k_c * jnp.exp(jnp.minimum(b_t - b_c, 0.0)) * q_t
        a = jnp.where(s_idx <= t - c0, jnp.sum(w, axis=-1, keepdims=True), 0.0)
        out_ref[pl.ds(t, 1), :] = jnp.sum(a * v_c, axis=0, keepdims=True)
        return carry

    lax.fori_loop(0, q.shape[0], row, 0)
    return out_ref[...]


def _hgrn_intra_scores(a_parts):
    c = HG_SUB * len(a_parts)
    rows = [a if a.shape[1] == c else jnp.concatenate([a, jnp.zeros((HG_SUB, c - a.shape[1]), F32)], axis=1)
            for a in a_parts]
    a = jnp.concatenate(rows, axis=0)
    row = lax.broadcasted_iota(jnp.int32, (c, c), 0)
    col = lax.broadcasted_iota(jnp.int32, (c, c), 1)
    return jnp.where(col <= row, a, 0.0).astype(BF16)


def _out_proj(x, mix_bf16, wo_ref, mnorm_ref):
    x1 = x + _dot(mix_bf16, wo_ref[...])
    return x1, _rms(x1, mnorm_ref[...]).astype(BF16)


def _mlp_cols(acc, h2, w_up_cols, w_dn_rows):
    u = jnp.square(jnp.maximum(_dot(h2, w_up_cols), 0.0)).astype(BF16)
    return acc + _dot(u, w_dn_rows)


def _mlp_block(acc, h2, c, wup_ref, wdn_ref):
    return _mlp_cols(acc, h2, wup_ref[:, c * FF_BLOCK:(c + 1) * FF_BLOCK],
                     wdn_ref[c * FF_BLOCK:(c + 1) * FF_BLOCK, :])


def _prompt_step(with_mixer, with_tail, layer, last, n_tblk, t_blk,
                 x_ref, cos_ref, sdn_ref, sup_ref, tri_ref, sinks_ref, lb_ref, anorm_ref,
                 win_ref, onorm_ref, wo_ref, mnorm_ref, wup_ref, wdn_ref, fnorm_ref,
                 y_ref, nk_ref, nv_ref, ns_ref,
                 proj_ref, mix_ref, xprev_ref, kprev_ref, vprev_ref, st_ref, work_ref, exact_ref):
    tb = PROMPT_BLOCK
    if with_mixer:
        @pl.when(t_blk == 0)
        def _():
            kprev_ref[...] = jnp.zeros_like(kprev_ref)
            vprev_ref[...] = jnp.zeros_like(vprev_ref)
            st_ref[...] = jnp.zeros_like(st_ref)

    if with_tail:
        acc, h2 = _out_proj(xprev_ref[...], mix_ref[...], wo_ref, mnorm_ref)
    if not with_mixer:
        for c in range(D_FF // FF_BLOCK):
            acc = _mlp_block(acc, h2, c, wup_ref, wdn_ref)
        y_ref[0] = _rms(acc, fnorm_ref[...]) if last else acc
        return

    x = x_ref[0]
    xprev_ref[...] = x
    h = _rms(x, anorm_ref[...]).astype(BF16)
    proj_ref[...] = _dot(h, win_ref[...])

    cos, sdn, sup = cos_ref[...], sdn_ref[...], sup_ref[...]

    k_rot = _rope(proj_ref[:, OFF_K:OFF_K + KV_WIDTH], cos, sdn, sup)
    v_new = proj_ref[:, OFF_V:OFF_V + KV_WIDTH]
    k_all = jnp.concatenate([kprev_ref[...], k_rot], axis=0).astype(BF16)
    v_all = jnp.concatenate([vprev_ref[...], v_new], axis=0).astype(BF16)

    q_ext = _stacked_queries(proj_ref, cos, sdn, sup)
    row = lax.broadcasted_iota(jnp.int32, (WINDOW, 2 * WINDOW), 0)
    col = lax.broadcasted_iota(jnp.int32, (WINDOW, 2 * WINDOW), 1)
    in_window = (col >= row) & (col <= row + WINDOW)
    first_lo = jnp.where(t_blk == 0, WINDOW, 0)
    att = [[None] * (tb // WINDOW) for _ in range(ATT_HEADS)]
    for n in range(tb // WINDOW):
        rows_n = slice(n * WINDOW, (n + 1) * WINDOW)
        keys_n = slice(n * WINDOW, (n + 2) * WINDOW)
        q_stack = jnp.concatenate([q[rows_n] for q in q_ext], axis=0)
        s_all = _dot_nt(q_stack, k_all[keys_n])
        allowed = in_window & (col >= first_lo) if n == 0 else in_window
        soft = [_softmax_sink(s_all[head * WINDOW:(head + 1) * WINDOW], allowed, sinks_ref[head])
                for head in range(ATT_HEADS)]
        o_all = _dot(jnp.concatenate([p for p, _ in soft], axis=0), v_all[keys_n])
        for head in range(ATT_HEADS):
            att[head][n] = o_all[head * WINDOW:(head + 1) * WINDOW] * soft[head][1]
    for hp in range(ATT_HEADS // 2):
        a, b = (jnp.concatenate(att[2 * hp + sub], axis=0) for sub in range(2))
        mix_ref[:, hp * LANES:(hp + 1) * LANES] = _merge_head_pair(a, b, (2 * hp) // GROUP).astype(BF16)
    if with_tail:
        acc = _mlp_block(acc, h2, 0, wup_ref, wdn_ref)

    lb = _lower_bound(lb_ref[...], layer)
    log_f, k_in = _hgrn_gates(proj_ref[:, OFF_HF:OFF_HF + HG_WIDTH], lb)
    b_all = _cumsum_rows(tri_ref[...], log_f)
    onorm = onorm_ref[...]
    n_chunks = tb // HG_CHUNK
    units = [(hh, c) for c in range(n_chunks) for hh in range(HG_HEADS)]
    q_in, v_in = [], []
    for hh in range(HG_HEADS):
        q_in.append(_silu(proj_ref[:, OFF_HQ + hh * HG_DK:OFF_HQ + (hh + 1) * HG_DK]))
        v_in.append(proj_ref[:, OFF_HI + hh * HG_DV:OFF_HI + (hh + 1) * HG_DV])

    def piece(x, hh, c, lanes=False):
        x = x[:, hh * HG_DK:(hh + 1) * HG_DK] if lanes else x[hh]
        return x[c * HG_CHUNK:(c + 1) * HG_CHUNK]

    intra_ops = {u: _hgrn_intra_operands(piece(q_in, *u), piece(k_in, *u, lanes=True),
                                         piece(b_all, *u, lanes=True)) for u in units}
    a_full = {u: [_dot_nt(qh, kh) for qh, kh in intra_ops[u]] for u in units}
    o_inter, st_new = {}, []
    for hh in range(HG_HEADS):
        st = st_ref[hh]
        for c in range(n_chunks):
            q, k, v, b = (piece(q_in, hh, c), piece(k_in, hh, c, True), piece(v_in, hh, c),
                          piece(b_all, hh, c, True))
            o_inter[(hh, c)] = _dot_nt((q * jnp.exp(b)).astype(BF16), st.astype(BF16))
            b_last = b[HG_CHUNK - 1:HG_CHUNK]
            k_hat = (k * jnp.exp(b_last - b)).astype(BF16)
            st = st * jnp.exp(b_last) + _dot(v.T.astype(BF16), k_hat)
        st_new.append(st)
    a_mask = {u: _hgrn_intra_scores(a_full[u]) for u in units}
    o_intra = {u: _dot(a_mask[u], piece(v_in, *u).astype(BF16)) for u in units}

    def emit_head(hh, o_intra_h):
        o = o_intra_h + jnp.concatenate([o_inter[(hh, c)] for c in range(n_chunks)], axis=0)
        gate = _silu(proj_ref[:, OFF_HG + hh * HG_DV:OFF_HG + (hh + 1) * HG_DV])
        mix_ref[:, ATT_WIDTH + hh * HG_DV:ATT_WIDTH + (hh + 1) * HG_DV] = (
            _rms(o, onorm) * gate).astype(BF16)

    for hh in range(HG_HEADS):
        emit_head(hh, jnp.concatenate([o_intra[(hh, c)] for c in range(n_chunks)], axis=0))
    if with_tail:
        acc = _mlp_block(acc, h2, 1, wup_ref, wdn_ref)
        y_ref[0] = _rms(acc, fnorm_ref[...]) if last else acc

    @pl.when(jnp.min(_hgrn_worst_subblock_decay(b_all)) < -HG_GUARD)
    def _():
        for hh in range(HG_HEADS):
            sl = slice(hh * HG_DK, (hh + 1) * HG_DK)
            emit_head(hh, _hgrn_intra_exact(q_in[hh], k_in[:, sl], v_in[hh], b_all[:, sl],
                                            work_ref, exact_ref))

    kprev_ref[...] = k_rot[tb - WINDOW:]
    vprev_ref[...] = v_new[tb - WINDOW:]
    nk_ref[0] = k_rot[tb - WINDOW:]
    nv_ref[0] = v_new[tb - WINDOW:]
    for hh in range(HG_HEADS):
        st_ref[hh] = st_new[hh]

    @pl.when(t_blk == n_tblk - 1)
    def _():
        for hh in range(HG_HEADS):
            ns_ref[0, hh] = st_new[hh].T


def _prompt_layer_kernel(layer, last, n_tblk, *refs):
    j = pl.program_id(0)
    n_blocks = pl.num_programs(0) - 1
    t_blk = lax.rem(j, n_tblk)

    def run(with_mixer, with_tail):
        _prompt_step(with_mixer, with_tail, layer, last, n_tblk, t_blk, *refs)

    pl.when(j == 0)(functools.partial(run, True, False))
    pl.when((j > 0) & (j < n_blocks))(functools.partial(run, True, True))
    pl.when(j == n_blocks)(functools.partial(run, False, True))


def _const_spec(shape):
    nd = len(shape)
    return pl.BlockSpec(shape, lambda *_: (0,) * nd, pipeline_mode=pl.Buffered(1))


def _layer_spec(layer, shape):
    nd = len(shape)
    return pl.BlockSpec((None,) + shape, lambda *_: (layer,) + (0,) * nd, pipeline_mode=pl.Buffered(1))


def _prompt_layer(layer, x, rope, tri, sinks, lb_all, anorm, w_in, onorm, w_o, mnorm, w_up, w_dn, fnorm):
    batch, seq, _ = x.shape
    tb = PROMPT_BLOCK
    n_tblk = seq // tb
    n_blocks = batch * n_tblk
    last = layer == DEPTH - 1
    mixer_blk = lambda j: jnp.minimum(j, n_blocks - 1)
    tail_blk = lambda j: jnp.maximum(j - 1, 0)
    rope_spec = pl.BlockSpec((tb, LANES), lambda j: (mixer_blk(j) % n_tblk, 0))
    carry_spec = pl.BlockSpec((1, WINDOW, KV_WIDTH), lambda j: (mixer_blk(j) // n_tblk, 0, 0))
    return pl.pallas_call(
        functools.partial(_prompt_layer_kernel, layer, last, n_tblk),
        grid=(n_blocks + 1,),
        in_specs=[
            pl.BlockSpec((1, tb, D_MODEL), lambda j: (mixer_blk(j) // n_tblk, mixer_blk(j) % n_tblk, 0)),
            rope_spec, rope_spec, rope_spec,
            _const_spec((tb, tb)),
            pl.BlockSpec(memory_space=pltpu.SMEM),
            _const_spec((DEPTH, HG_WIDTH)),
            _layer_spec(layer, (1, D_MODEL)),
            _const_spec((D_MODEL, IN_WIDTH)),
            _layer_spec(layer, (1, HG_DV)),
            _const_spec((MIX_WIDTH, D_MODEL)),
            _layer_spec(layer, (1, D_MODEL)),
            _const_spec((D_MODEL, D_FF)),
            _const_spec((D_FF, D_MODEL)),
            _const_spec((1, D_MODEL)),
        ],
        out_specs=[
            pl.BlockSpec((1, tb, D_MODEL), lambda j: (tail_blk(j) // n_tblk, tail_blk(j) % n_tblk, 0)),
            carry_spec, carry_spec,
            pl.BlockSpec((1, HG_HEADS, HG_DK, HG_DV), lambda j: (mixer_blk(j) // n_tblk, 0, 0, 0)),
        ],
        out_shape=[
            jax.ShapeDtypeStruct((batch, seq, D_MODEL), F32),
            jax.ShapeDtypeStruct((batch, WINDOW, KV_WIDTH), F32),
            jax.ShapeDtypeStruct((batch, WINDOW, KV_WIDTH), F32),
            jax.ShapeDtypeStruct((batch, HG_HEADS, HG_DK, HG_DV), F32),
        ],
        scratch_shapes=[
            pltpu.VMEM((tb, IN_WIDTH), F32),
            pltpu.VMEM((tb, MIX_WIDTH), BF16),
            pltpu.VMEM((tb, D_MODEL), F32),
            pltpu.VMEM((WINDOW, KV_WIDTH), F32),
            pltpu.VMEM((WINDOW, KV_WIDTH), F32),
            pltpu.VMEM((HG_HEADS, HG_DV, HG_DK), F32),
            pltpu.VMEM((4, tb, HG_DK), F32),
            pltpu.VMEM((tb, HG_DV), F32),
        ],
        compiler_params=pltpu.CompilerParams(
            dimension_semantics=("arbitrary",), vmem_limit_bytes=VMEM_LIMIT),
        name=f"prompt_layer{layer}",
    )(x, *rope, tri, sinks, lb_all, anorm, w_in, onorm, w_o, mnorm, w_up, w_dn, fnorm)


def _sample_in_kernel(x_ref, anorm_ref, win_ref, proj_ref, win_bf_ref, h_ref):
    @pl.when(pl.program_id(0) == 0)
    def _():
        h_ref[...] = _rms(x_ref[...], anorm_ref[...]).astype(BF16)

    win_bf_ref[...] = win_ref[...].astype(BF16)
    proj_ref[...] = _dot(h_ref[...], win_bf_ref[...])


def _sample_out_kernel(last, x_ref, mix_ref, wo_ref, mnorm_ref, wup_ref, wdn_ref, fnorm_ref,
                       y_ref, wo_bf_ref, wup_bf_ref, wdn_bf_ref, h2_ref, acc_ref):
    c = pl.program_id(0)

    @pl.when(c == 0)
    def _():
        wo_bf_ref[...] = wo_ref[...].astype(BF16)
        acc_ref[...], h2_ref[...] = _out_proj(x_ref[...], mix_ref[...].astype(BF16), wo_bf_ref, mnorm_ref)

    wup_bf_ref[...] = wup_ref[...].astype(BF16)
    wdn_bf_ref[...] = wdn_ref[...].astype(BF16)
    acc_ref[...] = _mlp_cols(acc_ref[...], h2_ref[...], wup_bf_ref[...], wdn_bf_ref[...])

    @pl.when(c == pl.num_programs(0) - 1)
    def _():
        y_ref[...] = _rms(acc_ref[...], fnorm_ref[...]) if last else acc_ref[...]


def _sample_mixer_kernel(layer, n_new, n_aliased, *refs):
    (p_ref, cos_ref, sdn_ref, sup_ref, tri_ref, rev_ref, sel_ref, ck_ref, cv_ref, s0_ref, sinks_ref, lb_ref,
     onorm_ref) = refs[:SAMPLE_MIXER_INPUTS]
    mix_ref, nk_ref, nv_ref, ns_ref, pad_ref = refs[SAMPLE_MIXER_INPUTS + n_aliased:]
    slot = 0 if n_aliased else layer
    for out_ref in (nk_ref, nv_ref, ns_ref):
        for other in range(out_ref.shape[0]):
            if other != slot:
                out_ref[other] = jnp.zeros(out_ref.shape[1:], F32)
    sb = ck_ref.shape[0]
    ts = p_ref.shape[0]
    tok_shift = n_new.bit_length() - 1
    key_shift = WINDOW.bit_length() - 1
    n_cache = sb * WINDOW
    cos, sdn, sup = cos_ref[...], sdn_ref[...], sup_ref[...]

    k_new = _rope(p_ref[:, OFF_K:OFF_K + KV_WIDTH], cos, sdn, sup)
    v_new = p_ref[:, OFF_V:OFF_V + KV_WIDTH]
    pad_keys = jnp.zeros((LANES - ts, KV_WIDTH), F32)
    k_all = jnp.concatenate([ck_ref[...].reshape(n_cache, KV_WIDTH), k_new, pad_keys], axis=0).astype(BF16)
    v_all = jnp.concatenate([cv_ref[...].reshape(n_cache, KV_WIDTH), v_new, pad_keys], axis=0).astype(BF16)
    n_keys = n_cache + LANES
    row = lax.broadcasted_iota(jnp.int32, (ts, n_keys), 0)
    col = lax.broadcasted_iota(jnp.int32, (ts, n_keys), 1)
    new_idx = col - n_cache
    cached_ok = ((col >> key_shift) == (row >> tok_shift)) & ((col & (WINDOW - 1)) >= (row & (n_new - 1)))
    new_ok = ((new_idx < ts) & ((new_idx >> tok_shift) == (row >> tok_shift))
              & ((new_idx & (n_new - 1)) <= (row & (n_new - 1))))
    allowed = ((col < n_cache) & cached_ok) | ((col >= n_cache) & new_ok)

    q_ext = _stacked_queries(p_ref, cos, sdn, sup)
    s_all = _dot_nt(jnp.concatenate(q_ext, axis=0), k_all)
    soft = [_softmax_sink(s_all[head * ts:(head + 1) * ts], allowed, sinks_ref[head])
            for head in range(ATT_HEADS)]
    o_all = _dot(jnp.concatenate([p for p, _ in soft], axis=0), v_all)
    att = [o_all[head * ts:(head + 1) * ts] * soft[head][1] for head in range(ATT_HEADS)]
    for hp in range(ATT_HEADS // 2):
        mix_ref[:, hp * LANES:(hp + 1) * LANES] = _merge_head_pair(
            att[2 * hp], att[2 * hp + 1], (2 * hp) // GROUP)
    for s in range(sb):
        nk_ref[slot, s, 0:WINDOW - n_new] = ck_ref[s, n_new:WINDOW]
        nv_ref[slot, s, 0:WINDOW - n_new] = cv_ref[s, n_new:WINDOW]
        nk_ref[slot, s, WINDOW - n_new:WINDOW] = k_new[s * n_new:(s + 1) * n_new]
        nv_ref[slot, s, WINDOW - n_new:WINDOW] = v_new[s * n_new:(s + 1) * n_new]

    lb = _lower_bound(lb_ref[...], layer)
    log_f, k_in = _hgrn_gates(p_ref[:, OFF_HF:OFF_HF + HG_WIDTH], lb)
    b = _cumsum_rows(tri_ref[...], log_f)
    later = _cumsum_rows(rev_ref[...], log_f)
    q_in = _silu(p_ref[:, OFF_HQ:OFF_HQ + HG_WIDTH])
    v_in = p_ref[:, OFF_HI:OFF_HI + HG_WIDTH]
    gate = _silu(p_ref[:, OFF_HG:OFF_HG + HG_WIDTH])
    onorm = onorm_ref[...]
    heads = [slice(hh * HG_DK, (hh + 1) * HG_DK) for hh in range(HG_HEADS)]

    pad_ref[...] = jnp.zeros_like(pad_ref)
    for i, x in enumerate((k_in, b, v_in)):
        pad_ref[i, SUBLANES:SUBLANES + ts] = x
    tok = lax.broadcasted_iota(jnp.int32, (ts, 1), 0) & (n_new - 1)
    o_intra = [jnp.zeros((ts, HG_DV), F32) for _ in heads]
    for d in range(n_new):
        k_d, b_d, v_d = (pad_ref[i, SUBLANES - d:SUBLANES - d + ts] for i in range(3))
        w = q_in * k_d * jnp.exp(jnp.minimum(b - b_d, 0.0))
        for hh, sl in enumerate(heads):
            a_col = jnp.sum(w[:, sl], axis=-1, keepdims=True)
            o_intra[hh] = o_intra[hh] + jnp.where(tok >= d, a_col, 0.0) * v_d[:, sl]

    q_dec = (q_in * jnp.exp(b)).astype(BF16)
    seq_of_row = lax.broadcasted_iota(jnp.int32, (ts, 1), 0) >> tok_shift
    for hh, sl in enumerate(heads):
        o = o_intra[hh]
        for s in range(sb):
            o_s = _dot(q_dec[:, sl], s0_ref[s, hh].astype(BF16))
            o = o + jnp.where(seq_of_row == s, o_s, 0.0)
        mix_ref[:, ATT_WIDTH + hh * HG_DV:ATT_WIDTH + (hh + 1) * HG_DV] = _rms(o, onorm) * gate[:, sl]

    k_hat = k_in * jnp.exp(later)
    e_b = jnp.exp(b)
    e1 = e_b.astype(BF16).astype(F32)
    e2 = (e_b - e1).astype(BF16).astype(F32)
    e3 = e_b - e1 - e2
    own = ((lax.broadcasted_iota(jnp.int32, (ts, sb * HG_DV), 1) >> key_shift)
           == (lax.broadcasted_iota(jnp.int32, (ts, sb * HG_DV), 0) >> tok_shift))
    no_v = jnp.zeros((3 * ts, sb * HG_DV), F32)
    for hh, sl in enumerate(heads):
        lhs = jnp.concatenate([k_hat[:, sl], e1[:, sl], e2[:, sl], e3[:, sl]], axis=0).T.astype(BF16)
        v_rep = jnp.concatenate([v_in[:, sl]] * sb, axis=1)
        v_diag = jnp.concatenate([jnp.where(own, v_rep, 0.0), no_v], axis=0).astype(BF16)
        upd = _dot(lhs, v_diag)
        decay = _dot(lhs, sel_ref[...])
        for s in range(sb):
            cols = slice(s * HG_DV, (s + 1) * HG_DV)
            ns_ref[slot, s, hh] = decay[:, cols] * s0_ref[s, hh] + upd[:, cols]


def _sample_layer(layer, x, rope, cache_k, cache_v, state, new_caches, sinks, lb_all, anorm, w_in, onorm, w_o,
                  mnorm, w_up, w_dn, fnorm):
    n_seq, n_new, _ = x.shape
    n_tok = n_seq * n_new
    ts = LANES // 4
    sb = ts // n_new
    assert sb * n_new == ts and n_seq % sb == 0 and n_new <= SUBLANES
    assert n_new & (n_new - 1) == 0 and WINDOW & (WINDOW - 1) == 0 and HG_DV == WINDOW
    last = layer == DEPTH - 1
    x2 = x.reshape(n_tok, D_MODEL)
    params = pltpu.CompilerParams(vmem_limit_bytes=VMEM_LIMIT)
    whole = _const_spec
    slab = functools.partial(_layer_spec, layer)

    proj, w_in_bf = pl.pallas_call(
        _sample_in_kernel,
        grid=(IN_WIDTH // SAMPLE_IN_BLOCK,),
        in_specs=[whole((n_tok, D_MODEL)), slab((1, D_MODEL)),
                  pl.BlockSpec((None, D_MODEL, SAMPLE_IN_BLOCK), lambda c: (layer, 0, c))],
        out_specs=[pl.BlockSpec((n_tok, SAMPLE_IN_BLOCK), lambda c: (0, c)),
                   pl.BlockSpec((D_MODEL, SAMPLE_IN_BLOCK), lambda c: (0, c))],
        out_shape=[jax.ShapeDtypeStruct((n_tok, IN_WIDTH), F32),
                   jax.ShapeDtypeStruct((D_MODEL, IN_WIDTH), BF16)],
        scratch_shapes=[pltpu.VMEM((n_tok, D_MODEL), BF16)],
        compiler_params=params,
        name=f"sample_in{layer}",
    )(x2, anorm, w_in)

    r = jnp.arange(ts)
    same_seq = (r[:, None] // n_new) == (r[None, :] // n_new)
    tri = (same_seq & (r[None, :] <= r[:, None])).astype(BF16)
    rev = (same_seq & (r[None, :] > r[:, None])).astype(BF16)
    c = jnp.arange(sb * HG_DV)
    is_last = ((r[:, None] % n_new) == n_new - 1) & ((r[:, None] // n_new) == (c[None, :] // HG_DV))
    sel = jnp.concatenate([jnp.zeros_like(is_last)] + [is_last] * 3, axis=0).astype(BF16)

    tok_spec = lambda width: pl.BlockSpec((ts, width), lambda i: (i, 0))
    seq_spec = lambda *tail: pl.BlockSpec((sb,) + tail, lambda i: (i,) + (0,) * len(tail))
    layer_spec = lambda *tail: pl.BlockSpec((None, sb) + tail, lambda i: (layer, i) + (0,) * len(tail))
    rope_spec = _const_spec((ts, LANES))
    first = not new_caches
    slab_spec = lambda *tail: pl.BlockSpec(
        (DEPTH if first else 1, sb) + tail, lambda i: (0 if first else layer, i) + (0,) * len(tail))
    n_in = SAMPLE_MIXER_INPUTS
    aliases = {n_in + i: 1 + i for i in range(len(new_caches))}
    stacked = [(DEPTH, n_seq, WINDOW, KV_WIDTH)] * 2 + [(DEPTH, n_seq, HG_HEADS, HG_DK, HG_DV)]
    mix, *new_caches = pl.pallas_call(
        functools.partial(_sample_mixer_kernel, layer, n_new, len(new_caches)),
        grid=(n_seq // sb,),
        in_specs=[
            tok_spec(IN_WIDTH),
            rope_spec, rope_spec, rope_spec,
            _const_spec((ts, ts)), _const_spec((ts, ts)), _const_spec((4 * ts, sb * HG_DV)),
            layer_spec(WINDOW, KV_WIDTH), layer_spec(WINDOW, KV_WIDTH),
            layer_spec(HG_HEADS, HG_DK, HG_DV),
            pl.BlockSpec(memory_space=pltpu.SMEM),
            _const_spec((DEPTH, HG_WIDTH)),
            _layer_spec(layer, (1, HG_DV)),
        ] + [pl.BlockSpec(memory_space=pl.ANY)] * len(new_caches),
        out_specs=[
            tok_spec(MIX_WIDTH),
            slab_spec(WINDOW, KV_WIDTH), slab_spec(WINDOW, KV_WIDTH),
            slab_spec(HG_HEADS, HG_DK, HG_DV),
        ],
        out_shape=[jax.ShapeDtypeStruct((n_tok, MIX_WIDTH), F32)]
        + [jax.ShapeDtypeStruct(shape, F32) for shape in stacked],
        input_output_aliases=aliases,
        scratch_shapes=[pltpu.VMEM((3, SUBLANES + ts, HG_WIDTH), F32)],
        compiler_params=pltpu.CompilerParams(
            dimension_semantics=("arbitrary",), vmem_limit_bytes=VMEM_LIMIT),
        name=f"sample_mixer{layer}",
    )(proj, *(jnp.tile(t, (sb, 1)) for t in rope), tri, rev, sel, cache_k, cache_v, state, sinks, lb_all,
      onorm, *new_caches)

    y, w_o_bf, w_up_bf, w_dn_bf = pl.pallas_call(
        functools.partial(_sample_out_kernel, last),
        grid=(D_FF // SAMPLE_FF_BLOCK,),
        in_specs=[whole((n_tok, D_MODEL)), whole((n_tok, MIX_WIDTH)), slab((MIX_WIDTH, D_MODEL)),
                  slab((1, D_MODEL)),
                  pl.BlockSpec((None, D_MODEL, SAMPLE_FF_BLOCK), lambda c: (layer, 0, c)),
                  pl.BlockSpec((None, SAMPLE_FF_BLOCK, D_MODEL), lambda c: (layer, c, 0)),
                  whole((1, D_MODEL))],
        out_specs=[pl.BlockSpec((n_tok, D_MODEL), lambda c: (0, 0)),
                   pl.BlockSpec((MIX_WIDTH, D_MODEL), lambda c: (0, 0)),
                   pl.BlockSpec((D_MODEL, SAMPLE_FF_BLOCK), lambda c: (0, c)),
                   pl.BlockSpec((SAMPLE_FF_BLOCK, D_MODEL), lambda c: (c, 0))],
        out_shape=[jax.ShapeDtypeStruct((n_tok, D_MODEL), F32),
                   jax.ShapeDtypeStruct((MIX_WIDTH, D_MODEL), BF16),
                   jax.ShapeDtypeStruct((D_MODEL, D_FF), BF16),
                   jax.ShapeDtypeStruct((D_FF, D_MODEL), BF16)],
        scratch_shapes=[pltpu.VMEM((n_tok, D_MODEL), BF16), pltpu.VMEM((n_tok, D_MODEL), F32)],
        compiler_params=params,
        name=f"sample_out{layer}",
    )(x2, mix, w_o, mnorm, w_up, w_dn, fnorm)
    return y.reshape(n_seq, n_new, D_MODEL), new_caches, (w_in_bf, w_o_bf, w_up_bf, w_dn_bf)


def _rope_tables(pos):
    n = pos.shape[0]
    inv_freq = jnp.power(ROPE_THETA, -jnp.arange(ROT_HALF, dtype=F32) * (2.0 / ROT_DIM))
    ang = pos.astype(F32)[:, None] * inv_freq[None, :]
    cos, sin = jnp.cos(ang), jnp.sin(ang)
    rest = HEAD_DIM - ROT_DIM
    zeros_h = jnp.zeros((n, ROT_HALF), F32)
    cos_t = jnp.concatenate([cos, cos, jnp.ones((n, rest), F32)], axis=1)
    sdn_t = jnp.concatenate([zeros_h, sin, jnp.zeros((n, rest), F32)], axis=1)
    sup_t = jnp.concatenate([-sin, zeros_h, jnp.zeros((n, rest), F32)], axis=1)
    reps = LANES // HEAD_DIM
    return tuple(jnp.tile(t, (1, reps)) for t in (cos_t, sdn_t, sup_t))


def _chunk_tri(n, chunk):
    r = jnp.arange(n)
    same = (r[:, None] // chunk) == (r[None, :] // chunk)
    return (same & (r[None, :] <= r[:, None])).astype(BF16)


def kernel(x_prompt, x_sample, cache_k, cache_v, state_hgrn, attn_norm, w_in, att_sinks, hgrn_lower_bounds,
           hgrn_out_norm, w_o, mlp_norm, w_up, w_down, final_norm):
    batch, seq, _ = x_prompt.shape
    n_seq, n_new, _ = x_sample.shape
    assert seq % PROMPT_BLOCK == 0 and PROMPT_BLOCK % HG_CHUNK == 0

    rope_p = _rope_tables(jnp.arange(seq, dtype=jnp.int32))
    rope_s = _rope_tables(PAST_LEN + jnp.arange(n_new, dtype=jnp.int32))
    tri = _chunk_tri(PROMPT_BLOCK, HG_CHUNK)
    lb_all = hgrn_lower_bounds.astype(F32)
    fnorm = final_norm.reshape(1, D_MODEL)

    ck = cache_k.reshape(DEPTH, n_seq, WINDOW, KV_WIDTH)
    cv = cache_v.reshape(DEPTH, n_seq, WINDOW, KV_WIDTH)
    anorm, onorm, mnorm = (attn_norm.reshape(DEPTH, 1, D_MODEL), hgrn_out_norm.reshape(DEPTH, 1, HG_DV),
                           mlp_norm.reshape(DEPTH, 1, D_MODEL))
    xp, xs = x_prompt, x_sample
    nk_p, nv_p, ns_p, new_caches = [], [], [], []
    for l in range(DEPTH):
        xs, new_caches, (w_in_bf, w_o_bf, w_up_bf, w_dn_bf) = _sample_layer(
            l, xs, rope_s, ck, cv, state_hgrn, new_caches, att_sinks[l], lb_all, anorm, w_in, onorm, w_o, mnorm,
            w_up, w_down, fnorm)
        xp, k1, v1, s1 = _prompt_layer(l, xp, rope_p, tri, att_sinks[l], lb_all, anorm, w_in_bf, onorm, w_o_bf,
                                       mnorm, w_up_bf, w_dn_bf, fnorm)
        nk_p.append(k1), nv_p.append(v1), ns_p.append(s1)
    nk_s, nv_s, ns_s = new_caches

    kv_p = (DEPTH, batch, WINDOW, ATT_KV_HEADS, HEAD_DIM)
    kv_s = (DEPTH, n_seq, WINDOW, ATT_KV_HEADS, HEAD_DIM)
    return (xp, xs,
            jnp.stack(nk_p).reshape(kv_p), jnp.stack(nv_p).reshape(kv_p), jnp.stack(ns_p),
            nk_s.reshape(kv_s), nv_s.reshape(kv_s), ns_s)
```

```python
import functools

import jax
import jax.numpy as jnp
from jax import lax
from jax.experimental import pallas as pl
from jax.experimental.pallas import tpu as pltpu

F32 = jnp.float32
BF16 = jnp.bfloat16

D_MODEL = 1024
DEPTH = 2
PAST_LEN = 16384
ATT_HEADS = 8
ATT_KV_HEADS = 2
HEAD_DIM = 64
GROUP = ATT_HEADS // ATT_KV_HEADS
ATT_WIDTH = ATT_HEADS * HEAD_DIM
KV_WIDTH = ATT_KV_HEADS * HEAD_DIM
WINDOW = 128
ROT_DIM = HEAD_DIM // 4
ROT_HALF = ROT_DIM // 2
ROPE_THETA = 500000.0
HG_HEADS = 4
HG_DK = 128
HG_DV = 128
HG_WIDTH = HG_HEADS * HG_DK
MIX_WIDTH = ATT_WIDTH + HG_WIDTH
IN_WIDTH = ATT_WIDTH + 2 * KV_WIDTH + 4 * HG_WIDTH
D_FF = 4 * D_MODEL
EPS = 1e-6

OFF_Q = 0
OFF_K = ATT_WIDTH
OFF_V = OFF_K + KV_WIDTH
OFF_HQ = OFF_V + KV_WIDTH
OFF_HF = OFF_HQ + HG_WIDTH
OFF_HI = OFF_HF + HG_WIDTH
OFF_HG = OFF_HI + HG_WIDTH

LANES = 128
PROMPT_BLOCK = 256
HG_CHUNK = 128
HG_SUB = 32
HG_GUARD = 80.0
FF_BLOCK = 2048
SUBLANES = 8
SAMPLE_IN_BLOCK = 1408
SAMPLE_FF_BLOCK = 1024
SAMPLE_MIXER_INPUTS = 13
VMEM_LIMIT = 58 * 1024 * 1024


def _dot(a, b):
    return jnp.dot(a, b, preferred_element_type=F32)


def _dot_nt(a, b):
    return lax.dot_general(a, b, (((1,), (1,)), ((), ())), preferred_element_type=F32)


def _rms(x, g_row):
    ms = jnp.mean(x * x, axis=-1, keepdims=True)
    return (x * lax.rsqrt(ms + EPS)) * g_row


def _rope(x, cos, sin_dn, sin_up):
    return x * cos + pltpu.roll(x, ROT_HALF, 1) * sin_dn + pltpu.roll(x, LANES - ROT_HALF, 1) * sin_up


def _lower_bound(lb_all, layer):
    m = jnp.max(lb_all, axis=0, keepdims=True)
    e = jnp.exp(lb_all - m)
    p = e / jnp.sum(e, axis=0, keepdims=True)
    cs = p[0:1]
    for l in range(1, layer + 1):
        cs = cs + p[l:l + 1]
    return jnp.maximum(cs - p[0:1], 0.0)


def _hgrn_gates(z, lb):
    e = jnp.exp(-jnp.abs(z))
    log_sig = jnp.minimum(z, 0.0) - jnp.log(1.0 + e)
    a1 = jnp.log(lb)
    a2 = jnp.log1p(-lb) + log_sig
    log_f = jnp.maximum(a1, a2) + jnp.log(1.0 + jnp.exp(-jnp.abs(a1 - a2)))
    k_in = (1.0 - lb) * (jnp.where(z >= 0.0, e, 1.0) / (1.0 + e))
    return log_f, k_in


def _silu(x):
    return x * (0.5 + 0.5 * jnp.tanh(0.5 * x))


def _stacked_queries(proj_ref, cos, sdn, sup):
    t = proj_ref.shape[0]
    lo_half = lax.broadcasted_iota(jnp.int32, (t, LANES), 1) < HEAD_DIM
    q_ext = []
    for hp in range(ATT_HEADS // 2):
        q_slab = _rope(proj_ref[:, OFF_Q + hp * LANES:OFF_Q + (hp + 1) * LANES], cos, sdn, sup)
        q_slab = q_slab * (HEAD_DIM ** -0.5)
        q_swap = pltpu.roll(q_slab, HEAD_DIM, 1)
        for sub in range(2):
            kvh = (2 * hp + sub) // GROUP
            src = q_slab if sub == kvh else q_swap
            keep = lo_half if kvh == 0 else ~lo_half
            q_ext.append(jnp.where(keep, src, 0.0).astype(BF16))
    return q_ext


def _softmax_sink(s, allowed, sink):
    s = jnp.where(allowed, s, -jnp.inf)
    m = jnp.maximum(jnp.max(s, axis=-1, keepdims=True), sink)
    p = jnp.exp(s - m)
    inv = 1.0 / (jnp.sum(p, axis=-1, keepdims=True) + jnp.exp(sink - m))
    return p.astype(BF16), inv


def _merge_head_pair(a, b, kvh):
    lo_half = lax.broadcasted_iota(jnp.int32, a.shape, 1) < HEAD_DIM
    if kvh == 0:
        return jnp.where(lo_half, a, pltpu.roll(b, HEAD_DIM, 1))
    return jnp.where(lo_half, pltpu.roll(a, HEAD_DIM, 1), b)


def _cumsum_rows(tri, g):
    g1 = g.astype(BF16)
    r1 = g - g1.astype(F32)
    g2 = r1.astype(BF16)
    g3 = (r1 - g2.astype(F32)).astype(BF16)
    return _dot(tri, g1) + _dot(tri, g2) + _dot(tri, g3)


def _hgrn_intra_operands(q, k, b):
    c = q.shape[0]
    pairs = []
    for i in range(c // HG_SUB):
        lo, hi = i * HG_SUB, (i + 1) * HG_SUB
        r = b[lo - 1:lo] if i > 0 else jnp.zeros((1, HG_DK), F32)
        pairs.append(((q[lo:hi] * jnp.exp(b[lo:hi] - r)).astype(BF16),
                      (k[:hi] * jnp.exp(r - b[:hi])).astype(BF16)))
    return pairs


def _hgrn_worst_subblock_decay(b_all):
    worst = None
    for lo in range(0, b_all.shape[0], HG_SUB):
        tot = b_all[lo + HG_SUB - 1:lo + HG_SUB]
        if lo % HG_CHUNK:
            tot = tot - b_all[lo - 1:lo]
        worst = tot if worst is None else jnp.minimum(worst, tot)
    return worst


def _hgrn_intra_exact(q, k, v, b, work_ref, out_ref):
    for i, x in enumerate((q, k, v, b)):
        work_ref[i] = x
    s_idx = lax.broadcasted_iota(jnp.int32, (HG_CHUNK, 1), 0)

    def row(t, carry):
        c0 = pl.multiple_of((t // HG_CHUNK) * HG_CHUNK, HG_CHUNK)
        q_t = work_ref[0, pl.ds(t, 1), :]
        b_t = work_ref[3, pl.ds(t, 1), :]
        k_c = work_ref[1, pl.ds(c0, HG_CHUNK), :]
        v_c = work_ref[2, pl.ds(c0, HG_CHUNK), :]
        b_c = work_ref[3, pl.ds(c0, HG_CHUNK), :]
        w = k_c * jnp.exp(jnp.minimum(b_t - b_c, 0.0)) * q_t
        a = jnp.where(s_idx <= t - c0, jnp.sum(w, axis=-1, keepdims=True), 0.0)
        out_ref[pl.ds(t, 1), :] = jnp.sum(a * v_c, axis=0, keepdims=True)
        return carry

    lax.fori_loop(0, q.shape[0], row, 0)
    return out_ref[...]


def _hgrn_intra_scores(a_parts):
    c = HG_SUB * len(a_parts)
    rows = [a if a.shape[1] == c else jnp.concatenate([a, jnp.zeros((HG_SUB, c - a.shape[1]), F32)], axis=1)
            for a in a_parts]
    a = jnp.concatenate(rows, axis=0)
    row = lax.broadcasted_iota(jnp.int32, (c, c), 0)
    col = lax.broadcasted_iota(jnp.int32, (c, c), 1)
    return jnp.where(col <= row, a, 0.0).astype(BF16)


def _out_proj(x, mix_bf16, wo_ref, mnorm_ref):
    x1 = x + _dot(mix_bf16, wo_ref[...])
    return x1, _rms(x1, mnorm_ref[...]).astype(BF16)


def _mlp_cols(acc, h2, w_up_cols, w_dn_rows):
    u = jnp.square(jnp.maximum(_dot(h2, w_up_cols), 0.0)).astype(BF16)
    return acc + _dot(u, w_dn_rows)


def _mlp_block(acc, h2, c, wup_ref, wdn_ref):
    return _mlp_cols(acc, h2, wup_ref[:, c * FF_BLOCK:(c + 1) * FF_BLOCK],
                     wdn_ref[c * FF_BLOCK:(c + 1) * FF_BLOCK, :])


def _prompt_step(with_mixer, with_tail, layer, last, n_tblk, t_blk,
                 x_ref, cos_ref, sdn_ref, sup_ref, tri_ref, sinks_ref, lb_ref, anorm_ref,
                 win_ref, onorm_ref, wo_ref, mnorm_ref, wup_ref, wdn_ref, fnorm_ref,
                 y_ref, nk_ref, nv_ref, ns_ref,
                 proj_ref, mix_ref, xprev_ref, kprev_ref, vprev_ref, st_ref, work_ref, exact_ref):
    tb = PROMPT_BLOCK
    if with_mixer:
        @pl.when(t_blk == 0)
        def _():
            kprev_ref[...] = jnp.zeros_like(kprev_ref)
            vprev_ref[...] = jnp.zeros_like(vprev_ref)
            st_ref[...] = jnp.zeros_like(st_ref)

    if with_tail:
        acc, h2 = _out_proj(xprev_ref[...], mix_ref[...], wo_ref, mnorm_ref)
    if not with_mixer:
        for c in range(D_FF // FF_BLOCK):
            acc = _mlp_block(acc, h2, c, wup_ref, wdn_ref)
        y_ref[0] = _rms(acc, fnorm_ref[...]) if last else acc
        return

    x = x_ref[0]
    xprev_ref[...] = x
    h = _rms(x, anorm_ref[...]).astype(BF16)
    proj_ref[...] = _dot(h, win_ref[...])

    cos, sdn, sup = cos_ref[...], sdn_ref[...], sup_ref[...]

    k_rot = _rope(proj_ref[:, OFF_K:OFF_K + KV_WIDTH], cos, sdn, sup)
    v_new = proj_ref[:, OFF_V:OFF_V + KV_WIDTH]
    k_all = jnp.concatenate([kprev_ref[...], k_rot], axis=0).astype(BF16)
    v_all = jnp.concatenate([vprev_ref[...], v_new], axis=0).astype(BF16)

    q_ext = _stacked_queries(proj_ref, cos, sdn, sup)
    row = lax.broadcasted_iota(jnp.int32, (WINDOW, 2 * WINDOW), 0)
    col = lax.broadcasted_iota(jnp.int32, (WINDOW, 2 * WINDOW), 1)
    in_window = (col >= row) & (col <= row + WINDOW)
    first_lo = jnp.where(t_blk == 0, WINDOW, 0)
    att = [[None] * (tb // WINDOW) for _ in range(ATT_HEADS)]
    for n in range(tb // WINDOW):
        rows_n = slice(n * WINDOW, (n + 1) * WINDOW)
        keys_n = slice(n * WINDOW, (n + 2) * WINDOW)
        q_stack = jnp.concatenate([q[rows_n] for q in q_ext], axis=0)
        s_all = _dot_nt(q_stack, k_all[keys_n])
        allowed = in_window & (col >= first_lo) if n == 0 else in_window
        soft = [_softmax_sink(s_all[head * WINDOW:(head + 1) * WINDOW], allowed, sinks_ref[head])
                for head in range(ATT_HEADS)]
        o_all = _dot(jnp.concatenate([p for p, _ in soft], axis=0), v_all[keys_n])
        for head in range(ATT_HEADS):
            att[head][n] = o_all[head * WINDOW:(head + 1) * WINDOW] * soft[head][1]
    for hp in range(ATT_HEADS // 2):
        a, b = (jnp.concatenate(att[2 * hp + sub], axis=0) for sub in range(2))
        mix_ref[:, hp * LANES:(hp + 1) * LANES] = _merge_head_pair(a, b, (2 * hp) // GROUP).astype(BF16)
    if with_tail:
        acc = _mlp_block(acc, h2, 0, wup_ref, wdn_ref)

    lb = _lower_bound(lb_ref[...], layer)
    log_f, k_in = _hgrn_gates(proj_ref[:, OFF_HF:OFF_HF + HG_WIDTH], lb)
    b_all = _cumsum_rows(tri_ref[...], log_f)
    onorm = onorm_ref[...]
    n_chunks = tb // HG_CHUNK
    units = [(hh, c) for c in range(n_chunks) for hh in range(HG_HEADS)]
    q_in, v_in = [], []
    for hh in range(HG_HEADS):
        q_in.append(_silu(proj_ref[:, OFF_HQ + hh * HG_DK:OFF_HQ + (hh + 1) * HG_DK]))
        v_in.append(proj_ref[:, OFF_HI + hh * HG_DV:OFF_HI + (hh + 1) * HG_DV])

    def piece(x, hh, c, lanes=False):
        x = x[:, hh * HG_DK:(hh + 1) * HG_DK] if lanes else x[hh]
        return x[c * HG_CHUNK:(c + 1) * HG_CHUNK]

    intra_ops = {u: _hgrn_intra_operands(piece(q_in, *u), piece(k_in, *u, lanes=True),
                                         piece(b_all, *u, lanes=True)) for u in units}
    a_full = {u: [_dot_nt(qh, kh) for qh, kh in intra_ops[u]] for u in units}
    o_inter, st_new = {}, []
    for hh in range(HG_HEADS):
        st = st_ref[hh]
        for c in range(n_chunks):
            q, k, v, b = (piece(q_in, hh, c), piece(k_in, hh, c, True), piece(v_in, hh, c),
                          piece(b_all, hh, c, True))
            o_inter[(hh, c)] = _dot_nt((q * jnp.exp(b)).astype(BF16), st.astype(BF16))
            b_last = b[HG_CHUNK - 1:HG_CHUNK]
            k_hat = (k * jnp.exp(b_last - b)).astype(BF16)
            st = st * jnp.exp(b_last) + _dot(v.T.astype(BF16), k_hat)
        st_new.append(st)
    a_mask = {u: _hgrn_intra_scores(a_full[u]) for u in units}
    o_intra = {u: _dot(a_mask[u], piece(v_in, *u).astype(BF16)) for u in units}

    def emit_head(hh, o_intra_h):
        o = o_intra_h + jnp.concatenate([o_inter[(hh, c)] for c in range(n_chunks)], axis=0)
        gate = _silu(proj_ref[:, OFF_HG + hh * HG_DV:OFF_HG + (hh + 1) * HG_DV])
        mix_ref[:, ATT_WIDTH + hh * HG_DV:ATT_WIDTH + (hh + 1) * HG_DV] = (
            _rms(o, onorm) * gate).astype(BF16)

    for hh in range(HG_HEADS):
        emit_head(hh, jnp.concatenate([o_intra[(hh, c)] for c in range(n_chunks)], axis=0))
    if with_tail:
        acc = _mlp_block(acc, h2, 1, wup_ref, wdn_ref)
        y_ref[0] = _rms(acc, fnorm_ref[...]) if last else acc

    @pl.when(jnp.min(_hgrn_worst_subblock_decay(b_all)) < -HG_GUARD)
    def _():
        for hh in range(HG_HEADS):
            sl = slice(hh * HG_DK, (hh + 1) * HG_DK)
            emit_head(hh, _hgrn_intra_exact(q_in[hh], k_in[:, sl], v_in[hh], b_all[:, sl],
                                            work_ref, exact_ref))

    kprev_ref[...] = k_rot[tb - WINDOW:]
    vprev_ref[...] = v_new[tb - WINDOW:]
    nk_ref[0] = k_rot[tb - WINDOW:]
    nv_ref[0] = v_new[tb - WINDOW:]
    for hh in range(HG_HEADS):
        st_ref[hh] = st_new[hh]

    @pl.when(t_blk == n_tblk - 1)
    def _():
        for hh in range(HG_HEADS):
            ns_ref[0, hh] = st_new[hh].T


def _prompt_layer_kernel(layer, last, n_tblk, *refs):
    j = pl.program_id(0)
    n_blocks = pl.num_programs(0) - 1
    t_blk = lax.rem(j, n_tblk)

    def run(with_mixer, with_tail):
        _prompt_step(with_mixer, with_tail, layer, last, n_tblk, t_blk, *refs)

    pl.when(j == 0)(functools.partial(run, True, False))
    pl.when((j > 0) & (j < n_blocks))(functools.partial(run, True, True))
    pl.when(j == n_blocks)(functools.partial(run, False, True))


def _const_spec(shape):
    nd = len(shape)
    return pl.BlockSpec(shape, lambda *_: (0,) * nd, pipeline_mode=pl.Buffered(1))


def _layer_spec(layer, shape):
    nd = len(shape)
    return pl.BlockSpec((None,) + shape, lambda *_: (layer,) + (0,) * nd, pipeline_mode=pl.Buffered(1))


def _prompt_layer(layer, x, rope, tri, sinks, lb_all, anorm, w_in, onorm, w_o, mnorm, w_up, w_dn, fnorm):
    batch, seq, _ = x.shape
    tb = PROMPT_BLOCK
    n_tblk = seq // tb
    n_blocks = batch * n_tblk
    last = layer == DEPTH - 1
    mixer_blk = lambda j: jnp.minimum(j, n_blocks - 1)
    tail_blk = lambda j: jnp.maximum(j - 1, 0)
    rope_spec = pl.BlockSpec((tb, LANES), lambda j: (mixer_blk(j) % n_tblk, 0))
    carry_spec = pl.BlockSpec((1, WINDOW, KV_WIDTH), lambda j: (mixer_blk(j) // n_tblk, 0, 0))
    return pl.pallas_call(
        functools.partial(_prompt_layer_kernel, layer, last, n_tblk),
        grid=(n_blocks + 1,),
        in_specs=[
            pl.BlockSpec((1, tb, D_MODEL), lambda j: (mixer_blk(j) // n_tblk, mixer_blk(j) % n_tblk, 0)),
            rope_spec, rope_spec, rope_spec,
            _const_spec((tb, tb)),
            pl.BlockSpec(memory_space=pltpu.SMEM),
            _const_spec((DEPTH, HG_WIDTH)),
            _layer_spec(layer, (1, D_MODEL)),
            _const_spec((D_MODEL, IN_WIDTH)),
            _layer_spec(layer, (1, HG_DV)),
            _const_spec((MIX_WIDTH, D_MODEL)),
            _layer_spec(layer, (1, D_MODEL)),
            _const_spec((D_MODEL, D_FF)),
            _const_spec((D_FF, D_MODEL)),
            _const_spec((1, D_MODEL)),
        ],
        out_specs=[
            pl.BlockSpec((1, tb, D_MODEL), lambda j: (tail_blk(j) // n_tblk, tail_blk(j) % n_tblk, 0)),
            carry_spec, carry_spec,
            pl.BlockSpec((1, HG_HEADS, HG_DK, HG_DV), lambda j: (mixer_blk(j) // n_tblk, 0, 0, 0)),
        ],
        out_shape=[
            jax.ShapeDtypeStruct((batch, seq, D_MODEL), F32),
            jax.ShapeDtypeStruct((batch, WINDOW, KV_WIDTH), F32),
            jax.ShapeDtypeStruct((batch, WINDOW, KV_WIDTH), F32),
            jax.ShapeDtypeStruct((batch, HG_HEADS, HG_DK, HG_DV), F32),
        ],
        scratch_shapes=[
            pltpu.VMEM((tb, IN_WIDTH), F32),
            pltpu.VMEM((tb, MIX_WIDTH), BF16),
            pltpu.VMEM((tb, D_MODEL), F32),
            pltpu.VMEM((WINDOW, KV_WIDTH), F32),
            pltpu.VMEM((WINDOW, KV_WIDTH), F32),
            pltpu.VMEM((HG_HEADS, HG_DV, HG_DK), F32),
            pltpu.VMEM((4, tb, HG_DK), F32),
            pltpu.VMEM((tb, HG_DV), F32),
        ],
        compiler_params=pltpu.CompilerParams(
            dimension_semantics=("arbitrary",), vmem_limit_bytes=VMEM_LIMIT),
        name=f"prompt_layer{layer}",
    )(x, *rope, tri, sinks, lb_all, anorm, w_in, onorm, w_o, mnorm, w_up, w_dn, fnorm)


def _sample_in_kernel(x_ref, anorm_ref, win_ref, proj_ref, win_bf_ref, h_ref):
    @pl.when(pl.program_id(0) == 0)
    def _():
        h_ref[...] = _rms(x_ref[...], anorm_ref[...]).astype(BF16)

    win_bf_ref[...] = win_ref[...].astype(BF16)
    proj_ref[...] = _dot(h_ref[...], win_bf_ref[...])


def _sample_out_kernel(last, x_ref, mix_ref, wo_ref, mnorm_ref, wup_ref, wdn_ref, fnorm_ref,
                       y_ref, wo_bf_ref, wup_bf_ref, wdn_bf_ref, h2_ref, acc_ref):
    c = pl.program_id(0)

    @pl.when(c == 0)
    def _():
        wo_bf_ref[...] = wo_ref[...].astype(BF16)
        acc_ref[...], h2_ref[...] = _out_proj(x_ref[...], mix_ref[...].astype(BF16), wo_bf_ref, mnorm_ref)

    wup_bf_ref[...] = wup_ref[...].astype(BF16)
    wdn_bf_ref[...] = wdn_ref[...].astype(BF16)
    acc_ref[...] = _mlp_cols(acc_ref[...], h2_ref[...], wup_bf_ref[...], wdn_bf_ref[...])

    @pl.when(c == pl.num_programs(0) - 1)
    def _():
        y_ref[...] = _rms(acc_ref[...], fnorm_ref[...]) if last else acc_ref[...]


def _sample_mixer_kernel(layer, n_new, n_aliased, *refs):
    (p_ref, cos_ref, sdn_ref, sup_ref, tri_ref, rev_ref, sel_ref, ck_ref, cv_ref, s0_ref, sinks_ref, lb_ref,
     onorm_ref) = refs[:SAMPLE_MIXER_INPUTS]
    mix_ref, nk_ref, nv_ref, ns_ref, pad_ref = refs[SAMPLE_MIXER_INPUTS + n_aliased:]
    slot = 0 if n_aliased else layer
    for out_ref in (nk_ref, nv_ref, ns_ref):
        for other in range(out_ref.shape[0]):
            if other != slot:
                out_ref[other] = jnp.zeros(out_ref.shape[1:], F32)
    sb = ck_ref.shape[0]
    ts = p_ref.shape[0]
    tok_shift = n_new.bit_length() - 1
    key_shift = WINDOW.bit_length() - 1
    n_cache = sb * WINDOW
    cos, sdn, sup = cos_ref[...], sdn_ref[...], sup_ref[...]

    k_new = _rope(p_ref[:, OFF_K:OFF_K + KV_WIDTH], cos, sdn, sup)
    v_new = p_ref[:, OFF_V:OFF_V + KV_WIDTH]
    pad_keys = jnp.zeros((LANES - ts, KV_WIDTH), F32)
    k_all = jnp.concatenate([ck_ref[...].reshape(n_cache, KV_WIDTH), k_new, pad_keys], axis=0).astype(BF16)
    v_all = jnp.concatenate([cv_ref[...].reshape(n_cache, KV_WIDTH), v_new, pad_keys], axis=0).astype(BF16)
    n_keys = n_cache + LANES
    row = lax.broadcasted_iota(jnp.int32, (ts, n_keys), 0)
    col = lax.broadcasted_iota(jnp.int32, (ts, n_keys), 1)
    new_idx = col - n_cache
    cached_ok = ((col >> key_shift) == (row >> tok_shift)) & ((col & (WINDOW - 1)) >= (row & (n_new - 1)))
    new_ok = ((new_idx < ts) & ((new_idx >> tok_shift) == (row >> tok_shift))
              & ((new_idx & (n_new - 1)) <= (row & (n_new - 1))))
    allowed = ((col < n_cache) & cached_ok) | ((col >= n_cache) & new_ok)

    q_ext = _stacked_queries(p_ref, cos, sdn, sup)
    s_all = _dot_nt(jnp.concatenate(q_ext, axis=0), k_all)
    soft = [_softmax_sink(s_all[head * ts:(head + 1) * ts], allowed, sinks_ref[head])
            for head in range(ATT_HEADS)]
    o_all = _dot(jnp.concatenate([p for p, _ in soft], axis=0), v_all)
    att = [o_all[head * ts:(head + 1) * ts] * soft[head][1] for head in range(ATT_HEADS)]
    for hp in range(ATT_HEADS // 2):
        mix_ref[:, hp * LANES:(hp + 1) * LANES] = _merge_head_pair(
            att[2 * hp], att[2 * hp + 1], (2 * hp) // GROUP)
    for s in range(sb):
        nk_ref[slot, s, 0:WINDOW - n_new] = ck_ref[s, n_new:WINDOW]
        nv_ref[slot, s, 0:WINDOW - n_new] = cv_ref[s, n_new:WINDOW]
        nk_ref[slot, s, WINDOW - n_new:WINDOW] = k_new[s * n_new:(s + 1) * n_new]
        nv_ref[slot, s, WINDOW - n_new:WINDOW] = v_new[s * n_new:(s + 1) * n_new]

    lb = _lower_bound(lb_ref[...], layer)
    log_f, k_in = _hgrn_gates(p_ref[:, OFF_HF:OFF_HF + HG_WIDTH], lb)
    b = _cumsum_rows(tri_ref[...], log_f)
    later = _cumsum_rows(rev_ref[...], log_f)
    q_in = _silu(p_ref[:, OFF_HQ:OFF_HQ + HG_WIDTH])
    v_in = p_ref[:, OFF_HI:OFF_HI + HG_WIDTH]
    gate = _silu(p_ref[:, OFF_HG:OFF_HG + HG_WIDTH])
    onorm = onorm_ref[...]
    heads = [slice(hh * HG_DK, (hh + 1) * HG_DK) for hh in range(HG_HEADS)]

    pad_ref[...] = jnp.zeros_like(pad_ref)
    for i, x in enumerate((k_in, b, v_in)):
        pad_ref[i, SUBLANES:SUBLANES + ts] = x
    tok = lax.broadcasted_iota(jnp.int32, (ts, 1), 0) & (n_new - 1)
    o_intra = [jnp.zeros((ts, HG_DV), F32) for _ in heads]
    for d in range(n_new):
        k_d, b_d, v_d = (pad_ref[i, SUBLANES - d:SUBLANES - d + ts] for i in range(3))
        w = q_in * k_d * jnp.exp(jnp.minimum(b - b_d, 0.0))
        for hh, sl in enumerate(heads):
            a_col = jnp.sum(w[:, sl], axis=-1, keepdims=True)
            o_intra[hh] = o_intra[hh] + jnp.where(tok >= d, a_col, 0.0) * v_d[:, sl]

    q_dec = (q_in * jnp.exp(b)).astype(BF16)
    seq_of_row = lax.broadcasted_iota(jnp.int32, (ts, 1), 0) >> tok_shift
    for hh, sl in enumerate(heads):
        o = o_intra[hh]
        for s in range(sb):
            o_s = _dot(q_dec[:, sl], s0_ref[s, hh].astype(BF16))
            o = o + jnp.where(seq_of_row == s, o_s, 0.0)
        mix_ref[:, ATT_WIDTH + hh * HG_DV:ATT_WIDTH + (hh + 1) * HG_DV] = _rms(o, onorm) * gate[:, sl]

    k_hat = k_in * jnp.exp(later)
    e_b = jnp.exp(b)
    e1 = e_b.astype(BF16).astype(F32)
    e2 = (e_b - e1).astype(BF16).astype(F32)
    e3 = e_b - e1 - e2
    own = ((lax.broadcasted_iota(jnp.int32, (ts, sb * HG_DV), 1) >> key_shift)
           == (lax.broadcasted_iota(jnp.int32, (ts, sb * HG_DV), 0) >> tok_shift))
    no_v = jnp.zeros((3 * ts, sb * HG_DV), F32)
    for hh, sl in enumerate(heads):
        lhs = jnp.concatenate([k_hat[:, sl], e1[:, sl], e2[:, sl], e3[:, sl]], axis=0).T.astype(BF16)
        v_rep = jnp.concatenate([v_in[:, sl]] * sb, axis=1)
        v_diag = jnp.concatenate([jnp.where(own, v_rep, 0.0), no_v], axis=0).astype(BF16)
        upd = _dot(lhs, v_diag)
        decay = _dot(lhs, sel_ref[...])
        for s in range(sb):
            cols = slice(s * HG_DV, (s + 1) * HG_DV)
            ns_ref[slot, s, hh] = decay[:, cols] * s0_ref[s, hh] + upd[:, cols]


def _sample_layer(layer, x, rope, cache_k, cache_v, state, new_caches, sinks, lb_all, anorm, w_in, onorm, w_o,
                  mnorm, w_up, w_dn, fnorm):
    n_seq, n_new, _ = x.shape
    n_tok = n_seq * n_new
    ts = LANES // 4
    sb = ts // n_new
    assert sb * n_new == ts and n_seq % sb == 0 and n_new <= SUBLANES
    assert n_new & (n_new - 1) == 0 and WINDOW & (WINDOW - 1) == 0 and HG_DV == WINDOW
    last = layer == DEPTH - 1
    x2 = x.reshape(n_tok, D_MODEL)
    params = pltpu.CompilerParams(vmem_limit_bytes=VMEM_LIMIT)
    whole = _const_spec
    slab = functools.partial(_layer_spec, layer)

    proj, w_in_bf = pl.pallas_call(
        _sample_in_kernel,
        grid=(IN_WIDTH // SAMPLE_IN_BLOCK,),
        in_specs=[whole((n_tok, D_MODEL)), slab((1, D_MODEL)),
                  pl.BlockSpec((None, D_MODEL, SAMPLE_IN_BLOCK), lambda c: (layer, 0, c))],
        out_specs=[pl.BlockSpec((n_tok, SAMPLE_IN_BLOCK), lambda c: (0, c)),
                   pl.BlockSpec((D_MODEL, SAMPLE_IN_BLOCK), lambda c: (0, c))],
        out_shape=[jax.ShapeDtypeStruct((n_tok, IN_WIDTH), F32),
                   jax.ShapeDtypeStruct((D_MODEL, IN_WIDTH), BF16)],
        scratch_shapes=[pltpu.VMEM((n_tok, D_MODEL), BF16)],
        compiler_params=params,
        name=f"sample_in{layer}",
    )(x2, anorm, w_in)

    r = jnp.arange(ts)
    same_seq = (r[:, None] // n_new) == (r[None, :] // n_new)
    tri = (same_seq & (r[None, :] <= r[:, None])).astype(BF16)
    rev = (same_seq & (r[None, :] > r[:, None])).astype(BF16)
    c = jnp.arange(sb * HG_DV)
    is_last = ((r[:, None] % n_new) == n_new - 1) & ((r[:, None] // n_new) == (c[None, :] // HG_DV))
    sel = jnp.concatenate([jnp.zeros_like(is_last)] + [is_last] * 3, axis=0).astype(BF16)

    tok_spec = lambda width: pl.BlockSpec((ts, width), lambda i: (i, 0))
    seq_spec = lambda *tail: pl.BlockSpec((sb,) + tail, lambda i: (i,) + (0,) * len(tail))
    layer_spec = lambda *tail: pl.BlockSpec((None, sb) + tail, lambda i: (layer, i) + (0,) * len(tail))
    rope_spec = _const_spec((ts, LANES))
    first = not new_caches
    slab_spec = lambda *tail: pl.BlockSpec(
        (DEPTH if first else 1, sb) + tail, lambda i: (0 if first else layer, i) + (0,) * len(tail))
    n_in = SAMPLE_MIXER_INPUTS
    aliases = {n_in + i: 1 + i for i in range(len(new_caches))}
    stacked = [(DEPTH, n_seq, WINDOW, KV_WIDTH)] * 2 + [(DEPTH, n_seq, HG_HEADS, HG_DK, HG_DV)]
    mix, *new_caches = pl.pallas_call(
        functools.partial(_sample_mixer_kernel, layer, n_new, len(new_caches)),
        grid=(n_seq // sb,),
        in_specs=[
            tok_spec(IN_WIDTH),
            rope_spec, rope_spec, rope_spec,
            _const_spec((ts, ts)), _const_spec((ts, ts)), _const_spec((4 * ts, sb * HG_DV)),
            layer_spec(WINDOW, KV_WIDTH), layer_spec(WINDOW, KV_WIDTH),
            layer_spec(HG_HEADS, HG_DK, HG_DV),
            pl.BlockSpec(memory_space=pltpu.SMEM),
            _const_spec((DEPTH, HG_WIDTH)),
            _layer_spec(layer, (1, HG_DV)),
        ] + [pl.BlockSpec(memory_space=pl.ANY)] * len(new_caches),
        out_specs=[
            tok_spec(MIX_WIDTH),
            slab_spec(WINDOW, KV_WIDTH), slab_spec(WINDOW, KV_WIDTH),
            slab_spec(HG_HEADS, HG_DK, HG_DV),
        ],
        out_shape=[jax.ShapeDtypeStruct((n_tok, MIX_WIDTH), F32)]
        + [jax.ShapeDtypeStruct(shape, F32) for shape in stacked],
        input_output_aliases=aliases,
        scratch_shapes=[pltpu.VMEM((3, SUBLANES + ts, HG_WIDTH), F32)],
        compiler_params=pltpu.CompilerParams(
            dimension_semantics=("arbitrary",), vmem_limit_bytes=VMEM_LIMIT),
        name=f"sample_mixer{layer}",
    )(proj, *(jnp.tile(t, (sb, 1)) for t in rope), tri, rev, sel, cache_k, cache_v, state, sinks, lb_all,
      onorm, *new_caches)

    y, w_o_bf, w_up_bf, w_dn_bf = pl.pallas_call(
        functools.partial(_sample_out_kernel, last),
        grid=(D_FF // SAMPLE_FF_BLOCK,),
        in_specs=[whole((n_tok, D_MODEL)), whole((n_tok, MIX_WIDTH)), slab((MIX_WIDTH, D_MODEL)),
                  slab((1, D_MODEL)),
                  pl.BlockSpec((None, D_MODEL, SAMPLE_FF_BLOCK), lambda c: (layer, 0, c)),
                  pl.BlockSpec((None, SAMPLE_FF_BLOCK, D_MODEL), lambda c: (layer, c, 0)),
                  whole((1, D_MODEL))],
        out_specs=[pl.BlockSpec((n_tok, D_MODEL), lambda c: (0, 0)),
                   pl.BlockSpec((MIX_WIDTH, D_MODEL), lambda c: (0, 0)),
                   pl.BlockSpec((D_MODEL, SAMPLE_FF_BLOCK), lambda c: (0, c)),
                   pl.BlockSpec((SAMPLE_FF_BLOCK, D_MODEL), lambda c: (c, 0))],
        out_shape=[jax.ShapeDtypeStruct((n_tok, D_MODEL), F32),
                   jax.ShapeDtypeStruct((MIX_WIDTH, D_MODEL), BF16),
                   jax.ShapeDtypeStruct((D_MODEL, D_FF), BF16),
                   jax.ShapeDtypeStruct((D_FF, D_MODEL), BF16)],
        scratch_shapes=[pltpu.VMEM((n_tok, D_MODEL), BF16), pltpu.VMEM((n_tok, D_MODEL), F32)],
        compiler_params=params,
        name=f"sample_out{layer}",
    )(x2, mix, w_o, mnorm, w_up, w_dn, fnorm)
    return y.reshape(n_seq, n_new, D_MODEL), new_caches, (w_in_bf, w_o_bf, w_up_bf, w_dn_bf)


def _rope_tables(pos):
    n = pos.shape[0]
    inv_freq = jnp.power(ROPE_THETA, -jnp.arange(ROT_HALF, dtype=F32) * (2.0 / ROT_DIM))
    ang = pos.astype(F32)[:, None] * inv_freq[None, :]
    cos, sin = jnp.cos(ang), jnp.sin(ang)
    rest = HEAD_DIM - ROT_DIM
    zeros_h = jnp.zeros((n, ROT_HALF), F32)
    cos_t = jnp.concatenate([cos, cos, jnp.ones((n, rest), F32)], axis=1)
    sdn_t = jnp.concatenate([zeros_h, sin, jnp.zeros((n, rest), F32)], axis=1)
    sup_t = jnp.concatenate([-sin, zeros_h, jnp.zeros((n, rest), F32)], axis=1)
    reps = LANES // HEAD_DIM
    return tuple(jnp.tile(t, (1, reps)) for t in (cos_t, sdn_t, sup_t))


def _chunk_tri(n, chunk):
    r = jnp.arange(n)
    same = (r[:, None] // chunk) == (r[None, :] // chunk)
    return (same & (r[None, :] <= r[:, None])).astype(BF16)


def kernel(x_prompt, x_sample, cache_k, cache_v, state_hgrn, attn_norm, w_in, att_sinks, hgrn_lower_bounds,
           hgrn_out_norm, w_o, mlp_norm, w_up, w_down, final_norm):
    batch, seq, _ = x_prompt.shape
    n_seq, n_new, _ = x_sample.shape
    assert seq % PROMPT_BLOCK == 0 and PROMPT_BLOCK % HG_CHUNK == 0

    rope_p = _rope_tables(jnp.arange(seq, dtype=jnp.int32))
    rope_s = _rope_tables(PAST_LEN + jnp.arange(n_new, dtype=jnp.int32))
    tri = _chunk_tri(PROMPT_BLOCK, HG_CHUNK)
    lb_all = hgrn_lower_bounds.astype(F32)
    fnorm = final_norm.reshape(1, D_MODEL)

    ck = cache_k.reshape(DEPTH, n_seq, WINDOW, KV_WIDTH)
    cv = cache_v.reshape(DEPTH, n_seq, WINDOW, KV_WIDTH)
    anorm, onorm, mnorm = (attn_norm.reshape(DEPTH, 1, D_MODEL), hgrn_out_norm.reshape(DEPTH, 1, HG_DV),
                           mlp_norm.reshape(DEPTH, 1, D_MODEL))
    xp, xs = x_prompt, x_sample
    nk_p, nv_p, ns_p, new_caches = [], [], [], []
    for l in range(DEPTH):
        xs, new_caches, (w_in_bf, w_o_bf, w_up_bf, w_dn_bf) = _sample_layer(
            l, xs, rope_s, ck, cv, state_hgrn, new_caches, att_sinks[l], lb_all, anorm, w_in, onorm, w_o, mnorm,
            w_up, w_down, fnorm)
        xp, k1, v1, s1 = _prompt_layer(l, xp, rope_p, tri, att_sinks[l], lb_all, anorm, w_in_bf, onorm, w_o_bf,
                                       mnorm, w_up_bf, w_dn_bf, fnorm)
        nk_p.append(k1), nv_p.append(v1), ns_p.append(s1)
    nk_s, nv_s, ns_s = new_caches

    kv_p = (DEPTH, batch, WINDOW, ATT_KV_HEADS, HEAD_DIM)
    kv_s = (DEPTH, n_seq, WINDOW, ATT_KV_HEADS, HEAD_DIM)
    return (xp, xs,
            jnp.stack(nk_p).reshape(kv_p), jnp.stack(nv_p).reshape(kv_p), jnp.stack(ns_p),
            nk_s.reshape(kv_s), nv_s.reshape(kv_s), ns_s)
```

```python
import functools

import jax
import jax.numpy as jnp
from jax import lax
from jax.experimental import pallas as pl
from jax.experimental.pallas import tpu as pltpu

F32 = jnp.float32
BF16 = jnp.bfloat16

D_MODEL = 1024
DEPTH = 2
PAST_LEN = 16384
ATT_HEADS = 8
ATT_KV_HEADS = 2
HEAD_DIM = 64
GROUP = ATT_HEADS // ATT_KV_HEADS
ATT_WIDTH = ATT_HEADS * HEAD_DIM
KV_WIDTH = ATT_KV_HEADS * HEAD_DIM
WINDOW = 128
ROT_DIM = HEAD_DIM // 4
ROT_HALF = ROT_DIM // 2
ROPE_THETA = 500000.0
HG_HEADS = 4
HG_DK = 128
HG_DV = 128
HG_WIDTH = HG_HEADS * HG_DK
MIX_WIDTH = ATT_WIDTH + HG_WIDTH
IN_WIDTH = ATT_WIDTH + 2 * KV_WIDTH + 4 * HG_WIDTH
D_FF = 4 * D_MODEL
EPS = 1e-6

OFF_Q = 0
OFF_K = ATT_WIDTH
OFF_V = OFF_K + KV_WIDTH
OFF_HQ = OFF_V + KV_WIDTH
OFF_HF = OFF_HQ + HG_WIDTH
OFF_HI = OFF_HF + HG_WIDTH
OFF_HG = OFF_HI + HG_WIDTH

LANES = 128
PROMPT_BLOCK = 256
HG_CHUNK = 128
HG_SUB = 32
HG_GUARD = 80.0
FF_BLOCK = 2048
SUBLANES = 8
SAMPLE_IN_BLOCK = 1408
SAMPLE_FF_BLOCK = 1024
SAMPLE_MIXER_INPUTS = 13
VMEM_LIMIT = 58 * 1024 * 1024


def _dot(a, b):
    return jnp.dot(a, b, preferred_element_type=F32)


def _dot_nt(a, b):
    return lax.dot_general(a, b, (((1,), (1,)), ((), ())), preferred_element_type=F32)


def _rms(x, g_row):
    ms = jnp.mean(x * x, axis=-1, keepdims=True)
    return (x * lax.rsqrt(ms + EPS)) * g_row


def _rope(x, cos, sin_dn, sin_up):
    return x * cos + pltpu.roll(x, ROT_HALF, 1) * sin_dn + pltpu.roll(x, LANES - ROT_HALF, 1) * sin_up


def _lower_bound(lb_all, layer):
    m = jnp.max(lb_all, axis=0, keepdims=True)
    e = jnp.exp(lb_all - m)
    p = e / jnp.sum(e, axis=0, keepdims=True)
    cs = p[0:1]
    for l in range(1, layer + 1):
        cs = cs + p[l:l + 1]
    return jnp.maximum(cs - p[0:1], 0.0)


def _hgrn_gates(z, lb):
    e = jnp.exp(-jnp.abs(z))
    log_sig = jnp.minimum(z, 0.0) - jnp.log1p(e)
    a1 = jnp.log(lb)
    a2 = jnp.log1p(-lb) + log_sig
    log_f = jnp.maximum(a1, a2) + jnp.log1p(jnp.exp(-jnp.abs(a1 - a2)))
    k_in = (1.0 - lb) * (jnp.where(z >= 0.0, e, 1.0) / (1.0 + e))
    return log_f, k_in


def _silu(x):
    return x * (0.5 + 0.5 * jnp.tanh(0.5 * x))


def _stacked_queries(proj_ref, cos, sdn, sup):
    t = proj_ref.shape[0]
    lo_half = lax.broadcasted_iota(jnp.int32, (t, LANES), 1) < HEAD_DIM
    q_ext = []
    for hp in range(ATT_HEADS // 2):
        q_slab = _rope(proj_ref[:, OFF_Q + hp * LANES:OFF_Q + (hp + 1) * LANES], cos, sdn, sup)
        q_slab = q_slab * (HEAD_DIM ** -0.5)
        q_swap = pltpu.roll(q_slab, HEAD_DIM, 1)
        for sub in range(2):
            kvh = (2 * hp + sub) // GROUP
            src = q_slab if sub == kvh else q_swap
            keep = lo_half if kvh == 0 else ~lo_half
            q_ext.append(jnp.where(keep, src, 0.0).astype(BF16))
    return q_ext


def _softmax_sink(s, allowed, sink):
    s = jnp.where(allowed, s, -jnp.inf)
    m = jnp.maximum(jnp.max(s, axis=-1, keepdims=True), sink)
    p = jnp.exp(s - m)
    inv = 1.0 / (jnp.sum(p, axis=-1, keepdims=True) + jnp.exp(sink - m))
    return p.astype(BF16), inv


def _merge_head_pair(a, b, kvh):
    lo_half = lax.broadcasted_iota(jnp.int32, a.shape, 1) < HEAD_DIM
    if kvh == 0:
        return jnp.where(lo_half, a, pltpu.roll(b, HEAD_DIM, 1))
    return jnp.where(lo_half, pltpu.roll(a, HEAD_DIM, 1), b)


def _cumsum_rows(tri, g):
    g1 = g.astype(BF16)
    r1 = g - g1.astype(F32)
    g2 = r1.astype(BF16)
    g3 = (r1 - g2.astype(F32)).astype(BF16)
    return _dot(tri, g1) + _dot(tri, g2) + _dot(tri, g3)


def _hgrn_intra_operands(q, k, b):
    c = q.shape[0]
    pairs = []
    for i in range(c // HG_SUB):
        lo, hi = i * HG_SUB, (i + 1) * HG_SUB
        r = b[lo - 1:lo] if i > 0 else jnp.zeros((1, HG_DK), F32)
        pairs.append(((q[lo:hi] * jnp.exp(b[lo:hi] - r)).astype(BF16),
                      (k[:hi] * jnp.exp(r - b[:hi])).astype(BF16)))
    return pairs


def _hgrn_worst_subblock_decay(b_all):
    worst = None
    for lo in range(0, b_all.shape[0], HG_SUB):
        tot = b_all[lo + HG_SUB - 1:lo + HG_SUB]
        if lo % HG_CHUNK:
            tot = tot - b_all[lo - 1:lo]
        worst = tot if worst is None else jnp.minimum(worst, tot)
    return worst


def _hgrn_intra_exact(q, k, v, b, work_ref, out_ref):
    for i, x in enumerate((q, k, v, b)):
        work_ref[i] = x
    s_idx = lax.broadcasted_iota(jnp.int32, (HG_CHUNK, 1), 0)

    def row(t, carry):
        c0 = pl.multiple_of((t // HG_CHUNK) * HG_CHUNK, HG_CHUNK)
        q_t = work_ref[0, pl.ds(t, 1), :]
        b_t = work_ref[3, pl.ds(t, 1), :]
        k_c = work_ref[1, pl.ds(c0, HG_CHUNK), :]
        v_c = work_ref[2, pl.ds(c0, HG_CHUNK), :]
        b_c = work_ref[3, pl.ds(c0, HG_CHUNK), :]
        w = k_c * jnp.exp(jnp.minimum(b_t - b_c, 0.0)) * q_t
        a = jnp.where(s_idx <= t - c0, jnp.sum(w, axis=-1, keepdims=True), 0.0)
        out_ref[pl.ds(t, 1), :] = jnp.sum(a * v_c, axis=0, keepdims=True)
        return carry

    lax.fori_loop(0, q.shape[0], row, 0)
    return out_ref[...]


def _hgrn_intra_scores(a_parts):
    c = HG_SUB * len(a_parts)
    rows = [a if a.shape[1] == c else jnp.concatenate([a, jnp.zeros((HG_SUB, c - a.shape[1]), F32)], axis=1)
            for a in a_parts]
    a = jnp.concatenate(rows, axis=0)
    row = lax.broadcasted_iota(jnp.int32, (c, c), 0)
    col = lax.broadcasted_iota(jnp.int32, (c, c), 1)
    return jnp.where(col <= row, a, 0.0).astype(BF16)


def _out_proj(x, mix_bf16, wo_ref, mnorm_ref):
    x1 = x + _dot(mix_bf16, wo_ref[...])
    return x1, _rms(x1, mnorm_ref[...]).astype(BF16)


def _mlp_cols(acc, h2, w_up_cols, w_dn_rows):
    u = jnp.maximum(_dot(h2, w_up_cols).astype(BF16), 0.0)
    return acc + _dot(u * u, w_dn_rows)


def _mlp_block(acc, h2, c, wup_ref, wdn_ref):
    return _mlp_cols(acc, h2, wup_ref[:, c * FF_BLOCK:(c + 1) * FF_BLOCK],
                     wdn_ref[c * FF_BLOCK:(c + 1) * FF_BLOCK, :])


def _prompt_step(with_mixer, with_tail, layer, last, n_tblk, t_blk,
                 x_ref, cos_ref, sdn_ref, sup_ref, tri_ref, sinks_ref, lb_ref, anorm_ref,
                 win_ref, onorm_ref, wo_ref, mnorm_ref, wup_ref, wdn_ref, fnorm_ref,
                 y_ref, nk_ref, nv_ref, ns_ref,
                 proj_ref, mix_ref, xprev_ref, kprev_ref, vprev_ref, st_ref, work_ref, exact_ref):
    tb = PROMPT_BLOCK
    if with_mixer:
        @pl.when(t_blk == 0)
        def _():
            kprev_ref[...] = jnp.zeros_like(kprev_ref)
            vprev_ref[...] = jnp.zeros_like(vprev_ref)
            st_ref[...] = jnp.zeros_like(st_ref)

    if with_tail:
        acc, h2 = _out_proj(xprev_ref[...], mix_ref[...], wo_ref, mnorm_ref)
    if not with_mixer:
        for c in range(D_FF // FF_BLOCK):
            acc = _mlp_block(acc, h2, c, wup_ref, wdn_ref)
        y_ref[0] = _rms(acc, fnorm_ref[...]) if last else acc
        return

    x = x_ref[0]
    xprev_ref[...] = x
    h = _rms(x, anorm_ref[...]).astype(BF16)
    proj_ref[...] = _dot(h, win_ref[...])

    cos, sdn, sup = cos_ref[...], sdn_ref[...], sup_ref[...]

    k_rot = _rope(proj_ref[:, OFF_K:OFF_K + KV_WIDTH], cos, sdn, sup)
    v_new = proj_ref[:, OFF_V:OFF_V + KV_WIDTH]
    k_all = jnp.concatenate([kprev_ref[...], k_rot], axis=0).astype(BF16)
    v_all = jnp.concatenate([vprev_ref[...], v_new], axis=0).astype(BF16)

    q_ext = _stacked_queries(proj_ref, cos, sdn, sup)
    row = lax.broadcasted_iota(jnp.int32, (WINDOW, 2 * WINDOW), 0)
    col = lax.broadcasted_iota(jnp.int32, (WINDOW, 2 * WINDOW), 1)
    in_window = (col >= row) & (col <= row + WINDOW)
    first_lo = jnp.where(t_blk == 0, WINDOW, 0)
    att = [[None] * (tb // WINDOW) for _ in range(ATT_HEADS)]
    for n in range(tb // WINDOW):
        rows_n = slice(n * WINDOW, (n + 1) * WINDOW)
        keys_n = slice(n * WINDOW, (n + 2) * WINDOW)
        q_stack = jnp.concatenate([q[rows_n] for q in q_ext], axis=0)
        s_all = _dot_nt(q_stack, k_all[keys_n])
        allowed = in_window & (col >= first_lo) if n == 0 else in_window
        soft = [_softmax_sink(s_all[head * WINDOW:(head + 1) * WINDOW], allowed, sinks_ref[head])
                for head in range(ATT_HEADS)]
        o_all = _dot(jnp.concatenate([p for p, _ in soft], axis=0), v_all[keys_n])
        for head in range(ATT_HEADS):
            att[head][n] = o_all[head * WINDOW:(head + 1) * WINDOW] * soft[head][1]
    for hp in range(ATT_HEADS // 2):
        a, b = (jnp.concatenate(att[2 * hp + sub], axis=0) for sub in range(2))
        mix_ref[:, hp * LANES:(hp + 1) * LANES] = _merge_head_pair(a, b, (2 * hp) // GROUP).astype(BF16)
    if with_tail:
        acc = _mlp_block(acc, h2, 0, wup_ref, wdn_ref)

    lb = _lower_bound(lb_ref[...], layer)
    log_f, k_in = _hgrn_gates(proj_ref[:, OFF_HF:OFF_HF + HG_WIDTH], lb)
    b_all = _cumsum_rows(tri_ref[...], log_f)
    onorm = onorm_ref[...]
    n_chunks = tb // HG_CHUNK
    units = [(hh, c) for c in range(n_chunks) for hh in range(HG_HEADS)]
    q_in, v_in = [], []
    for hh in range(HG_HEADS):
        q_in.append(_silu(proj_ref[:, OFF_HQ + hh * HG_DK:OFF_HQ + (hh + 1) * HG_DK]))
        v_in.append(proj_ref[:, OFF_HI + hh * HG_DV:OFF_HI + (hh + 1) * HG_DV])

    def piece(x, hh, c, lanes=False):
        x = x[:, hh * HG_DK:(hh + 1) * HG_DK] if lanes else x[hh]
        return x[c * HG_CHUNK:(c + 1) * HG_CHUNK]

    intra_ops = {u: _hgrn_intra_operands(piece(q_in, *u), piece(k_in, *u, lanes=True),
                                         piece(b_all, *u, lanes=True)) for u in units}
    a_full = {u: [_dot_nt(qh, kh) for qh, kh in intra_ops[u]] for u in units}
    o_inter, st_new = {}, []
    for hh in range(HG_HEADS):
        st = st_ref[hh]
        for c in range(n_chunks):
            q, k, v, b = (piece(q_in, hh, c), piece(k_in, hh, c, True), piece(v_in, hh, c),
                          piece(b_all, hh, c, True))
            o_inter[(hh, c)] = _dot_nt((q * jnp.exp(b)).astype(BF16), st.astype(BF16))
            b_last = b[HG_CHUNK - 1:HG_CHUNK]
            k_hat = (k * jnp.exp(b_last - b)).astype(BF16)
            st = st * jnp.exp(b_last) + _dot(v.T.astype(BF16), k_hat)
        st_new.append(st)
    a_mask = {u: _hgrn_intra_scores(a_full[u]) for u in units}
    o_intra = {u: _dot(a_mask[u], piece(v_in, *u).astype(BF16)) for u in units}

    def emit_head(hh, o_intra_h):
        o = o_intra_h + jnp.concatenate([o_inter[(hh, c)] for c in range(n_chunks)], axis=0)
        gate = _silu(proj_ref[:, OFF_HG + hh * HG_DV:OFF_HG + (hh + 1) * HG_DV])
        mix_ref[:, ATT_WIDTH + hh * HG_DV:ATT_WIDTH + (hh + 1) * HG_DV] = (
            _rms(o, onorm) * gate).astype(BF16)

    for hh in range(HG_HEADS):
        emit_head(hh, jnp.concatenate([o_intra[(hh, c)] for c in range(n_chunks)], axis=0))
    if with_tail:
        acc = _mlp_block(acc, h2, 1, wup_ref, wdn_ref)
        y_ref[0] = _rms(acc, fnorm_ref[...]) if last else acc

    @pl.when(jnp.min(_hgrn_worst_subblock_decay(b_all)) < -HG_GUARD)
    def _():
        for hh in range(HG_HEADS):
            sl = slice(hh * HG_DK, (hh + 1) * HG_DK)
            emit_head(hh, _hgrn_intra_exact(q_in[hh], k_in[:, sl], v_in[hh], b_all[:, sl],
                                            work_ref, exact_ref))

    kprev_ref[...] = k_rot[tb - WINDOW:]
    vprev_ref[...] = v_new[tb - WINDOW:]
    nk_ref[0] = k_rot[tb - WINDOW:]
    nv_ref[0] = v_new[tb - WINDOW:]
    for hh in range(HG_HEADS):
        st_ref[hh] = st_new[hh]

    @pl.when(t_blk == n_tblk - 1)
    def _():
        for hh in range(HG_HEADS):
            ns_ref[0, hh] = st_new[hh].T


def _prompt_layer_kernel(layer, last, n_tblk, *refs):
    j = pl.program_id(0)
    n_blocks = pl.num_programs(0) - 1
    t_blk = lax.rem(j, n_tblk)

    def run(with_mixer, with_tail):
        _prompt_step(with_mixer, with_tail, layer, last, n_tblk, t_blk, *refs)

    pl.when(j == 0)(functools.partial(run, True, False))
    pl.when((j > 0) & (j < n_blocks))(functools.partial(run, True, True))
    pl.when(j == n_blocks)(functools.partial(run, False, True))


def _const_spec(shape):
    nd = len(shape)
    return pl.BlockSpec(shape, lambda *_: (0,) * nd, pipeline_mode=pl.Buffered(1))


def _layer_spec(layer, shape):
    nd = len(shape)
    return pl.BlockSpec((None,) + shape, lambda *_: (layer,) + (0,) * nd, pipeline_mode=pl.Buffered(1))


def _prompt_layer(layer, x, rope, tri, sinks, lb_all, anorm, w_in, onorm, w_o, mnorm, w_up, w_dn, fnorm):
    batch, seq, _ = x.shape
    tb = PROMPT_BLOCK
    n_tblk = seq // tb
    n_blocks = batch * n_tblk
    last = layer == DEPTH - 1
    mixer_blk = lambda j: jnp.minimum(j, n_blocks - 1)
    tail_blk = lambda j: jnp.maximum(j - 1, 0)
    rope_spec = pl.BlockSpec((tb, LANES), lambda j: (mixer_blk(j) % n_tblk, 0))
    carry_spec = pl.BlockSpec((1, WINDOW, KV_WIDTH), lambda j: (mixer_blk(j) // n_tblk, 0, 0))
    return pl.pallas_call(
        functools.partial(_prompt_layer_kernel, layer, last, n_tblk),
        grid=(n_blocks + 1,),
        in_specs=[
            pl.BlockSpec((1, tb, D_MODEL), lambda j: (mixer_blk(j) // n_tblk, mixer_blk(j) % n_tblk, 0)),
            rope_spec, rope_spec, rope_spec,
            _const_spec((tb, tb)),
            pl.BlockSpec(memory_space=pltpu.SMEM),
            _const_spec((DEPTH, HG_WIDTH)),
            _layer_spec(layer, (1, D_MODEL)),
            _const_spec((D_MODEL, IN_WIDTH)),
            _layer_spec(layer, (1, HG_DV)),
            _const_spec((MIX_WIDTH, D_MODEL)),
            _layer_spec(layer, (1, D_MODEL)),
            _const_spec((D_MODEL, D_FF)),
            _const_spec((D_FF, D_MODEL)),
            _const_spec((1, D_MODEL)),
        ],
        out_specs=[
            pl.BlockSpec((1, tb, D_MODEL), lambda j: (tail_blk(j) // n_tblk, tail_blk(j) % n_tblk, 0)),
            carry_spec, carry_spec,
            pl.BlockSpec((1, HG_HEADS, HG_DK, HG_DV), lambda j: (mixer_blk(j) // n_tblk, 0, 0, 0)),
        ],
        out_shape=[
            jax.ShapeDtypeStruct((batch, seq, D_MODEL), F32),
            jax.ShapeDtypeStruct((batch, WINDOW, KV_WIDTH), F32),
            jax.ShapeDtypeStruct((batch, WINDOW, KV_WIDTH), F32),
            jax.ShapeDtypeStruct((batch, HG_HEADS, HG_DK, HG_DV), F32),
        ],
        scratch_shapes=[
            pltpu.VMEM((tb, IN_WIDTH), F32),
            pltpu.VMEM((tb, MIX_WIDTH), BF16),
            pltpu.VMEM((tb, D_MODEL), F32),
            pltpu.VMEM((WINDOW, KV_WIDTH), F32),
            pltpu.VMEM((WINDOW, KV_WIDTH), F32),
            pltpu.VMEM((HG_HEADS, HG_DV, HG_DK), F32),
            pltpu.VMEM((4, tb, HG_DK), F32),
            pltpu.VMEM((tb, HG_DV), F32),
        ],
        compiler_params=pltpu.CompilerParams(
            dimension_semantics=("arbitrary",), vmem_limit_bytes=VMEM_LIMIT),
        name=f"prompt_layer{layer}",
    )(x, *rope, tri, sinks, lb_all, anorm, w_in, onorm, w_o, mnorm, w_up, w_dn, fnorm)


def _sample_in_kernel(x_ref, anorm_ref, win_ref, proj_ref, win_bf_ref, h_ref):
    @pl.when(pl.program_id(0) == 0)
    def _():
        h_ref[...] = _rms(x_ref[...], anorm_ref[...]).astype(BF16)

    win_bf_ref[...] = win_ref[...].astype(BF16)
    proj_ref[...] = _dot(h_ref[...], win_bf_ref[...])


def _sample_out_kernel(last, x_ref, mix_ref, wo_ref, mnorm_ref, wup_ref, wdn_ref, fnorm_ref,
                       y_ref, wo_bf_ref, wup_bf_ref, wdn_bf_ref, h2_ref, acc_ref):
    c = pl.program_id(0)

    @pl.when(c == 0)
    def _():
        wo_bf_ref[...] = wo_ref[...].astype(BF16)
        acc_ref[...], h2_ref[...] = _out_proj(x_ref[...], mix_ref[...].astype(BF16), wo_bf_ref, mnorm_ref)

    wup_bf_ref[...] = wup_ref[...].astype(BF16)
    wdn_bf_ref[...] = wdn_ref[...].astype(BF16)
    acc_ref[...] = _mlp_cols(acc_ref[...], h2_ref[...], wup_bf_ref[...], wdn_bf_ref[...])

    @pl.when(c == pl.num_programs(0) - 1)
    def _():
        y_ref[...] = _rms(acc_ref[...], fnorm_ref[...]) if last else acc_ref[...]


def _sample_mixer_kernel(layer, n_new, n_aliased, *refs):
    (p_ref, cos_ref, sdn_ref, sup_ref, tri_ref, rev_ref, sel_ref, ck_ref, cv_ref, s0_ref, sinks_ref, lb_ref,
     onorm_ref) = refs[:SAMPLE_MIXER_INPUTS]
    mix_ref, nk_ref, nv_ref, ns_ref, pad_ref = refs[SAMPLE_MIXER_INPUTS + n_aliased:]
    slot = 0 if n_aliased else layer
    for out_ref in (nk_ref, nv_ref, ns_ref):
        for other in range(out_ref.shape[0]):
            if other != slot:
                out_ref[other] = jnp.zeros(out_ref.shape[1:], F32)
    sb = ck_ref.shape[0]
    ts = p_ref.shape[0]
    tok_shift = n_new.bit_length() - 1
    key_shift = WINDOW.bit_length() - 1
    n_cache = sb * WINDOW
    cos, sdn, sup = cos_ref[...], sdn_ref[...], sup_ref[...]

    k_new = _rope(p_ref[:, OFF_K:OFF_K + KV_WIDTH], cos, sdn, sup)
    v_new = p_ref[:, OFF_V:OFF_V + KV_WIDTH]
    pad_keys = jnp.zeros((LANES - ts, KV_WIDTH), F32)
    k_all = jnp.concatenate([ck_ref[...].reshape(n_cache, KV_WIDTH), k_new, pad_keys], axis=0).astype(BF16)
    v_all = jnp.concatenate([cv_ref[...].reshape(n_cache, KV_WIDTH), v_new, pad_keys], axis=0).astype(BF16)
    n_keys = n_cache + LANES
    row = lax.broadcasted_iota(jnp.int32, (ts, n_keys), 0)
    col = lax.broadcasted_iota(jnp.int32, (ts, n_keys), 1)
    new_idx = col - n_cache
    cached_ok = ((col >> key_shift) == (row >> tok_shift)) & ((col & (WINDOW - 1)) >= (row & (n_new - 1)))
    new_ok = ((new_idx < ts) & ((new_idx >> tok_shift) == (row >> tok_shift))
              & ((new_idx & (n_new - 1)) <= (row & (n_new - 1))))
    allowed = ((col < n_cache) & cached_ok) | ((col >= n_cache) & new_ok)

    q_ext = _stacked_queries(p_ref, cos, sdn, sup)
    s_all = _dot_nt(jnp.concatenate(q_ext, axis=0), k_all)
    soft = [_softmax_sink(s_all[head * ts:(head + 1) * ts], allowed, sinks_ref[head])
            for head in range(ATT_HEADS)]
    o_all = _dot(jnp.concatenate([p for p, _ in soft], axis=0), v_all)
    att = [o_all[head * ts:(head + 1) * ts] * soft[head][1] for head in range(ATT_HEADS)]
    for hp in range(ATT_HEADS // 2):
        mix_ref[:, hp * LANES:(hp + 1) * LANES] = _merge_head_pair(
            att[2 * hp], att[2 * hp + 1], (2 * hp) // GROUP)
    for s in range(sb):
        nk_ref[slot, s, 0:WINDOW - n_new] = ck_ref[s, n_new:WINDOW]
        nv_ref[slot, s, 0:WINDOW - n_new] = cv_ref[s, n_new:WINDOW]
        nk_ref[slot, s, WINDOW - n_new:WINDOW] = k_new[s * n_new:(s + 1) * n_new]
        nv_ref[slot, s, WINDOW - n_new:WINDOW] = v_new[s * n_new:(s + 1) * n_new]

    lb = _lower_bound(lb_ref[...], layer)
    log_f, k_in = _hgrn_gates(p_ref[:, OFF_HF:OFF_HF + HG_WIDTH], lb)
    b = _cumsum_rows(tri_ref[...], log_f)
    later = _cumsum_rows(rev_ref[...], log_f)
    q_in = _silu(p_ref[:, OFF_HQ:OFF_HQ + HG_WIDTH])
    v_in = p_ref[:, OFF_HI:OFF_HI + HG_WIDTH]
    gate = _silu(p_ref[:, OFF_HG:OFF_HG + HG_WIDTH])
    onorm = onorm_ref[...]
    heads = [slice(hh * HG_DK, (hh + 1) * HG_DK) for hh in range(HG_HEADS)]

    pad_ref[...] = jnp.zeros_like(pad_ref)
    for i, x in enumerate((k_in, b, v_in)):
        pad_ref[i, SUBLANES:SUBLANES + ts] = x
    tok = lax.broadcasted_iota(jnp.int32, (ts, 1), 0) & (n_new - 1)
    o_intra = [jnp.zeros((ts, HG_DV), F32) for _ in heads]
    for d in range(n_new):
        k_d, b_d, v_d = (pad_ref[i, SUBLANES - d:SUBLANES - d + ts] for i in range(3))
        w = q_in * k_d * jnp.exp(jnp.minimum(b - b_d, 0.0))
        for hh, sl in enumerate(heads):
            a_col = jnp.sum(w[:, sl], axis=-1, keepdims=True)
            o_intra[hh] = o_intra[hh] + jnp.where(tok >= d, a_col, 0.0) * v_d[:, sl]

    q_dec = (q_in * jnp.exp(b)).astype(BF16)
    seq_of_row = lax.broadcasted_iota(jnp.int32, (ts, 1), 0) >> tok_shift
    for hh, sl in enumerate(heads):
        o = o_intra[hh]
        for s in range(sb):
            o_s = _dot(q_dec[:, sl], s0_ref[s, hh].astype(BF16))
            o = o + jnp.where(seq_of_row == s, o_s, 0.0)
        mix_ref[:, ATT_WIDTH + hh * HG_DV:ATT_WIDTH + (hh + 1) * HG_DV] = _rms(o, onorm) * gate[:, sl]

    k_hat = k_in * jnp.exp(later)
    e_b = jnp.exp(b)
    e1 = e_b.astype(BF16).astype(F32)
    e2 = (e_b - e1).astype(BF16).astype(F32)
    e3 = e_b - e1 - e2
    own = ((lax.broadcasted_iota(jnp.int32, (ts, sb * HG_DV), 1) >> key_shift)
           == (lax.broadcasted_iota(jnp.int32, (ts, sb * HG_DV), 0) >> tok_shift))
    no_v = jnp.zeros((3 * ts, sb * HG_DV), F32)
    for hh, sl in enumerate(heads):
        lhs = jnp.concatenate([k_hat[:, sl], e1[:, sl], e2[:, sl], e3[:, sl]], axis=0).T.astype(BF16)
        v_rep = jnp.concatenate([v_in[:, sl]] * sb, axis=1)
        v_diag = jnp.concatenate([jnp.where(own, v_rep, 0.0), no_v], axis=0).astype(BF16)
        upd = _dot(lhs, v_diag)
        decay = _dot(lhs, sel_ref[...])
        for s in range(sb):
            cols = slice(s * HG_DV, (s + 1) * HG_DV)
            ns_ref[slot, s, hh] = decay[:, cols] * s0_ref[s, hh] + upd[:, cols]


def _sample_layer(layer, x, rope, cache_k, cache_v, state, new_caches, sinks, lb_all, anorm, w_in, onorm, w_o,
                  mnorm, w_up, w_dn, fnorm):
    n_seq, n_new, _ = x.shape
    n_tok = n_seq * n_new
    ts = LANES // 4
    sb = ts // n_new
    assert sb * n_new == ts and n_seq % sb == 0 and n_new <= SUBLANES
    assert n_new & (n_new - 1) == 0 and WINDOW & (WINDOW - 1) == 0 and HG_DV == WINDOW
    last = layer == DEPTH - 1
    x2 = x.reshape(n_tok, D_MODEL)
    params = pltpu.CompilerParams(vmem_limit_bytes=VMEM_LIMIT)
    whole = _const_spec
    slab = functools.partial(_layer_spec, layer)

    proj, w_in_bf = pl.pallas_call(
        _sample_in_kernel,
        grid=(IN_WIDTH // SAMPLE_IN_BLOCK,),
        in_specs=[whole((n_tok, D_MODEL)), slab((1, D_MODEL)),
                  pl.BlockSpec((None, D_MODEL, SAMPLE_IN_BLOCK), lambda c: (layer, 0, c))],
        out_specs=[pl.BlockSpec((n_tok, SAMPLE_IN_BLOCK), lambda c: (0, c)),
                   pl.BlockSpec((D_MODEL, SAMPLE_IN_BLOCK), lambda c: (0, c))],
        out_shape=[jax.ShapeDtypeStruct((n_tok, IN_WIDTH), F32),
                   jax.ShapeDtypeStruct((D_MODEL, IN_WIDTH), BF16)],
        scratch_shapes=[pltpu.VMEM((n_tok, D_MODEL), BF16)],
        compiler_params=params,
        name=f"sample_in{layer}",
    )(x2, anorm, w_in)

    r = jnp.arange(ts)
    same_seq = (r[:, None] // n_new) == (r[None, :] // n_new)
    tri = (same_seq & (r[None, :] <= r[:, None])).astype(BF16)
    rev = (same_seq & (r[None, :] > r[:, None])).astype(BF16)
    c = jnp.arange(sb * HG_DV)
    is_last = ((r[:, None] % n_new) == n_new - 1) & ((r[:, None] // n_new) == (c[None, :] // HG_DV))
    sel = jnp.concatenate([jnp.zeros_like(is_last)] + [is_last] * 3, axis=0).astype(BF16)

    tok_spec = lambda width: pl.BlockSpec((ts, width), lambda i: (i, 0))
    seq_spec = lambda *tail: pl.BlockSpec((sb,) + tail, lambda i: (i,) + (0,) * len(tail))
    layer_spec = lambda *tail: pl.BlockSpec((None, sb) + tail, lambda i: (layer, i) + (0,) * len(tail))
    rope_spec = _const_spec((ts, LANES))
    first = not new_caches
    slab_spec = lambda *tail: pl.BlockSpec(
        (DEPTH if first else 1, sb) + tail, lambda i: (0 if first else layer, i) + (0,) * len(tail))
    n_in = SAMPLE_MIXER_INPUTS
    aliases = {n_in + i: 1 + i for i in range(len(new_caches))}
    stacked = [(DEPTH, n_seq, WINDOW, KV_WIDTH)] * 2 + [(DEPTH, n_seq, HG_HEADS, HG_DK, HG_DV)]
    mix, *new_caches = pl.pallas_call(
        functools.partial(_sample_mixer_kernel, layer, n_new, len(new_caches)),
        grid=(n_seq // sb,),
        in_specs=[
            tok_spec(IN_WIDTH),
            rope_spec, rope_spec, rope_spec,
            _const_spec((ts, ts)), _const_spec((ts, ts)), _const_spec((4 * ts, sb * HG_DV)),
            layer_spec(WINDOW, KV_WIDTH), layer_spec(WINDOW, KV_WIDTH),
            layer_spec(HG_HEADS, HG_DK, HG_DV),
            pl.BlockSpec(memory_space=pltpu.SMEM),
            _const_spec((DEPTH, HG_WIDTH)),
            _layer_spec(layer, (1, HG_DV)),
        ] + [pl.BlockSpec(memory_space=pl.ANY)] * len(new_caches),
        out_specs=[
            tok_spec(MIX_WIDTH),
            slab_spec(WINDOW, KV_WIDTH), slab_spec(WINDOW, KV_WIDTH),
            slab_spec(HG_HEADS, HG_DK, HG_DV),
        ],
        out_shape=[jax.ShapeDtypeStruct((n_tok, MIX_WIDTH), F32)]
        + [jax.ShapeDtypeStruct(shape, F32) for shape in stacked],
        input_output_aliases=aliases,
        scratch_shapes=[pltpu.VMEM((3, SUBLANES + ts, HG_WIDTH), F32)],
        compiler_params=pltpu.CompilerParams(
            dimension_semantics=("arbitrary",), vmem_limit_bytes=VMEM_LIMIT),
        name=f"sample_mixer{layer}",
    )(proj, *(jnp.tile(t, (sb, 1)) for t in rope), tri, rev, sel, cache_k, cache_v, state, sinks, lb_all,
      onorm, *new_caches)

    y, w_o_bf, w_up_bf, w_dn_bf = pl.pallas_call(
        functools.partial(_sample_out_kernel, last),
        grid=(D_FF // SAMPLE_FF_BLOCK,),
        in_specs=[whole((n_tok, D_MODEL)), whole((n_tok, MIX_WIDTH)), slab((MIX_WIDTH, D_MODEL)),
                  slab((1, D_MODEL)),
                  pl.BlockSpec((None, D_MODEL, SAMPLE_FF_BLOCK), lambda c: (layer, 0, c)),
                  pl.BlockSpec((None, SAMPLE_FF_BLOCK, D_MODEL), lambda c: (layer, c, 0)),
                  whole((1, D_MODEL))],
        out_specs=[pl.BlockSpec((n_tok, D_MODEL), lambda c: (0, 0)),
                   pl.BlockSpec((MIX_WIDTH, D_MODEL), lambda c: (0, 0)),
                   pl.BlockSpec((D_MODEL, SAMPLE_FF_BLOCK), lambda c: (0, c)),
                   pl.BlockSpec((SAMPLE_FF_BLOCK, D_MODEL), lambda c: (c, 0))],
        out_shape=[jax.ShapeDtypeStruct((n_tok, D_MODEL), F32),
                   jax.ShapeDtypeStruct((MIX_WIDTH, D_MODEL), BF16),
                   jax.ShapeDtypeStruct((D_MODEL, D_FF), BF16),
                   jax.ShapeDtypeStruct((D_FF, D_MODEL), BF16)],
        scratch_shapes=[pltpu.VMEM((n_tok, D_MODEL), BF16), pltpu.VMEM((n_tok, D_MODEL), F32)],
        compiler_params=params,
        name=f"sample_out{layer}",
    )(x2, mix, w_o, mnorm, w_up, w_dn, fnorm)
    return y.reshape(n_seq, n_new, D_MODEL), new_caches, (w_in_bf, w_o_bf, w_up_bf, w_dn_bf)


def _rope_tables(pos):
    n = pos.shape[0]
    inv_freq = jnp.power(ROPE_THETA, -jnp.arange(ROT_HALF, dtype=F32) * (2.0 / ROT_DIM))
    ang = pos.astype(F32)[:, None] * inv_freq[None, :]
    cos, sin = jnp.cos(ang), jnp.sin(ang)
    rest = HEAD_DIM - ROT_DIM
    zeros_h = jnp.zeros((n, ROT_HALF), F32)
    cos_t = jnp.concatenate([cos, cos, jnp.ones((n, rest), F32)], axis=1)
    sdn_t = jnp.concatenate([zeros_h, sin, jnp.zeros((n, rest), F32)], axis=1)
    sup_t = jnp.concatenate([-sin, zeros_h, jnp.zeros((n, rest), F32)], axis=1)
    reps = LANES // HEAD_DIM
    return tuple(jnp.tile(t, (1, reps)) for t in (cos_t, sdn_t, sup_t))


def _chunk_tri(n, chunk):
    r = jnp.arange(n)
    same = (r[:, None] // chunk) == (r[None, :] // chunk)
    return (same & (r[None, :] <= r[:, None])).astype(BF16)


def kernel(x_prompt, x_sample, cache_k, cache_v, state_hgrn, attn_norm, w_in, att_sinks, hgrn_lower_bounds,
           hgrn_out_norm, w_o, mlp_norm, w_up, w_down, final_norm):
    batch, seq, _ = x_prompt.shape
    n_seq, n_new, _ = x_sample.shape
    assert seq % PROMPT_BLOCK == 0 and PROMPT_BLOCK % HG_CHUNK == 0

    rope_p = _rope_tables(jnp.arange(seq, dtype=jnp.int32))
    rope_s = _rope_tables(PAST_LEN + jnp.arange(n_new, dtype=jnp.int32))
    tri = _chunk_tri(PROMPT_BLOCK, HG_CHUNK)
    lb_all = hgrn_lower_bounds.astype(F32)
    fnorm = final_norm.reshape(1, D_MODEL)

    ck = cache_k.reshape(DEPTH, n_seq, WINDOW, KV_WIDTH)
    cv = cache_v.reshape(DEPTH, n_seq, WINDOW, KV_WIDTH)
    anorm, onorm, mnorm = (attn_norm.reshape(DEPTH, 1, D_MODEL), hgrn_out_norm.reshape(DEPTH, 1, HG_DV),
                           mlp_norm.reshape(DEPTH, 1, D_MODEL))
    xp, xs = x_prompt, x_sample
    nk_p, nv_p, ns_p, new_caches = [], [], [], []
    for l in range(DEPTH):
        xs, new_caches, (w_in_bf, w_o_bf, w_up_bf, w_dn_bf) = _sample_layer(
            l, xs, rope_s, ck, cv, state_hgrn, new_caches, att_sinks[l], lb_all, anorm, w_in, onorm, w_o, mnorm,
            w_up, w_down, fnorm)
        xp, k1, v1, s1 = _prompt_layer(l, xp, rope_p, tri, att_sinks[l], lb_all, anorm, w_in_bf, onorm, w_o_bf,
                                       mnorm, w_up_bf, w_dn_bf, fnorm)
        nk_p.append(k1), nv_p.append(v1), ns_p.append(s1)
    nk_s, nv_s, ns_s = new_caches

    kv_p = (DEPTH, batch, WINDOW, ATT_KV_HEADS, HEAD_DIM)
    kv_s = (DEPTH, n_seq, WINDOW, ATT_KV_HEADS, HEAD_DIM)
    return (xp, xs,
            jnp.stack(nk_p).reshape(kv_p), jnp.stack(nv_p).reshape(kv_p), jnp.stack(ns_p),
            nk_s.reshape(kv_s), nv_s.reshape(kv_s), ns_s)
```

```python
import functools

import jax
import jax.numpy as jnp
from jax import lax
from jax.experimental import pallas as pl
from jax.experimental.pallas import tpu as pltpu

F32 = jnp.float32
BF16 = jnp.bfloat16

D_MODEL = 1024
DEPTH = 2
PAST_LEN = 16384
ATT_HEADS = 8
ATT_KV_HEADS = 2
HEAD_DIM = 64
GROUP = ATT_HEADS // ATT_KV_HEADS
ATT_WIDTH = ATT_HEADS * HEAD_DIM
KV_WIDTH = ATT_KV_HEADS * HEAD_DIM
WINDOW = 128
ROT_DIM = HEAD_DIM // 4
ROT_HALF = ROT_DIM // 2
ROPE_THETA = 500000.0
HG_HEADS = 4
HG_DK = 128
HG_DV = 128
HG_WIDTH = HG_HEADS * HG_DK
MIX_WIDTH = ATT_WIDTH + HG_WIDTH
IN_WIDTH = ATT_WIDTH + 2 * KV_WIDTH + 4 * HG_WIDTH
D_FF = 4 * D_MODEL
EPS = 1e-6

OFF_Q = 0
OFF_K = ATT_WIDTH
OFF_V = OFF_K + KV_WIDTH
OFF_HQ = OFF_V + KV_WIDTH
OFF_HF = OFF_HQ + HG_WIDTH
OFF_HI = OFF_HF + HG_WIDTH
OFF_HG = OFF_HI + HG_WIDTH

LANES = 128
PROMPT_BLOCK = 256
HG_CHUNK = 128
HG_SUB = 32
HG_GUARD = 80.0
FF_BLOCK = 2048
SUBLANES = 8
SAMPLE_IN_BLOCK = 1408
SAMPLE_FF_BLOCK = 1024
SAMPLE_MIXER_INPUTS = 13
VMEM_LIMIT = 58 * 1024 * 1024


def _dot(a, b):
    return jnp.dot(a, b, preferred_element_type=F32)


def _dot_nt(a, b):
    return lax.dot_general(a, b, (((1,), (1,)), ((), ())), preferred_element_type=F32)


def _rms(x, g_row):
    ms = jnp.mean(x * x, axis=-1, keepdims=True)
    return (x * lax.rsqrt(ms + EPS)) * g_row


def _rope(x, cos, sin_dn, sin_up):
    return x * cos + pltpu.roll(x, ROT_HALF, 1) * sin_dn + pltpu.roll(x, LANES - ROT_HALF, 1) * sin_up


def _lower_bound(lb_all, layer):
    m = jnp.max(lb_all, axis=0, keepdims=True)
    e = jnp.exp(lb_all - m)
    p = e / jnp.sum(e, axis=0, keepdims=True)
    cs = p[0:1]
    for l in range(1, layer + 1):
        cs = cs + p[l:l + 1]
    return jnp.maximum(cs - p[0:1], 0.0)


def _hgrn_gates(z, lb):
    e = jnp.exp(-jnp.abs(z))
    log_sig = jnp.minimum(z, 0.0) - jnp.log1p(e)
    a1 = jnp.log(lb)
    a2 = jnp.log1p(-lb) + log_sig
    log_f = jnp.maximum(a1, a2) + jnp.log1p(jnp.exp(-jnp.abs(a1 - a2)))
    k_in = (1.0 - lb) * (jnp.where(z >= 0.0, e, 1.0) / (1.0 + e))
    return log_f, k_in


def _silu(x):
    return x * (0.5 + 0.5 * jnp.tanh(0.5 * x))


def _stacked_queries(proj_ref, cos, sdn, sup):
    t = proj_ref.shape[0]
    lo_half = lax.broadcasted_iota(jnp.int32, (t, LANES), 1) < HEAD_DIM
    q_ext = []
    for hp in range(ATT_HEADS // 2):
        q_slab = _rope(proj_ref[:, OFF_Q + hp * LANES:OFF_Q + (hp + 1) * LANES], cos, sdn, sup)
        q_slab = q_slab * (HEAD_DIM ** -0.5)
        q_swap = pltpu.roll(q_slab, HEAD_DIM, 1)
        for sub in range(2):
            kvh = (2 * hp + sub) // GROUP
            src = q_slab if sub == kvh else q_swap
            keep = lo_half if kvh == 0 else ~lo_half
            q_ext.append(jnp.where(keep, src, 0.0).astype(BF16))
    return q_ext


def _softmax_sink(s, allowed, sink):
    s = jnp.where(allowed, s, -jnp.inf)
    m = jnp.maximum(jnp.max(s, axis=-1, keepdims=True), sink)
    p = jnp.exp(s - m)
    inv = 1.0 / (jnp.sum(p, axis=-1, keepdims=True) + jnp.exp(sink - m))
    return p.astype(BF16), inv


def _merge_head_pair(a, b, kvh):
    lo_half = lax.broadcasted_iota(jnp.int32, a.shape, 1) < HEAD_DIM
    if kvh == 0:
        return jnp.where(lo_half, a, pltpu.roll(b, HEAD_DIM, 1))
    return jnp.where(lo_half, pltpu.roll(a, HEAD_DIM, 1), b)


def _cumsum_rows(tri, g):
    g1 = g.astype(BF16)
    r1 = g - g1.astype(F32)
    g2 = r1.astype(BF16)
    g3 = (r1 - g2.astype(F32)).astype(BF16)
    return _dot(tri, g1) + _dot(tri, g2) + _dot(tri, g3)


def _hgrn_intra_operands(q, k, b):
    c = q.shape[0]
    pairs = []
    for i in range(c // HG_SUB):
        lo, hi = i * HG_SUB, (i + 1) * HG_SUB
        r = b[lo - 1:lo] if i > 0 else jnp.zeros((1, HG_DK), F32)
        pairs.append(((q[lo:hi] * jnp.exp(b[lo:hi] - r)).astype(BF16),
                      (k[:hi] * jnp.exp(r - b[:hi])).astype(BF16)))
    return pairs


def _hgrn_worst_subblock_decay(b_all):
    worst = None
    for lo in range(0, b_all.shape[0], HG_SUB):
        tot = b_all[lo + HG_SUB - 1:lo + HG_SUB]
        if lo % HG_CHUNK:
            tot = tot - b_all[lo - 1:lo]
        worst = tot if worst is None else jnp.minimum(worst, tot)
    return worst


def _hgrn_intra_exact(q, k, v, b, work_ref, out_ref):
    for i, x in enumerate((q, k, v, b)):
        work_ref[i] = x
    s_idx = lax.broadcasted_iota(jnp.int32, (HG_CHUNK, 1), 0)

    def row(t, carry):
        c0 = pl.multiple_of((t // HG_CHUNK) * HG_CHUNK, HG_CHUNK)
        q_t = work_ref[0, pl.ds(t, 1), :]
        b_t = work_ref[3, pl.ds(t, 1), :]
        k_c = work_ref[1, pl.ds(c0, HG_CHUNK), :]
        v_c = work_ref[2, pl.ds(c0, HG_CHUNK), :]
        b_c = work_ref[3, pl.ds(c0, HG_CHUNK), :]
        w = k_c * jnp.exp(jnp.minimum(b_t - b_c, 0.0)) * q_t
        a = jnp.where(s_idx <= t - c0, jnp.sum(w, axis=-1, keepdims=True), 0.0)
        out_ref[pl.ds(t, 1), :] = jnp.sum(a * v_c, axis=0, keepdims=True)
        return carry

    lax.fori_loop(0, q.shape[0], row, 0)
    return out_ref[...]


def _hgrn_intra_scores(a_parts):
    c = HG_SUB * len(a_parts)
    rows = [a if a.shape[1] == c else jnp.concatenate([a, jnp.zeros((HG_SUB, c - a.shape[1]), F32)], axis=1)
            for a in a_parts]
    a = jnp.concatenate(rows, axis=0)
    row = lax.broadcasted_iota(jnp.int32, (c, c), 0)
    col = lax.broadcasted_iota(jnp.int32, (c, c), 1)
    return jnp.where(col <= row, a, 0.0).astype(BF16)


def _out_proj(x, mix_bf16, wo_ref, mnorm_ref):
    x1 = x + _dot(mix_bf16, wo_ref[...])
    return x1, _rms(x1, mnorm_ref[...]).astype(BF16)


def _mlp_cols(acc, h2, w_up_cols, w_dn_rows):
    u = jnp.square(jnp.maximum(_dot(h2, w_up_cols), 0.0)).astype(BF16)
    return acc + _dot(u, w_dn_rows)


def _mlp_block(acc, h2, c, wup_ref, wdn_ref):
    return _mlp_cols(acc, h2, wup_ref[:, c * FF_BLOCK:(c + 1) * FF_BLOCK],
                     wdn_ref[c * FF_BLOCK:(c + 1) * FF_BLOCK, :])


def _prompt_step(with_mixer, with_tail, layer, last, n_tblk, t_blk,
                 x_ref, cos_ref, sdn_ref, sup_ref, tri_ref, sinks_ref, lb_ref, anorm_ref,
                 win_ref, onorm_ref, wo_ref, mnorm_ref, wup_ref, wdn_ref, fnorm_ref,
                 y_ref, nk_ref, nv_ref, ns_ref,
                 proj_ref, mix_ref, xprev_ref, kprev_ref, vprev_ref, st_ref, work_ref, exact_ref):
    tb = PROMPT_BLOCK
    if with_mixer:
        @pl.when(t_blk == 0)
        def _():
            kprev_ref[...] = jnp.zeros_like(kprev_ref)
            vprev_ref[...] = jnp.zeros_like(vprev_ref)
            st_ref[...] = jnp.zeros_like(st_ref)

    if with_tail:
        acc, h2 = _out_proj(xprev_ref[...], mix_ref[...], wo_ref, mnorm_ref)
    if not with_mixer:
        for c in range(D_FF // FF_BLOCK):
            acc = _mlp_block(acc, h2, c, wup_ref, wdn_ref)
        y_ref[0] = _rms(acc, fnorm_ref[...]) if last else acc
        return

    x = x_ref[0]
    xprev_ref[...] = x
    h = _rms(x, anorm_ref[...]).astype(BF16)
    proj_ref[...] = _dot(h, win_ref[...])

    cos, sdn, sup = cos_ref[...], sdn_ref[...], sup_ref[...]

    k_rot = _rope(proj_ref[:, OFF_K:OFF_K + KV_WIDTH], cos, sdn, sup)
    v_new = proj_ref[:, OFF_V:OFF_V + KV_WIDTH]
    k_all = jnp.concatenate([kprev_ref[...], k_rot], axis=0).astype(BF16)
    v_all = jnp.concatenate([vprev_ref[...], v_new], axis=0).astype(BF16)

    q_ext = _stacked_queries(proj_ref, cos, sdn, sup)
    row = lax.broadcasted_iota(jnp.int32, (WINDOW, 2 * WINDOW), 0)
    col = lax.broadcasted_iota(jnp.int32, (WINDOW, 2 * WINDOW), 1)
    in_window = (col >= row) & (col <= row + WINDOW)
    first_lo = jnp.where(t_blk == 0, WINDOW, 0)
    att = [[None] * (tb // WINDOW) for _ in range(ATT_HEADS)]
    for n in range(tb // WINDOW):
        rows_n = slice(n * WINDOW, (n + 1) * WINDOW)
        keys_n = slice(n * WINDOW, (n + 2) * WINDOW)
        q_stack = jnp.concatenate([q[rows_n] for q in q_ext], axis=0)
        s_all = _dot_nt(q_stack, k_all[keys_n])
        allowed = in_window & (col >= first_lo) if n == 0 else in_window
        soft = [_softmax_sink(s_all[head * WINDOW:(head + 1) * WINDOW], allowed, sinks_ref[head])
                for head in range(ATT_HEADS)]
        o_all = _dot(jnp.concatenate([p for p, _ in soft], axis=0), v_all[keys_n])
        for head in range(ATT_HEADS):
            att[head][n] = o_all[head * WINDOW:(head + 1) * WINDOW] * soft[head][1]
    for hp in range(ATT_HEADS // 2):
        a, b = (jnp.concatenate(att[2 * hp + sub], axis=0) for sub in range(2))
        mix_ref[:, hp * LANES:(hp + 1) * LANES] = _merge_head_pair(a, b, (2 * hp) // GROUP).astype(BF16)
    if with_tail:
        acc = _mlp_block(acc, h2, 0, wup_ref, wdn_ref)

    lb = _lower_bound(lb_ref[...], layer)
    n_chunks = tb // HG_CHUNK
    gate_parts = []
    for c in range(n_chunks):
        log_f, k_c = _hgrn_gates(proj_ref[c * HG_CHUNK:(c + 1) * HG_CHUNK, OFF_HF:OFF_HF + HG_WIDTH], lb)
        gate_parts.append((k_c, _cumsum_rows(tri_ref[:HG_CHUNK, :HG_CHUNK], log_f)))
    k_in = jnp.concatenate([k for k, _ in gate_parts], axis=0)
    b_all = jnp.concatenate([b for _, b in gate_parts], axis=0)
    onorm = onorm_ref[...]
    units = [(hh, c) for c in range(n_chunks) for hh in range(HG_HEADS)]
    q_in, v_in = [], []
    for hh in range(HG_HEADS):
        q_in.append(_silu(proj_ref[:, OFF_HQ + hh * HG_DK:OFF_HQ + (hh + 1) * HG_DK]))
        v_in.append(proj_ref[:, OFF_HI + hh * HG_DV:OFF_HI + (hh + 1) * HG_DV])

    def piece(x, hh, c, lanes=False):
        x = x[:, hh * HG_DK:(hh + 1) * HG_DK] if lanes else x[hh]
        return x[c * HG_CHUNK:(c + 1) * HG_CHUNK]

    intra_ops = {u: _hgrn_intra_operands(piece(q_in, *u), piece(k_in, *u, lanes=True),
                                         piece(b_all, *u, lanes=True)) for u in units}
    a_full = {u: [_dot_nt(qh, kh) for qh, kh in intra_ops[u]] for u in units}
    o_inter, st_new = {}, []
    for hh in range(HG_HEADS):
        st = st_ref[hh]
        for c in range(n_chunks):
            q, k, v, b = (piece(q_in, hh, c), piece(k_in, hh, c, True), piece(v_in, hh, c),
                          piece(b_all, hh, c, True))
            o_inter[(hh, c)] = _dot_nt((q * jnp.exp(b)).astype(BF16), st.astype(BF16))
            b_last = b[HG_CHUNK - 1:HG_CHUNK]
            k_hat = (k * jnp.exp(b_last - b)).astype(BF16)
            st = st * jnp.exp(b_last) + _dot(v.T.astype(BF16), k_hat)
        st_new.append(st)
    a_mask = {u: _hgrn_intra_scores(a_full[u]) for u in units}
    o_intra = {u: _dot(a_mask[u], piece(v_in, *u).astype(BF16)) for u in units}

    def emit_head(hh, o_intra_h):
        o = o_intra_h + jnp.concatenate([o_inter[(hh, c)] for c in range(n_chunks)], axis=0)
        gate = _silu(proj_ref[:, OFF_HG + hh * HG_DV:OFF_HG + (hh + 1) * HG_DV])
        mix_ref[:, ATT_WIDTH + hh * HG_DV:ATT_WIDTH + (hh + 1) * HG_DV] = (
            _rms(o, onorm) * gate).astype(BF16)

    for hh in range(HG_HEADS):
        emit_head(hh, jnp.concatenate([o_intra[(hh, c)] for c in range(n_chunks)], axis=0))
    if with_tail:
        acc = _mlp_block(acc, h2, 1, wup_ref, wdn_ref)
        y_ref[0] = _rms(acc, fnorm_ref[...]) if last else acc

    @pl.when(jnp.min(_hgrn_worst_subblock_decay(b_all)) < -HG_GUARD)
    def _():
        for hh in range(HG_HEADS):
            sl = slice(hh * HG_DK, (hh + 1) * HG_DK)
            emit_head(hh, _hgrn_intra_exact(q_in[hh], k_in[:, sl], v_in[hh], b_all[:, sl],
                                            work_ref, exact_ref))

    kprev_ref[...] = k_rot[tb - WINDOW:]
    vprev_ref[...] = v_new[tb - WINDOW:]
    nk_ref[0] = k_rot[tb - WINDOW:]
    nv_ref[0] = v_new[tb - WINDOW:]
    for hh in range(HG_HEADS):
        st_ref[hh] = st_new[hh]

    @pl.when(t_blk == n_tblk - 1)
    def _():
        for hh in range(HG_HEADS):
            ns_ref[0, hh] = st_new[hh].T


def _prompt_layer_kernel(layer, last, n_tblk, *refs):
    j = pl.program_id(0)
    n_blocks = pl.num_programs(0) - 1
    t_blk = lax.rem(j, n_tblk)

    def run(with_mixer, with_tail):
        _prompt_step(with_mixer, with_tail, layer, last, n_tblk, t_blk, *refs)

    pl.when(j == 0)(functools.partial(run, True, False))
    pl.when((j > 0) & (j < n_blocks))(functools.partial(run, True, True))
    pl.when(j == n_blocks)(functools.partial(run, False, True))


def _const_spec(shape):
    nd = len(shape)
    return pl.BlockSpec(shape, lambda *_: (0,) * nd, pipeline_mode=pl.Buffered(1))


def _layer_spec(layer, shape):
    nd = len(shape)
    return pl.BlockSpec((None,) + shape, lambda *_: (layer,) + (0,) * nd, pipeline_mode=pl.Buffered(1))


def _prompt_layer(layer, x, rope, tri, sinks, lb_all, anorm, w_in, onorm, w_o, mnorm, w_up, w_dn, fnorm):
    batch, seq, _ = x.shape
    tb = PROMPT_BLOCK
    n_tblk = seq // tb
    n_blocks = batch * n_tblk
    last = layer == DEPTH - 1
    mixer_blk = lambda j: jnp.minimum(j, n_blocks - 1)
    tail_blk = lambda j: jnp.maximum(j - 1, 0)
    rope_spec = pl.BlockSpec((tb, LANES), lambda j: (mixer_blk(j) % n_tblk, 0))
    carry_spec = pl.BlockSpec((1, WINDOW, KV_WIDTH), lambda j: (mixer_blk(j) // n_tblk, 0, 0))
    return pl.pallas_call(
        functools.partial(_prompt_layer_kernel, layer, last, n_tblk),
        grid=(n_blocks + 1,),
        in_specs=[
            pl.BlockSpec((1, tb, D_MODEL), lambda j: (mixer_blk(j) // n_tblk, mixer_blk(j) % n_tblk, 0)),
            rope_spec, rope_spec, rope_spec,
            _const_spec((tb, tb)),
            pl.BlockSpec(memory_space=pltpu.SMEM),
            _const_spec((DEPTH, HG_WIDTH)),
            _layer_spec(layer, (1, D_MODEL)),
            _const_spec((D_MODEL, IN_WIDTH)),
            _layer_spec(layer, (1, HG_DV)),
            _const_spec((MIX_WIDTH, D_MODEL)),
            _layer_spec(layer, (1, D_MODEL)),
            _const_spec((D_MODEL, D_FF)),
            _const_spec((D_FF, D_MODEL)),
            _const_spec((1, D_MODEL)),
        ],
        out_specs=[
            pl.BlockSpec((1, tb, D_MODEL), lambda j: (tail_blk(j) // n_tblk, tail_blk(j) % n_tblk, 0)),
            carry_spec, carry_spec,
            pl.BlockSpec((1, HG_HEADS, HG_DK, HG_DV), lambda j: (mixer_blk(j) // n_tblk, 0, 0, 0)),
        ],
        out_shape=[
            jax.ShapeDtypeStruct((batch, seq, D_MODEL), F32),
            jax.ShapeDtypeStruct((batch, WINDOW, KV_WIDTH), F32),
            jax.ShapeDtypeStruct((batch, WINDOW, KV_WIDTH), F32),
            jax.ShapeDtypeStruct((batch, HG_HEADS, HG_DK, HG_DV), F32),
        ],
        scratch_shapes=[
            pltpu.VMEM((tb, IN_WIDTH), F32),
            pltpu.VMEM((tb, MIX_WIDTH), BF16),
            pltpu.VMEM((tb, D_MODEL), F32),
            pltpu.VMEM((WINDOW, KV_WIDTH), F32),
            pltpu.VMEM((WINDOW, KV_WIDTH), F32),
            pltpu.VMEM((HG_HEADS, HG_DV, HG_DK), F32),
            pltpu.VMEM((4, tb, HG_DK), F32),
            pltpu.VMEM((tb, HG_DV), F32),
        ],
        compiler_params=pltpu.CompilerParams(
            dimension_semantics=("arbitrary",), vmem_limit_bytes=VMEM_LIMIT),
        name=f"prompt_layer{layer}",
    )(x, *rope, tri, sinks, lb_all, anorm, w_in, onorm, w_o, mnorm, w_up, w_dn, fnorm)


def _sample_in_kernel(x_ref, anorm_ref, win_ref, proj_ref, win_bf_ref, h_ref):
    @pl.when(pl.program_id(0) == 0)
    def _():
        h_ref[...] = _rms(x_ref[...], anorm_ref[...]).astype(BF16)

    win_bf_ref[...] = win_ref[...].astype(BF16)
    proj_ref[...] = _dot(h_ref[...], win_bf_ref[...])


def _sample_out_kernel(last, x_ref, mix_ref, wo_ref, mnorm_ref, wup_ref, wdn_ref, fnorm_ref,
                       y_ref, wo_bf_ref, wup_bf_ref, wdn_bf_ref, h2_ref, acc_ref):
    c = pl.program_id(0)

    @pl.when(c == 0)
    def _():
        wo_bf_ref[...] = wo_ref[...].astype(BF16)
        acc_ref[...], h2_ref[...] = _out_proj(x_ref[...], mix_ref[...].astype(BF16), wo_bf_ref, mnorm_ref)

    wup_bf_ref[...] = wup_ref[...].astype(BF16)
    wdn_bf_ref[...] = wdn_ref[...].astype(BF16)
    acc_ref[...] = _mlp_cols(acc_ref[...], h2_ref[...], wup_bf_ref[...], wdn_bf_ref[...])

    @pl.when(c == pl.num_programs(0) - 1)
    def _():
        y_ref[...] = _rms(acc_ref[...], fnorm_ref[...]) if last else acc_ref[...]


def _sample_mixer_kernel(layer, n_new, n_aliased, *refs):
    (p_ref, cos_ref, sdn_ref, sup_ref, tri_ref, rev_ref, sel_ref, ck_ref, cv_ref, s0_ref, sinks_ref, lb_ref,
     onorm_ref) = refs[:SAMPLE_MIXER_INPUTS]
    mix_ref, nk_ref, nv_ref, ns_ref, pad_ref = refs[SAMPLE_MIXER_INPUTS + n_aliased:]
    slot = 0 if n_aliased else layer
    for out_ref in (nk_ref, nv_ref, ns_ref):
        for other in range(out_ref.shape[0]):
            if other != slot:
                out_ref[other] = jnp.zeros(out_ref.shape[1:], F32)
    sb = ck_ref.shape[0]
    ts = p_ref.shape[0]
    tok_shift = n_new.bit_length() - 1
    key_shift = WINDOW.bit_length() - 1
    n_cache = sb * WINDOW
    cos, sdn, sup = cos_ref[...], sdn_ref[...], sup_ref[...]

    k_new = _rope(p_ref[:, OFF_K:OFF_K + KV_WIDTH], cos, sdn, sup)
    v_new = p_ref[:, OFF_V:OFF_V + KV_WIDTH]
    pad_keys = jnp.zeros((LANES - ts, KV_WIDTH), F32)
    k_all = jnp.concatenate([ck_ref[...].reshape(n_cache, KV_WIDTH), k_new, pad_keys], axis=0).astype(BF16)
    v_all = jnp.concatenate([cv_ref[...].reshape(n_cache, KV_WIDTH), v_new, pad_keys], axis=0).astype(BF16)
    n_keys = n_cache + LANES
    row = lax.broadcasted_iota(jnp.int32, (ts, n_keys), 0)
    col = lax.broadcasted_iota(jnp.int32, (ts, n_keys), 1)
    new_idx = col - n_cache
    cached_ok = ((col >> key_shift) == (row >> tok_shift)) & ((col & (WINDOW - 1)) >= (row & (n_new - 1)))
    new_ok = ((new_idx < ts) & ((new_idx >> tok_shift) == (row >> tok_shift))
              & ((new_idx & (n_new - 1)) <= (row & (n_new - 1))))
    allowed = ((col < n_cache) & cached_ok) | ((col >= n_cache) & new_ok)

    q_ext = _stacked_queries(p_ref, cos, sdn, sup)
    s_all = _dot_nt(jnp.concatenate(q_ext, axis=0), k_all)
    soft = [_softmax_sink(s_all[head * ts:(head + 1) * ts], allowed, sinks_ref[head])
            for head in range(ATT_HEADS)]
    o_all = _dot(jnp.concatenate([p for p, _ in soft], axis=0), v_all)
    att = [o_all[head * ts:(head + 1) * ts] * soft[head][1] for head in range(ATT_HEADS)]
    for hp in range(ATT_HEADS // 2):
        mix_ref[:, hp * LANES:(hp + 1) * LANES] = _merge_head_pair(
            att[2 * hp], att[2 * hp + 1], (2 * hp) // GROUP)
    for s in range(sb):
        nk_ref[slot, s, 0:WINDOW - n_new] = ck_ref[s, n_new:WINDOW]
        nv_ref[slot, s, 0:WINDOW - n_new] = cv_ref[s, n_new:WINDOW]
        nk_ref[slot, s, WINDOW - n_new:WINDOW] = k_new[s * n_new:(s + 1) * n_new]
        nv_ref[slot, s, WINDOW - n_new:WINDOW] = v_new[s * n_new:(s + 1) * n_new]

    lb = _lower_bound(lb_ref[...], layer)
    log_f, k_in = _hgrn_gates(p_ref[:, OFF_HF:OFF_HF + HG_WIDTH], lb)
    b = _cumsum_rows(tri_ref[...], log_f)
    later = _cumsum_rows(rev_ref[...], log_f)
    q_in = _silu(p_ref[:, OFF_HQ:OFF_HQ + HG_WIDTH])
    v_in = p_ref[:, OFF_HI:OFF_HI + HG_WIDTH]
    gate = _silu(p_ref[:, OFF_HG:OFF_HG + HG_WIDTH])
    onorm = onorm_ref[...]
    heads = [slice(hh * HG_DK, (hh + 1) * HG_DK) for hh in range(HG_HEADS)]

    pad_ref[...] = jnp.zeros_like(pad_ref)
    for i, x in enumerate((k_in, b, v_in)):
        pad_ref[i, SUBLANES:SUBLANES + ts] = x
    tok = lax.broadcasted_iota(jnp.int32, (ts, 1), 0) & (n_new - 1)
    o_intra = [jnp.zeros((ts, HG_DV), F32) for _ in heads]
    for d in range(n_new):
        k_d, b_d, v_d = (pad_ref[i, SUBLANES - d:SUBLANES - d + ts] for i in range(3))
        w = q_in * k_d * jnp.exp(jnp.minimum(b - b_d, 0.0))
        for hh, sl in enumerate(heads):
            a_col = jnp.sum(w[:, sl], axis=-1, keepdims=True)
            o_intra[hh] = o_intra[hh] + jnp.where(tok >= d, a_col, 0.0) * v_d[:, sl]

    q_dec = (q_in * jnp.exp(b)).astype(BF16)
    seq_of_row = lax.broadcasted_iota(jnp.int32, (ts, 1), 0) >> tok_shift
    for hh, sl in enumerate(heads):
        o = o_intra[hh]
        for s in range(sb):
            o_s = _dot(q_dec[:, sl], s0_ref[s, hh].astype(BF16))
            o = o + jnp.where(seq_of_row == s, o_s, 0.0)
        mix_ref[:, ATT_WIDTH + hh * HG_DV:ATT_WIDTH + (hh + 1) * HG_DV] = _rms(o, onorm) * gate[:, sl]

    k_hat = k_in * jnp.exp(later)
    e_b = jnp.exp(b)
    e1 = e_b.astype(BF16).astype(F32)
    e2 = (e_b - e1).astype(BF16).astype(F32)
    e3 = e_b - e1 - e2
    own = ((lax.broadcasted_iota(jnp.int32, (ts, sb * HG_DV), 1) >> key_shift)
           == (lax.broadcasted_iota(jnp.int32, (ts, sb * HG_DV), 0) >> tok_shift))
    no_v = jnp.zeros((3 * ts, sb * HG_DV), F32)
    for hh, sl in enumerate(heads):
        lhs = jnp.concatenate([k_hat[:, sl], e1[:, sl], e2[:, sl], e3[:, sl]], axis=0).T.astype(BF16)
        v_rep = jnp.concatenate([v_in[:, sl]] * sb, axis=1)
        v_diag = jnp.concatenate([jnp.where(own, v_rep, 0.0), no_v], axis=0).astype(BF16)
        upd = _dot(lhs, v_diag)
        decay = _dot(lhs, sel_ref[...])
        for s in range(sb):
            cols = slice(s * HG_DV, (s + 1) * HG_DV)
            ns_ref[slot, s, hh] = decay[:, cols] * s0_ref[s, hh] + upd[:, cols]


def _sample_layer(layer, x, rope, cache_k, cache_v, state, new_caches, sinks, lb_all, anorm, w_in, onorm, w_o,
                  mnorm, w_up, w_dn, fnorm):
    n_seq, n_new, _ = x.shape
    n_tok = n_seq * n_new
    ts = LANES // 4
    sb = ts // n_new
    assert sb * n_new == ts and n_seq % sb == 0 and n_new <= SUBLANES
    assert n_new & (n_new - 1) == 0 and WINDOW & (WINDOW - 1) == 0 and HG_DV == WINDOW
    last = layer == DEPTH - 1
    x2 = x.reshape(n_tok, D_MODEL)
    params = pltpu.CompilerParams(vmem_limit_bytes=VMEM_LIMIT)
    whole = _const_spec
    slab = functools.partial(_layer_spec, layer)

    proj, w_in_bf = pl.pallas_call(
        _sample_in_kernel,
        grid=(IN_WIDTH // SAMPLE_IN_BLOCK,),
        in_specs=[whole((n_tok, D_MODEL)), slab((1, D_MODEL)),
                  pl.BlockSpec((None, D_MODEL, SAMPLE_IN_BLOCK), lambda c: (layer, 0, c))],
        out_specs=[pl.BlockSpec((n_tok, SAMPLE_IN_BLOCK), lambda c: (0, c)),
                   pl.BlockSpec((D_MODEL, SAMPLE_IN_BLOCK), lambda c: (0, c))],
        out_shape=[jax.ShapeDtypeStruct((n_tok, IN_WIDTH), F32),
                   jax.ShapeDtypeStruct((D_MODEL, IN_WIDTH), BF16)],
        scratch_shapes=[pltpu.VMEM((n_tok, D_MODEL), BF16)],
        compiler_params=params,
        name=f"sample_in{layer}",
    )(x2, anorm, w_in)

    r = jnp.arange(ts)
    same_seq = (r[:, None] // n_new) == (r[None, :] // n_new)
    tri = (same_seq & (r[None, :] <= r[:, None])).astype(BF16)
    rev = (same_seq & (r[None, :] > r[:, None])).astype(BF16)
    c = jnp.arange(sb * HG_DV)
    is_last = ((r[:, None] % n_new) == n_new - 1) & ((r[:, None] // n_new) == (c[None, :] // HG_DV))
    sel = jnp.concatenate([jnp.zeros_like(is_last)] + [is_last] * 3, axis=0).astype(BF16)

    tok_spec = lambda width: pl.BlockSpec((ts, width), lambda i: (i, 0))
    seq_spec = lambda *tail: pl.BlockSpec((sb,) + tail, lambda i: (i,) + (0,) * len(tail))
    layer_spec = lambda *tail: pl.BlockSpec((None, sb) + tail, lambda i: (layer, i) + (0,) * len(tail))
    rope_spec = _const_spec((ts, LANES))
    first = not new_caches
    slab_spec = lambda *tail: pl.BlockSpec(
        (DEPTH if first else 1, sb) + tail, lambda i: (0 if first else layer, i) + (0,) * len(tail))
    n_in = SAMPLE_MIXER_INPUTS
    aliases = {n_in + i: 1 + i for i in range(len(new_caches))}
    stacked = [(DEPTH, n_seq, WINDOW, KV_WIDTH)] * 2 + [(DEPTH, n_seq, HG_HEADS, HG_DK, HG_DV)]
    mix, *new_caches = pl.pallas_call(
        functools.partial(_sample_mixer_kernel, layer, n_new, len(new_caches)),
        grid=(n_seq // sb,),
        in_specs=[
            tok_spec(IN_WIDTH),
            rope_spec, rope_spec, rope_spec,
            _const_spec((ts, ts)), _const_spec((ts, ts)), _const_spec((4 * ts, sb * HG_DV)),
            layer_spec(WINDOW, KV_WIDTH), layer_spec(WINDOW, KV_WIDTH),
            layer_spec(HG_HEADS, HG_DK, HG_DV),
            pl.BlockSpec(memory_space=pltpu.SMEM),
            _const_spec((DEPTH, HG_WIDTH)),
            _layer_spec(layer, (1, HG_DV)),
        ] + [pl.BlockSpec(memory_space=pl.ANY)] * len(new_caches),
        out_specs=[
            tok_spec(MIX_WIDTH),
            slab_spec(WINDOW, KV_WIDTH), slab_spec(WINDOW, KV_WIDTH),
            slab_spec(HG_HEADS, HG_DK, HG_DV),
        ],
        out_shape=[jax.ShapeDtypeStruct((n_tok, MIX_WIDTH), F32)]
        + [jax.ShapeDtypeStruct(shape, F32) for shape in stacked],
        input_output_aliases=aliases,
        scratch_shapes=[pltpu.VMEM((3, SUBLANES + ts, HG_WIDTH), F32)],
        compiler_params=pltpu.CompilerParams(
            dimension_semantics=("arbitrary",), vmem_limit_bytes=VMEM_LIMIT),
        name=f"sample_mixer{layer}",
    )(proj, *(jnp.tile(t, (sb, 1)) for t in rope), tri, rev, sel, cache_k, cache_v, state, sinks, lb_all,
      onorm, *new_caches)

    y, w_o_bf, w_up_bf, w_dn_bf = pl.pallas_call(
        functools.partial(_sample_out_kernel, last),
        grid=(D_FF // SAMPLE_FF_BLOCK,),
        in_specs=[whole((n_tok, D_MODEL)), whole((n_tok, MIX_WIDTH)), slab((MIX_WIDTH, D_MODEL)),
                  slab((1, D_MODEL)),
                  pl.BlockSpec((None, D_MODEL, SAMPLE_FF_BLOCK), lambda c: (layer, 0, c)),
                  pl.BlockSpec((None, SAMPLE_FF_BLOCK, D_MODEL), lambda c: (layer, c, 0)),
                  whole((1, D_MODEL))],
        out_specs=[pl.BlockSpec((n_tok, D_MODEL), lambda c: (0, 0)),
                   pl.BlockSpec((MIX_WIDTH, D_MODEL), lambda c: (0, 0)),
                   pl.BlockSpec((D_MODEL, SAMPLE_FF_BLOCK), lambda c: (0, c)),
                   pl.BlockSpec((SAMPLE_FF_BLOCK, D_MODEL), lambda c: (c, 0))],
        out_shape=[jax.ShapeDtypeStruct((n_tok, D_MODEL), F32),
                   jax.ShapeDtypeStruct((MIX_WIDTH, D_MODEL), BF16),
                   jax.ShapeDtypeStruct((D_MODEL, D_FF), BF16),
                   jax.ShapeDtypeStruct((D_FF, D_MODEL), BF16)],
        scratch_shapes=[pltpu.VMEM((n_tok, D_MODEL), BF16), pltpu.VMEM((n_tok, D_MODEL), F32)],
        compiler_params=params,
        name=f"sample_out{layer}",
    )(x2, mix, w_o, mnorm, w_up, w_dn, fnorm)
    return y.reshape(n_seq, n_new, D_MODEL), new_caches, (w_in_bf, w_o_bf, w_up_bf, w_dn_bf)


def _rope_tables(pos):
    n = pos.shape[0]
    inv_freq = jnp.power(ROPE_THETA, -jnp.arange(ROT_HALF, dtype=F32) * (2.0 / ROT_DIM))
    ang = pos.astype(F32)[:, None] * inv_freq[None, :]
    cos, sin = jnp.cos(ang), jnp.sin(ang)
    rest = HEAD_DIM - ROT_DIM
    zeros_h = jnp.zeros((n, ROT_HALF), F32)
    cos_t = jnp.concatenate([cos, cos, jnp.ones((n, rest), F32)], axis=1)
    sdn_t = jnp.concatenate([zeros_h, sin, jnp.zeros((n, rest), F32)], axis=1)
    sup_t = jnp.concatenate([-sin, zeros_h, jnp.zeros((n, rest), F32)], axis=1)
    reps = LANES // HEAD_DIM
    return tuple(jnp.tile(t, (1, reps)) for t in (cos_t, sdn_t, sup_t))


def _chunk_tri(n, chunk):
    r = jnp.arange(n)
    same = (r[:, None] // chunk) == (r[None, :] // chunk)
    return (same & (r[None, :] <= r[:, None])).astype(BF16)


def kernel(x_prompt, x_sample, cache_k, cache_v, state_hgrn, attn_norm, w_in, att_sinks, hgrn_lower_bounds,
           hgrn_out_norm, w_o, mlp_norm, w_up, w_down, final_norm):
    batch, seq, _ = x_prompt.shape
    n_seq, n_new, _ = x_sample.shape
    assert seq % PROMPT_BLOCK == 0 and PROMPT_BLOCK % HG_CHUNK == 0

    rope_p = _rope_tables(jnp.arange(seq, dtype=jnp.int32))
    rope_s = _rope_tables(PAST_LEN + jnp.arange(n_new, dtype=jnp.int32))
    tri = _chunk_tri(PROMPT_BLOCK, HG_CHUNK)
    lb_all = hgrn_lower_bounds.astype(F32)
    fnorm = final_norm.reshape(1, D_MODEL)

    ck = cache_k.reshape(DEPTH, n_seq, WINDOW, KV_WIDTH)
    cv = cache_v.reshape(DEPTH, n_seq, WINDOW, KV_WIDTH)
    anorm, onorm, mnorm = (attn_norm.reshape(DEPTH, 1, D_MODEL), hgrn_out_norm.reshape(DEPTH, 1, HG_DV),
                           mlp_norm.reshape(DEPTH, 1, D_MODEL))
    xp, xs = x_prompt, x_sample
    nk_p, nv_p, ns_p, new_caches = [], [], [], []
    for l in range(DEPTH):
        xs, new_caches, (w_in_bf, w_o_bf, w_up_bf, w_dn_bf) = _sample_layer(
            l, xs, rope_s, ck, cv, state_hgrn, new_caches, att_sinks[l], lb_all, anorm, w_in, onorm, w_o, mnorm,
            w_up, w_down, fnorm)
        xp, k1, v1, s1 = _prompt_layer(l, xp, rope_p, tri, att_sinks[l], lb_all, anorm, w_in_bf, onorm, w_o_bf,
                                       mnorm, w_up_bf, w_dn_bf, fnorm)
        nk_p.append(k1), nv_p.append(v1), ns_p.append(s1)
    nk_s, nv_s, ns_s = new_caches

    kv_p = (DEPTH, batch, WINDOW, ATT_KV_HEADS, HEAD_DIM)
    kv_s = (DEPTH, n_seq, WINDOW, ATT_KV_HEADS, HEAD_DIM)
    return (xp, xs,
            jnp.stack(nk_p).reshape(kv_p), jnp.stack(nv_p).reshape(kv_p), jnp.stack(ns_p),
            nk_s.reshape(kv_s), nv_s.reshape(kv_s), ns_s)
```
